```python
import math, functools
import jax, jax.numpy as jnp
from jax import lax
import numpy as np

D_MODEL = 4096
BATCH = 1
SEQ = 8192
DEPTH = 1
DEC_BATCH = 32
DEC_SEQ = 1
PAST_LEN = 8192
PAGE_SIZE = 128

H_A = 16
DK_A = 128
DV_A = 128
GDN_CONV_W = 4
GDN_CHUNK = 64
D_QKV_A = 2 * H_A * DK_A + H_A * DV_A
H_B = 8
DQK_B = 128
DV_B = 2 * DQK_B
Q_BLOCK = 128
D_FF = 11008
FFN_CONV_W = 3
EPS = 1e-6
IN_WIDTHS = (D_QKV_A, H_A * DV_A, H_A, H_A, H_B * 2 * DQK_B, H_B * 2 * DQK_B, H_B * DV_B, D_MODEL, D_MODEL)
N_IN = sum(IN_WIDTHS)

kernel_name = 'hybrid_gdn_diffattn_convffn_step'


def rmsnorm(x, g, eps=EPS):
    xf = x.astype(jnp.float32)
    return xf * lax.rsqrt(jnp.mean(xf * xf, axis=-1, keepdims=True) + eps) * g.astype(jnp.float32)


def l2norm(x, eps=EPS):
    return x * lax.rsqrt(jnp.sum(x * x, axis=-1, keepdims=True) + eps)


def causal_dwconv(x, buf, w):
    width = w.shape[0]
    L = x.shape[1]
    xp = jnp.concatenate([buf.astype(x.dtype), x], axis=1)
    y = xp[:, 0:L] * w[0]
    for j in range(1, width):
        y = y + xp[:, j:j + L] * w[j]
    return y, xp[:, L:]


def gated_delta_rule(q, k, v, beta, g, s0):
    B, L, H, DK = q.shape
    DV = v.shape[-1]
    C = math.gcd(L, GDN_CHUNK)
    N = L // C

    def blocks(t):
        return jnp.moveaxis(t.reshape((B, N, C) + t.shape[2:]), 3, 1)

    q, k, v, beta, g = blocks(q), blocks(k), blocks(v), blocks(beta), blocks(g)
    gc = jnp.cumsum(g, axis=-1)
    diff = gc[..., :, None] - gc[..., None, :]
    idx = jnp.arange(C)
    incl = idx[:, None] >= idx[None, :]
    strict = idx[:, None] > idx[None, :]
    dec_incl = jnp.exp(jnp.where(incl, diff, -jnp.inf))
    dec_strict = jnp.where(strict, dec_incl, 0.0)
    kb = k * beta[..., None]
    m = jnp.einsum('bhntd,bhnsd->bhnts', kb, k) * dec_strict
    lhs = m + jnp.eye(C, dtype=m.dtype)
    rhs = jnp.concatenate([v * beta[..., None], kb * jnp.exp(gc)[..., None]], axis=-1)
    sol = lax.linalg.triangular_solve(lhs, rhs, left_side=True, lower=True, unit_diagonal=True)
    u_v, w = sol[..., :DV], sol[..., DV:]
    a_qk = jnp.einsum('bhntd,bhnsd->bhnts', q, k) * dec_incl
    q_dec = q * jnp.exp(gc)[..., None]
    k_dec = k * jnp.exp(gc[..., -1:] - gc)[..., None]
    g_last = jnp.exp(gc[..., -1])
    xs = tuple(jnp.moveaxis(t, 2, 0) for t in (u_v, w, a_qk, q_dec, k_dec, g_last))

    def step(s, inp):
        u_v_c, w_c, a_c, q_c, k_c, gl_c = inp
        u = u_v_c - jnp.einsum('bhcd,bhde->bhce', w_c, s)
        o = jnp.einsum('bhcd,bhde->bhce', q_c, s) + jnp.einsum('bhts,bhse->bhte', a_c, u)
        s = gl_c[..., None, None] * s + jnp.einsum('bhcd,bhce->bhde', k_c, u)
        return s, o

    s_final, o = lax.scan(step, s0.astype(jnp.float32), xs)
    o = jnp.transpose(o, (1, 0, 3, 2, 4)).reshape(B, L, H, DV)
    return o, s_final


def diff_attention_prompt(q, k, v, lam):
    B, L, H, _, DQ = q.shape
    nb = L // Q_BLOCK
    scale = DQ ** -0.5
    qblk = jnp.moveaxis(q.reshape(B, nb, Q_BLOCK, H, 2, DQ), 1, 0)
    kpos = jnp.arange(L)

    def one_block(args):
        i, qi = args
        s = jnp.einsum('bqhcd,bkhcd->bhcqk', qi, k) * scale
        qpos = i * Q_BLOCK + jnp.arange(Q_BLOCK)
        s = jnp.where(kpos[None, :] <= qpos[:, None], s, -jnp.inf)
        p = jax.nn.softmax(s, axis=-1)
        a = p[:, :, 0] - lam * p[:, :, 1]
        return jnp.einsum('bhqk,bkhe->bqhe', a, v)

    o = lax.map(one_block, (jnp.arange(nb), qblk))
    return jnp.moveaxis(o, 0, 1).reshape(B, L, H, v.shape[-1])


def diff_attention_sample(q, k, v, lam, k_past, v_past):
    B, S, H, _, DQ = q.shape
    P = k_past.shape[1]
    scale = DQ ** -0.5
    kp = k_past.reshape(B, P, H, 2, DQ).astype(jnp.float32)
    s_past = jnp.einsum('bqhcd,bphcd->bhcqp', q, kp) * scale
    s_new = jnp.einsum('bqhcd,bkhcd->bhcqk', q, k) * scale
    causal = jnp.arange(S)[:, None] >= jnp.arange(S)[None, :]
    s_new = jnp.where(causal, s_new, -jnp.inf)
    p = jax.nn.softmax(jnp.concatenate([s_past, s_new], axis=-1), axis=-1)
    a = p[:, :, 0] - lam * p[:, :, 1]
    return (jnp.einsum('bhqp,bphe->bqhe', a[..., :P], v_past.astype(jnp.float32))
            + jnp.einsum('bhqk,bkhe->bqhe', a[..., P:], v))


def hybrid_layer(x, c, lam_init, wl, gdn_conv_buf, gdn_s0, ffn_conv_buf, attend):
    (w_ada, b_ada, norm1_g, norm2_g, w_in, gdn_conv_w, gdn_a_log, gdn_dt_bias, gdn_norm_g,
     diff_q_norm_g, diff_k_norm_g, diff_lambda, diff_subln_g, w_branch_a, w_branch_b, w_o,
     w_up, ffn_conv_w, ffn_conv_b, w_down) = wl
    B, L, _ = x.shape
    x = x.astype(jnp.float32)
    mod = jnp.dot(jax.nn.silu(c.astype(jnp.float32)), w_ada) + b_ada
    sh1, sc1, gt1, sh2, sc2, gt2 = jnp.split(mod[:, None, :], 6, axis=-1)

    h = rmsnorm(x, norm1_g) * (1.0 + sc1) + sh1
    z = h @ w_in
    points = np.cumsum(IN_WIDTHS)[:-1].tolist()
    qkv_a, z_a, b_a, a_a, q_b, k_b, v_b, gate_a, gate_b = jnp.split(z, points, axis=-1)

    conv_a, gdn_buf_new = causal_dwconv(qkv_a, gdn_conv_buf, gdn_conv_w)
    conv_a = jax.nn.silu(conv_a)
    qa, ka, va = jnp.split(conv_a, [H_A * DK_A, 2 * H_A * DK_A], axis=-1)
    qa = l2norm(qa.reshape(B, L, H_A, DK_A)) * (DK_A ** -0.5)
    ka = l2norm(ka.reshape(B, L, H_A, DK_A))
    va = va.reshape(B, L, H_A, DV_A)
    beta = jax.nn.sigmoid(b_a)
    g = -jnp.exp(gdn_a_log) * jax.nn.softplus(a_a + gdn_dt_bias)
    oa, s_new = gated_delta_rule(qa, ka, va, beta, g, gdn_s0)
    oa = rmsnorm(oa, gdn_norm_g) * jax.nn.silu(z_a.reshape(B, L, H_A, DV_A))

    qb = rmsnorm(q_b.reshape(B, L, H_B, 2, DQK_B), diff_q_norm_g)
    kb = rmsnorm(k_b.reshape(B, L, H_B, 2, DQK_B), diff_k_norm_g)
    vb = v_b.reshape(B, L, H_B, DV_B)
    lq1, lk1, lq2, lk2 = diff_lambda.astype(jnp.float32)
    lam = jnp.exp(jnp.sum(lq1 * lk1)) - jnp.exp(jnp.sum(lq2 * lk2)) + lam_init
    ob = attend(qb, kb, vb, lam)
    ob = rmsnorm(ob, diff_subln_g, 1e-5) * (1.0 - lam_init)

    mixed = (jax.nn.sigmoid(gate_a) * (oa.reshape(B, L, H_A * DV_A) @ w_branch_a)
             + jax.nn.sigmoid(gate_b) * (ob.reshape(B, L, H_B * DV_B) @ w_branch_b))
    x = x + gt1 * (mixed @ w_o)

    h2 = rmsnorm(x, norm2_g) * (1.0 + sc2) + sh2
    u, ffn_buf_new = causal_dwconv(h2 @ w_up, ffn_conv_buf, ffn_conv_w)
    u = u + ffn_conv_b
    gate_f, val_f = jnp.split(u, 2, axis=-1)
    x = x + gt2 * ((jax.nn.silu(gate_f) * val_f) @ w_down)
    return x, kb.reshape(B, L, H_B, 2 * DQK_B), vb, s_new, gdn_buf_new, ffn_buf_new


def setup_inputs(seed: int = 0) -> dict:
    key = jax.random.key(seed)
    ks = jax.random.split(key, 32)
    f32 = jnp.float32
    n_pages = PAST_LEN // PAGE_SIZE
    n_used = DEC_BATCH * n_pages
    n_phys = n_used + (n_used + 3) // 4

    def nrm(k, shape, s):
        return jax.random.normal(k, shape, f32) * s

    x_prompt = nrm(ks[0], (BATCH, SEQ, D_MODEL), 1.0)
    x_sample = nrm(ks[1], (DEC_BATCH, DEC_SEQ, D_MODEL), 1.0)
    c_prompt = nrm(ks[2], (BATCH, D_MODEL), 1.0)
    c_sample = nrm(ks[3], (DEC_BATCH, D_MODEL), 1.0)
    cache_k = nrm(ks[4], (DEPTH, n_phys, PAGE_SIZE, H_B, 2 * DQK_B), 1.0)
    cache_v = nrm(ks[5], (DEPTH, n_phys, PAGE_SIZE, H_B, DV_B), 1.0)
    state_gdn = nrm(ks[6], (DEPTH, DEC_BATCH, H_A, DK_A, DV_A), DK_A ** -0.5)
    state_gdn_conv = nrm(ks[7], (DEPTH, DEC_BATCH, GDN_CONV_W - 1, D_QKV_A), 1.0)
    state_ffn_conv = nrm(ks[8], (DEPTH, DEC_BATCH, FFN_CONV_W - 1, 2 * D_FF), 1.0)
    page_table = jax.random.permutation(ks[9], n_phys)[:n_used].reshape(DEC_BATCH, n_pages).astype(jnp.int32)

    w_ada = nrm(ks[10], (DEPTH, D_MODEL, 6 * D_MODEL), 0.5 * D_MODEL ** -0.5)
    b_ada = nrm(ks[11], (DEPTH, 6 * D_MODEL), 0.01)
    norm1_g = 1.0 + nrm(ks[12], (DEPTH, D_MODEL), 0.05)
    norm2_g = 1.0 + nrm(ks[13], (DEPTH, D_MODEL), 0.05)
    w_in = nrm(ks[14], (DEPTH, D_MODEL, N_IN), D_MODEL ** -0.5)
    gdn_conv_w = nrm(ks[15], (DEPTH, GDN_CONV_W, D_QKV_A), GDN_CONV_W ** -0.5)
    gdn_a_log = jnp.log(jax.random.uniform(ks[16], (DEPTH, H_A), f32, 1.0, 16.0))
    dt = jnp.exp(jax.random.uniform(ks[17], (DEPTH, H_A), f32, math.log(1e-3), math.log(1e-1)))
    gdn_dt_bias = dt + jnp.log(-jnp.expm1(-dt))
    gdn_norm_g = 1.0 + nrm(ks[18], (DEPTH, DV_A), 0.05)
    diff_q_norm_g = 1.0 + nrm(ks[19], (DEPTH, DQK_B), 0.05)
    diff_k_norm_g = 1.0 + nrm(ks[20], (DEPTH, DQK_B), 0.05)
    diff_lambda = nrm(ks[21], (DEPTH, 4, DQK_B), 0.1)
    diff_subln_g = 1.0 + nrm(ks[22], (DEPTH, DV_B), 0.05)
    w_branch_a = nrm(ks[23], (DEPTH, H_A * DV_A, D_MODEL), (H_A * DV_A) ** -0.5)
    w_branch_b = nrm(ks[24], (DEPTH, H_B * DV_B, D_MODEL), (H_B * DV_B) ** -0.5)
    w_o = nrm(ks[25], (DEPTH, D_MODEL, D_MODEL), D_MODEL ** -0.5)
    w_up = nrm(ks[26], (DEPTH, D_MODEL, 2 * D_FF), D_MODEL ** -0.5)
    ffn_conv_w = nrm(ks[27], (DEPTH, FFN_CONV_W, 2 * D_FF), FFN_CONV_W ** -0.5)
    ffn_conv_b = nrm(ks[28], (DEPTH, 2 * D_FF), 0.01)
    w_down = nrm(ks[29], (DEPTH, D_FF, D_MODEL), D_FF ** -0.5)
    return {'x_prompt': x_prompt, 'x_sample': x_sample, 'c_prompt': c_prompt, 'c_sample': c_sample,
            'cache_k': cache_k, 'cache_v': cache_v, 'state_gdn': state_gdn,
            'state_gdn_conv': state_gdn_conv, 'state_ffn_conv': state_ffn_conv, 'page_table': page_table,
            'w_ada': w_ada, 'b_ada': b_ada, 'norm1_g': norm1_g, 'norm2_g': norm2_g, 'w_in': w_in,
            'gdn_conv_w': gdn_conv_w, 'gdn_a_log': gdn_a_log, 'gdn_dt_bias': gdn_dt_bias,
            'gdn_norm_g': gdn_norm_g, 'diff_q_norm_g': diff_q_norm_g, 'diff_k_norm_g': diff_k_norm_g,
            'diff_lambda': diff_lambda, 'diff_subln_g': diff_subln_g, 'w_branch_a': w_branch_a,
            'w_branch_b': w_branch_b, 'w_o': w_o, 'w_up': w_up, 'ffn_conv_w': ffn_conv_w,
            'ffn_conv_b': ffn_conv_b, 'w_down': w_down}


def reference(x_prompt, x_sample, c_prompt, c_sample, cache_k, cache_v, state_gdn, state_gdn_conv,
              state_ffn_conv, page_table, w_ada, b_ada, norm1_g, norm2_g, w_in, gdn_conv_w, gdn_a_log,
              gdn_dt_bias, gdn_norm_g, diff_q_norm_g, diff_k_norm_g, diff_lambda, diff_subln_g,
              w_branch_a, w_branch_b, w_o, w_up, ffn_conv_w, ffn_conv_b, w_down):
    f32 = jnp.float32
    b_p = x_prompt.shape[0]
    n_seq, n_pages = page_table.shape
    page = cache_k.shape[2]
    yp, ys = x_prompt, x_sample
    kp_l, vp_l, sp_l, gcp_l, fcp_l = [], [], [], [], []
    ks_l, vs_l, ss_l, gcs_l, fcs_l = [], [], [], [], []
    for l in range(DEPTH):
        lam_init = 0.8 - 0.6 * math.exp(-0.3 * l)
        wl = (w_ada[l], b_ada[l], norm1_g[l], norm2_g[l], w_in[l], gdn_conv_w[l], gdn_a_log[l],
              gdn_dt_bias[l], gdn_norm_g[l], diff_q_norm_g[l], diff_k_norm_g[l], diff_lambda[l],
              diff_subln_g[l], w_branch_a[l], w_branch_b[l], w_o[l], w_up[l], ffn_conv_w[l],
              ffn_conv_b[l], w_down[l])
        yp, k_p, v_p, s_p, gc_p, fc_p = hybrid_layer(
            yp, c_prompt, lam_init, wl,
            jnp.zeros((b_p, GDN_CONV_W - 1, D_QKV_A), f32),
            jnp.zeros((b_p, H_A, DK_A, DV_A), f32),
            jnp.zeros((b_p, FFN_CONV_W - 1, 2 * D_FF), f32),
            diff_attention_prompt)
        k_past = cache_k[l][page_table].reshape(n_seq, n_pages * page, H_B, 2 * DQK_B)
        v_past = cache_v[l][page_table].reshape(n_seq, n_pages * page, H_B, DV_B)
        attend_s = functools.partial(diff_attention_sample, k_past=k_past, v_past=v_past)
        ys, k_s, v_s, s_s, gc_s, fc_s = hybrid_layer(
            ys, c_sample, lam_init, wl, state_gdn_conv[l], state_gdn[l], state_ffn_conv[l], attend_s)
        kp_l.append(k_p); vp_l.append(v_p); sp_l.append(s_p); gcp_l.append(gc_p); fcp_l.append(fc_p)
        ks_l.append(k_s); vs_l.append(v_s); ss_l.append(s_s); gcs_l.append(gc_s); fcs_l.append(fc_s)
    y_prompt = yp.astype(x_prompt.dtype)
    y_sample = ys.astype(x_sample.dtype)
    return (y_prompt, y_sample,
            jnp.stack(kp_l), jnp.stack(vp_l), jnp.stack(sp_l), jnp.stack(gcp_l), jnp.stack(fcp_l),
            jnp.stack(ks_l), jnp.stack(vs_l), jnp.stack(ss_l), jnp.stack(gcs_l), jnp.stack(fcs_l))
```

```python
import functools
import math

import jax
import jax.numpy as jnp
from jax import lax
from jax.experimental import pallas as pl
from jax.experimental.pallas import tpu as pltpu

F32 = jnp.float32
BF16 = jnp.bfloat16

LANES = 128
SUBLANES = 8
VMEM_LIMIT_BYTES = 56 * 1024 * 1024

EPS = 1e-6
SUBLN_EPS = 1e-5
GDN_CHUNK = 128
NEG_BIG = -1e30

TILES = dict(
    norm_rows=256,
    mm_m=1024, mm_n=512,
    up_m=512, up_n=512,
    down_k=2816,
    attn=1024,
    gdn_heads=4,
    small_n=1024,
)

NT_DIMS = (((1,), (1,)), ((), ()))


def _params(*sem):
    return pltpu.CompilerParams(dimension_semantics=sem, vmem_limit_bytes=VMEM_LIMIT_BYTES)


def _tile(dim, pref, align):
    if dim <= pref:
        return dim
    t = (pref // align) * align
    while t >= align:
        if dim % t == 0:
            return t
        t -= align
    return dim


def _sigmoid(x):
    return 1.0 / (1.0 + jnp.exp(-x))


def _silu(x):
    return x * _sigmoid(x)


def _softplus(x):
    return jnp.maximum(x, 0.0) + jnp.log1p(jnp.exp(-jnp.abs(x)))


def _normmod_kernel(x_ref, g_ref, sc_ref, sh_ref, o_ref):
    x = x_ref[...]
    ms = jnp.mean(x * x, axis=-1, keepdims=True)
    h = x * lax.rsqrt(ms + EPS) * g_ref[...]
    o_ref[...] = (h * (1.0 + sc_ref[...]) + sh_ref[...]).astype(o_ref.dtype)


def _normmod(x, g, sc, sh):
    m, d = x.shape
    tm = _tile(m, TILES["norm_rows"], SUBLANES)
    per_row = sc.shape[0] != 1
    mod_spec = (pl.BlockSpec((tm, d), lambda i: (i, 0)) if per_row
                else pl.BlockSpec((1, d), lambda i: (0, 0)))
    return pl.pallas_call(
        _normmod_kernel,
        grid=(m // tm,),
        in_specs=[pl.BlockSpec((tm, d), lambda i: (i, 0)),
                  pl.BlockSpec((1, d), lambda i: (0, 0)), mod_spec, mod_spec],
        out_specs=pl.BlockSpec((tm, d), lambda i: (i, 0)),
        out_shape=jax.ShapeDtypeStruct((m, d), BF16),
        compiler_params=_params("parallel"),
    )(x, g, sc, sh)


def _matmul(as_, ws, dots, n, epilogue, outs, *, tm, tn, extras=(), order="mn",
            tk=None, a_fn=None, carry_rows=0):
    m = as_[0].shape[0]
    kdim = as_[0].shape[1]
    tk = kdim if tk is None else tk
    nk = kdim // tk
    assert kdim % tk == 0 and m % tm == 0 and n % tn == 0
    if nk > 1:
        assert all(a.shape[1] == kdim for a in as_)
    ni, nj = m // tm, n // tn
    na, nw, ne, no, nd = len(as_), len(ws), len(extras), len(outs), len(dots)

    if order == "mn":
        grid = (ni, nj, nk)
        ij = lambda g0, g1: (g0, g1)
    else:
        grid = (nj, ni, nk)
        ij = lambda g0, g1: (g1, g0)

    def a_map(g0, g1, k):
        i, _ = ij(g0, g1)
        return (i, k)

    def w_map(off):
        def f(g0, g1, k):
            _, j = ij(g0, g1)
            return (k, j + off // tn)
        return f

    def row_map(off):
        def f(g0, g1, k):
            _, j = ij(g0, g1)
            return (0, j + off // tn)
        return f

    def tile_map(off):
        def f(g0, g1, k):
            i, j = ij(g0, g1)
            return (i, j + off // tn)
        return f

    in_specs = []
    for a in as_:
        in_specs.append(pl.BlockSpec((tm, tk if nk > 1 else a.shape[1]), a_map))
    for w, off in ws:
        assert off % tn == 0
        in_specs.append(pl.BlockSpec((tk if nk > 1 else w.shape[0], tn), w_map(off)))
    for arr, kind, off in extras:
        assert off % tn == 0
        if kind == "row":
            in_specs.append(pl.BlockSpec((1, tn), row_map(off)))
        else:
            in_specs.append(pl.BlockSpec((tm, tn), tile_map(off)))
    out_specs, out_shapes = [], []
    for dtype, kind in outs:
        if kind == "tile":
            out_specs.append(pl.BlockSpec((tm, tn), tile_map(0)))
            out_shapes.append(jax.ShapeDtypeStruct((m, n), dtype))
        else:
            out_specs.append(pl.BlockSpec((SUBLANES, tn), row_map(0)))
            out_shapes.append(jax.ShapeDtypeStruct((SUBLANES, n), dtype))
    scratch = []
    if nk > 1:
        scratch += [pltpu.VMEM((tm, tn), F32) for _ in range(nd)]
    if carry_rows:
        scratch += [pltpu.VMEM((SUBLANES, tn), F32) for _ in range(carry_rows)]

    def body(*refs):
        a_refs = refs[:na]
        w_refs = refs[na:na + nw]
        e_refs = refs[na + nw:na + nw + ne]
        o_refs = refs[na + nw + ne:na + nw + ne + no]
        s_refs = refs[na + nw + ne + no:]
        acc_refs = s_refs[:nd] if nk > 1 else ()
        carry_refs = s_refs[nd:] if nk > 1 else s_refs
        i, _ = ij(pl.program_id(0), pl.program_id(1))
        k = pl.program_id(2)

        def partial(d):
            ai, wi = dots[d]
            a = a_refs[ai][...]
            if a_fn is not None:
                a = a_fn(a)
            return jnp.dot(a.astype(BF16), w_refs[wi][...].astype(BF16),
                           preferred_element_type=F32)

        if nk == 1:
            epilogue([partial(d) for d in range(nd)], e_refs, o_refs, carry_refs, i, ni)
        else:
            @pl.when(k == 0)
            def _():
                for d in range(nd):
                    acc_refs[d][...] = jnp.zeros((tm, tn), F32)

            for d in range(nd):
                acc_refs[d][...] += partial(d)

            @pl.when(k == nk - 1)
            def _():
                epilogue([acc_refs[d][...] for d in range(nd)], e_refs, o_refs,
                         carry_refs, i, ni)

    res = pl.pallas_call(
        body,
        grid=grid,
        in_specs=in_specs,
        out_specs=out_specs,
        out_shape=out_shapes,
        scratch_shapes=scratch,
        compiler_params=_params("arbitrary", "arbitrary", "arbitrary"),
    )(*as_, *[w for w, _ in ws], *[e for e, _, _ in extras])
    return res


def _conv_taps(raw, carry_ref, taps, i, init_rows):
    width = len(taps)

    @pl.when(i == 0)
    def _():
        carry_ref[...] = init_rows

    prev = carry_ref[...]
    rowid = lax.broadcasted_iota(jnp.int32, prev.shape, 0)
    y = raw * taps[width - 1]
    head = raw[0:SUBLANES] * taps[width - 1]
    for s in range(1, width):
        tap = taps[width - 1 - s]
        shifted = pltpu.roll(raw, s, axis=0)
        y = y + shifted * tap
        head = head + jnp.where(rowid < s, pltpu.roll(prev, s, axis=0),
                                shifted[0:SUBLANES]) * tap
    carry_ref[...] = raw[raw.shape[0] - SUBLANES:]
    return y, head


def _gdn_prep_kernel(ba_ref, alog_ref, dtb_ref, beta_ref, gc_ref, gct_ref, *, heads_pad):
    c = GDN_CHUNK
    beta_ref[...] = _sigmoid(ba_ref[:, 0:LANES])
    g = -jnp.exp(alog_ref[...]) * _softplus(ba_ref[:, LANES:2 * LANES] + dtb_ref[...])
    row = lax.broadcasted_iota(jnp.int32, (c, LANES), 0)
    s = 1
    while s < c:
        g = g + jnp.where(row >= s, pltpu.roll(g, s, axis=0), 0.0)
        s *= 2
    gc_ref[...] = g
    gct_ref[...] = g.T[0:heads_pad, :]


def _gdn_intra_kernel(q_ref, k_ref, v_ref, beta_ref, gc_ref, gct_ref,
                      uv_ref, w_ref, aqk_ref, qdec_ref, kdect_ref):
    c = GDN_CHUNK
    h = pl.program_id(1)
    lane = lax.broadcasted_iota(jnp.int32, (c, LANES), 1)
    sel = lane == h
    beta_c = jnp.sum(jnp.where(sel, beta_ref[...], 0.0), axis=1, keepdims=True)
    gcc = jnp.sum(jnp.where(sel, gc_ref[...], 0.0), axis=1, keepdims=True)
    gcr = gct_ref[pl.ds(h, 1), :]
    gl = gcc[c - 1:c, :]
    row = lax.broadcasted_iota(jnp.int32, (c, c), 0)
    col = lax.broadcasted_iota(jnp.int32, (c, c), 1)
    dec_incl = jnp.exp(jnp.where(row >= col, gcc - gcr, NEG_BIG))
    dec_strict = jnp.where(row > col, dec_incl, 0.0)
    eye = jnp.where(row == col, 1.0, 0.0)

    q = q_ref[...]
    k = k_ref[...]
    kf = k.astype(F32)
    kb = kf * beta_c
    mm = lax.dot_general(kb.astype(BF16), k, NT_DIMS, preferred_element_type=F32) * dec_strict
    t = eye - jnp.where((row >> 1) == (col >> 1), mm, 0.0)
    for lvl in range(1, int(math.log2(c))):
        below = ((row >> (lvl + 1)) == (col >> (lvl + 1))) & ((row >> lvl) != (col >> lvl))
        tb = t.astype(BF16)
        bt = jnp.dot(jnp.where(below, mm, 0.0).astype(BF16), tb, preferred_element_type=F32)
        t = t - jnp.dot(tb, bt.astype(BF16), preferred_element_type=F32)
    tb = t.astype(BF16)
    egc = jnp.exp(gcc)
    vb = v_ref[...].astype(F32) * beta_c
    uv_ref[...] = jnp.dot(tb, vb.astype(BF16), preferred_element_type=F32)
    w_ref[...] = jnp.dot(tb, (kb * egc).astype(BF16), preferred_element_type=F32).astype(BF16)
    aqk = lax.dot_general(q, k, NT_DIMS, preferred_element_type=F32) * dec_incl
    aqk_ref[...] = aqk.astype(BF16)
    qdec_ref[...] = (q.astype(F32) * egc).astype(BF16)
    kdect_ref[...] = (kf * jnp.exp(gl - gcc)).T.astype(BF16)


def _gdn_rec_kernel(uv_ref, w_ref, aqk_ref, qdec_ref, kdect_ref, gct_ref, za_ref, gn_ref,
                    o_ref, sout_ref, s_ref, *, hb):
    c = GDN_CHUNK
    n = pl.program_id(1)
    g0 = pl.program_id(0)

    @pl.when(n == 0)
    def _():
        s_ref[...] = jnp.zeros(s_ref.shape, F32)

    for j in range(hb):
        sl = slice(j * LANES, (j + 1) * LANES)
        s = s_ref[j]
        sb = s.astype(BF16)
        u = uv_ref[:, sl] - jnp.dot(w_ref[:, sl], sb, preferred_element_type=F32)
        ub = u.astype(BF16)
        o = (jnp.dot(qdec_ref[:, sl], sb, preferred_element_type=F32)
             + jnp.dot(aqk_ref[:, sl], ub, preferred_element_type=F32))
        gl = jnp.exp(gct_ref[pl.ds(g0 * hb + j, 1), c - 1:c])
        s_ref[j] = gl * s + jnp.dot(kdect_ref[sl, :], ub, preferred_element_type=F32)
        ms = jnp.mean(o * o, axis=-1, keepdims=True)
        o_ref[:, sl] = (o * lax.rsqrt(ms + EPS) * gn_ref[...]
                        * za_ref[:, sl].astype(F32)).astype(o_ref.dtype)

    @pl.when(n == pl.num_programs(1) - 1)
    def _():
        sout_ref[...] = s_ref[...]


def _gdn_prompt(qkv, ba, za_silu, a_log, dt_bias, gnorm, h_a, dk):
    l = qkv.shape[0]
    c = GDN_CHUNK
    assert l % c == 0 and dk == LANES and h_a <= LANES
    nchunk = l // c
    hp = max(SUBLANES, -(-h_a // SUBLANES) * SUBLANES)
    alog_row = jnp.zeros((1, LANES), F32).at[0, :h_a].set(a_log)
    dtb_row = jnp.zeros((1, LANES), F32).at[0, :h_a].set(dt_bias)

    beta, gc, gct = pl.pallas_call(
        functools.partial(_gdn_prep_kernel, heads_pad=hp),
        grid=(nchunk,),
        in_specs=[pl.BlockSpec((c, 2 * LANES), lambda n: (n, 0)),
                  pl.BlockSpec((1, LANES), lambda n: (0, 0)),
                  pl.BlockSpec((1, LANES), lambda n: (0, 0))],
        out_specs=[pl.BlockSpec((c, LANES), lambda n: (n, 0)),
                   pl.BlockSpec((c, LANES), lambda n: (n, 0)),
                   pl.BlockSpec((hp, c), lambda n: (0, n))],
        out_shape=[jax.ShapeDtypeStruct((l, LANES), F32),
                   jax.ShapeDtypeStruct((l, LANES), F32),
                   jax.ShapeDtypeStruct((hp, l), F32)],
        compiler_params=_params("parallel"),
    )(ba, alog_row, dtb_row)

    hd = h_a * dk
    blk = lambda off: pl.BlockSpec((c, LANES), lambda n, h: (n, h + off))
    uv, w, aqk, qdec, kdect = pl.pallas_call(
        _gdn_intra_kernel,
        grid=(nchunk, h_a),
        in_specs=[blk(0), blk(h_a), blk(2 * h_a),
                  pl.BlockSpec((c, LANES), lambda n, h: (n, 0)),
                  pl.BlockSpec((c, LANES), lambda n, h: (n, 0)),
                  pl.BlockSpec((hp, c), lambda n, h: (0, n))],
        out_specs=[blk(0), blk(0), blk(0), blk(0),
                   pl.BlockSpec((LANES, c), lambda n, h: (h, n))],
        out_shape=[jax.ShapeDtypeStruct((l, hd), F32),
                   jax.ShapeDtypeStruct((l, hd), BF16),
                   jax.ShapeDtypeStruct((l, hd), BF16),
                   jax.ShapeDtypeStruct((l, hd), BF16),
                   jax.ShapeDtypeStruct((hd, l), BF16)],
        compiler_params=_params("parallel", "arbitrary"),
    )(qkv, qkv, qkv, beta, gc, gct)

    hb = _tile(h_a, TILES["gdn_heads"], 1)
    wide = lambda: pl.BlockSpec((c, hb * LANES), lambda g, n: (n, g))
    oa, s_fin = pl.pallas_call(
        functools.partial(_gdn_rec_kernel, hb=hb),
        grid=(h_a // hb, nchunk),
        in_specs=[wide(), wide(), wide(), wide(),
                  pl.BlockSpec((hb * LANES, c), lambda g, n: (g, n)),
                  pl.BlockSpec((hp, c), lambda g, n: (0, n)),
                  wide(),
                  pl.BlockSpec((1, LANES), lambda g, n: (0, 0))],
        out_specs=[wide(), pl.BlockSpec((hb, dk, LANES), lambda g, n: (g, 0, 0))],
        out_shape=[jax.ShapeDtypeStruct((l, hd), BF16),
                   jax.ShapeDtypeStruct((h_a, dk, LANES), F32)],
        scratch_shapes=[pltpu.VMEM((hb, dk, LANES), F32)],
        compiler_params=_params("parallel", "arbitrary"),
    )(uv, w, aqk, qdec, kdect, gct, za_silu, gnorm)
    return oa, s_fin


def _gdn_sample_kernel(z_ref, buf_ref, s_ref, cw_ref, alog_ref, dtb_ref, gn_ref,
                       o_ref, snew_ref, bufnew_ref, *, h_a, dk, width, c_ba):
    hd = h_a * dk
    raw = z_ref[0][:, 0:3 * hd]
    buf = buf_ref[0]
    y = raw * cw_ref[width - 1:width, :]
    for j in range(width - 1):
        y = y + buf[j:j + 1, :] * cw_ref[j:j + 1, :]
    y = _silu(y)
    bufnew_ref[0, 0:width - 2, :] = buf[1:width - 1, :]
    bufnew_ref[0, width - 2:width - 1, :] = raw
    za = z_ref[0][:, 3 * hd:4 * hd]
    ba = z_ref[0][:, c_ba:c_ba + 2 * LANES]
    beta_row = _sigmoid(ba[:, 0:LANES])
    g_row = -jnp.exp(alog_ref[...]) * _softplus(ba[:, LANES:2 * LANES] + dtb_ref[...])
    lane1 = lax.broadcasted_iota(jnp.int32, (1, LANES), 1)
    row = lax.broadcasted_iota(jnp.int32, (dk, LANES), 0)
    col = lax.broadcasted_iota(jnp.int32, (dk, LANES), 1)
    eye = row == col
    for h in range(h_a):
        q = y[:, h * dk:(h + 1) * dk]
        k = y[:, hd + h * dk:hd + (h + 1) * dk]
        v = y[:, 2 * hd + h * dk:2 * hd + (h + 1) * dk]
        q = q * lax.rsqrt(jnp.sum(q * q, axis=-1, keepdims=True) + EPS) * (dk ** -0.5)
        k = k * lax.rsqrt(jnp.sum(k * k, axis=-1, keepdims=True) + EPS)
        beta = jnp.sum(jnp.where(lane1 == h, beta_row, 0.0), axis=1, keepdims=True)
        a = jnp.exp(jnp.sum(jnp.where(lane1 == h, g_row, 0.0), axis=1, keepdims=True))
        k_col = jnp.sum(jnp.where(eye, k, 0.0), axis=1, keepdims=True)
        q_col = jnp.sum(jnp.where(eye, q, 0.0), axis=1, keepdims=True)
        s = a * s_ref[0, h]
        u = beta * (v - jnp.sum(s * k_col, axis=0, keepdims=True))
        s_new = s + k_col * u
        snew_ref[0, h] = s_new
        o = jnp.sum(s_new * q_col, axis=0, keepdims=True)
        ms = jnp.mean(o * o, axis=-1, keepdims=True)
        o_ref[0, :, h * dk:(h + 1) * dk] = (
            o * lax.rsqrt(ms + EPS) * gn_ref[...] * _silu(za[:, h * dk:(h + 1) * dk])
        ).astype(o_ref.dtype)


def _gdn_sample(z3, buf, s0, conv_w, a_log, dt_bias, gnorm, h_a, dk, c_ba):
    b, _, ncol = z3.shape
    width = conv_w.shape[0]
    hd = h_a * dk
    alog_row = jnp.zeros((1, LANES), F32).at[0, :h_a].set(a_log)
    dtb_row = jnp.zeros((1, LANES), F32).at[0, :h_a].set(dt_bias)
    return pl.pallas_call(
        functools.partial(_gdn_sample_kernel, h_a=h_a, dk=dk, width=width, c_ba=c_ba),
        grid=(b,),
        in_specs=[pl.BlockSpec((1, 1, ncol), lambda i: (i, 0, 0)),
                  pl.BlockSpec((1, width - 1, 3 * hd), lambda i: (i, 0, 0)),
                  pl.BlockSpec((1, h_a, dk, LANES), lambda i: (i, 0, 0, 0)),
                  pl.BlockSpec((width, 3 * hd), lambda i: (0, 0)),
                  pl.BlockSpec((1, LANES), lambda i: (0, 0)),
                  pl.BlockSpec((1, LANES), lambda i: (0, 0)),
                  pl.BlockSpec((1, LANES), lambda i: (0, 0))],
        out_specs=[pl.BlockSpec((1, 1, hd), lambda i: (i, 0, 0)),
                   pl.BlockSpec((1, h_a, dk, LANES), lambda i: (i, 0, 0, 0)),
                   pl.BlockSpec((1, width - 1, 3 * hd), lambda i: (i, 0, 0))],
        out_shape=[jax.ShapeDtypeStruct((b, 1, hd), BF16),
                   jax.ShapeDtypeStruct((b, h_a, dk, LANES), F32),
                   jax.ShapeDtypeStruct((b, width - 1, 3 * hd), F32)],
        compiler_params=_params("parallel"),
    )(z3, buf, s0, conv_w, alog_row, dtb_row, gnorm)


def _diff_lambda(lamp_ref, lam_init):
    lp = lamp_ref[...]
    e1 = jnp.exp(jnp.sum(lp[0:1] * lp[1:2], axis=1, keepdims=True))
    e2 = jnp.exp(jnp.sum(lp[2:3] * lp[3:4], axis=1, keepdims=True))
    return e1 - e2 + lam_init


def _attn_prompt_kernel(qi_ref, kj_ref, q_ref, k_ref, v_ref, lamp_ref, g_ref, o_ref,
                        m_ref, l_ref, acc_ref, *, dq, lam_init):
    p = pl.program_id(1)
    qi = qi_ref[p]
    kj = kj_ref[p]

    @pl.when(kj == 0)
    def _():
        m_ref[...] = jnp.full(m_ref.shape, NEG_BIG, F32)
        l_ref[...] = jnp.zeros(l_ref.shape, F32)
        acc_ref[...] = jnp.zeros(acc_ref.shape, F32)

    def step(masked):
        v = v_ref[...]
        for c in range(2):
            s = lax.dot_general(q_ref[:, c * dq:(c + 1) * dq], k_ref[:, c * dq:(c + 1) * dq],
                                NT_DIMS, preferred_element_type=F32)
            if masked:
                row = lax.broadcasted_iota(jnp.int32, s.shape, 0)
                col = lax.broadcasted_iota(jnp.int32, s.shape, 1)
                s = jnp.where(row >= col, s, NEG_BIG)
            m_prev = m_ref[c]
            m_new = jnp.maximum(m_prev, jnp.max(s, axis=1, keepdims=True))
            alpha = jnp.exp(m_prev - m_new)
            pm = jnp.exp(s - m_new)
            l_ref[c] = alpha * l_ref[c] + jnp.sum(pm, axis=1, keepdims=True)
            acc_ref[c] = alpha * acc_ref[c] + jnp.dot(pm.astype(BF16), v,
                                                      preferred_element_type=F32)
            m_ref[c] = m_new

    @pl.when(kj < qi)
    def _():
        step(False)

    @pl.when(kj == qi)
    def _():
        step(True)
        lam = _diff_lambda(lamp_ref, lam_init)
        o = acc_ref[0] / l_ref[0] - lam * (acc_ref[1] / l_ref[1])
        ms = jnp.mean(o * o, axis=-1, keepdims=True)
        o_ref[...] = (o * lax.rsqrt(ms + SUBLN_EPS) * g_ref[...]
                      * (1.0 - lam_init)).astype(o_ref.dtype)


def _attn_prompt(q, k, v, lam_params, subln_g, h_b, dq, lam_init):
    l = q.shape[0]
    dv = 2 * dq
    t = _tile(l, TILES["attn"], LANES)
    nb = l // t
    pairs = [(i, j) for i in range(nb) for j in range(i + 1)]
    qi_tab = jnp.asarray([a for a, _ in pairs], jnp.int32)
    kj_tab = jnp.asarray([b for _, b in pairs], jnp.int32)
    grid_spec = pltpu.PrefetchScalarGridSpec(
        num_scalar_prefetch=2,
        grid=(h_b, len(pairs)),
        in_specs=[pl.BlockSpec((t, dv), lambda h, p, qi, kj: (qi[p], h)),
                  pl.BlockSpec((t, dv), lambda h, p, qi, kj: (kj[p], h)),
                  pl.BlockSpec((t, dv), lambda h, p, qi, kj: (kj[p], h)),
                  pl.BlockSpec((4, dq), lambda h, p, qi, kj: (0, 0)),
                  pl.BlockSpec((1, dv), lambda h, p, qi, kj: (0, 0))],
        out_specs=pl.BlockSpec((t, dv), lambda h, p, qi, kj: (qi[p], h)),
        scratch_shapes=[pltpu.VMEM((2, t, 1), F32), pltpu.VMEM((2, t, 1), F32),
                        pltpu.VMEM((2, t, dv), F32)],
    )
    return pl.pallas_call(
        functools.partial(_attn_prompt_kernel, dq=dq, lam_init=lam_init),
        grid_spec=grid_spec,
        out_shape=jax.ShapeDtypeStruct((l, h_b * dv), BF16),
        compiler_params=_params("parallel", "arbitrary"),
    )(qi_tab, kj_tab, q, k, v, lam_params, subln_g)


def _attn_decode_kernel(pt_ref, q_ref, kn_ref, vn_ref, kc_ref, vc_ref, lamp_ref, g_ref, o_ref,
                        m_ref, l_ref, acc_ref, *, h_b, dq, lam_init):
    del pt_ref
    p = pl.program_id(1)
    nsub = 2 * h_b
    dv = 2 * dq
    page = kc_ref.shape[2]
    qm = q_ref[0]

    @pl.when(p == 0)
    def _():
        m_ref[...] = jnp.sum(qm * kn_ref[0], axis=1, keepdims=True)
        l_ref[...] = jnp.ones(l_ref.shape, F32)
        acc_ref[...] = vn_ref[0]

    k2 = kc_ref[0, 0].reshape(page * h_b, dv).astype(BF16)
    v2 = vc_ref[0, 0].reshape(page * h_b, dv).astype(BF16)
    s = lax.dot_general(qm.astype(BF16), k2, NT_DIMS, preferred_element_type=F32)
    sub = lax.broadcasted_iota(jnp.int32, s.shape, 0)
    lane = lax.broadcasted_iota(jnp.int32, s.shape, 1)
    assert h_b & (h_b - 1) == 0
    s = jnp.where((lane & (h_b - 1)) == (sub >> 1), s, NEG_BIG)
    m_prev = m_ref[...]
    m_new = jnp.maximum(m_prev, jnp.max(s, axis=1, keepdims=True))
    alpha = jnp.exp(m_prev - m_new)
    pm = jnp.exp(s - m_new)
    l_ref[...] = alpha * l_ref[...] + jnp.sum(pm, axis=1, keepdims=True)
    acc_ref[...] = alpha * acc_ref[...] + jnp.dot(pm.astype(BF16), v2, preferred_element_type=F32)
    m_ref[...] = m_new

    @pl.when(p == pl.num_programs(1) - 1)
    def _():
        lam = _diff_lambda(lamp_ref, lam_init)
        for h in range(h_b):
            o1 = acc_ref[2 * h:2 * h + 1, :] / l_ref[2 * h:2 * h + 1, :]
            o2 = acc_ref[2 * h + 1:2 * h + 2, :] / l_ref[2 * h + 1:2 * h + 2, :]
            o = o1 - lam * o2
            ms = jnp.mean(o * o, axis=-1, keepdims=True)
            o_ref[0, :, h * dv:(h + 1) * dv] = (o * lax.rsqrt(ms + SUBLN_EPS) * g_ref[...]
                                                * (1.0 - lam_init)).astype(o_ref.dtype)


def _attn_decode(qn, kn, vn, cache_k, cache_v, page_table, lam_params, subln_g, h_b, dq,
                 lam_init, lyr):
    b = qn.shape[0]
    n_pages = page_table.shape[1]
    page = cache_k.shape[2]
    dv = 2 * dq
    nsub = 2 * h_b
    zeros = jnp.zeros((b, h_b, dq), F32)

    def sub_rows(x):
        x4 = x.reshape(b, h_b, 2, dq)
        return jnp.stack([jnp.concatenate([x4[:, :, 0], zeros], axis=-1),
                          jnp.concatenate([zeros, x4[:, :, 1]], axis=-1)], axis=2).reshape(b, nsub, dv)

    qm = sub_rows(qn)
    km = sub_rows(kn)
    vm = jnp.repeat(vn.reshape(b, h_b, 1, dv), 2, axis=2).reshape(b, nsub, dv)
    rows = lambda: pl.BlockSpec((1, nsub, dv), lambda i, p, pt: (i, 0, 0))
    pages = lambda: pl.BlockSpec((1, 1, page, h_b, dv), lambda i, p, pt: (lyr, pt[i, p], 0, 0, 0))
    grid_spec = pltpu.PrefetchScalarGridSpec(
        num_scalar_prefetch=1,
        grid=(b, n_pages),
        in_specs=[rows(), rows(), rows(), pages(), pages(),
                  pl.BlockSpec((4, dq), lambda i, p, pt: (0, 0)),
                  pl.BlockSpec((1, dv), lambda i, p, pt: (0, 0))],
        out_specs=pl.BlockSpec((1, 1, h_b * dv), lambda i, p, pt: (i, 0, 0)),
        scratch_shapes=[pltpu.VMEM((nsub, 1), F32), pltpu.VMEM((nsub, 1), F32),
                        pltpu.VMEM((nsub, dv), F32)],
    )
    return pl.pallas_call(
        functools.partial(_attn_decode_kernel, h_b=h_b, dq=dq, lam_init=lam_init),
        grid_spec=grid_spec,
        out_shape=jax.ShapeDtypeStruct((b, 1, h_b * dv), BF16),
        compiler_params=_params("parallel", "arbitrary"),
    )(page_table, qm, km, vm, cache_k, cache_v, lam_params, subln_g)


def _ep_plain(accs, e, o, carry, i, ni):
    o[0][...] = accs[0].astype(o[0].dtype)


def _ep_bias(accs, e, o, carry, i, ni):
    o[0][...] = accs[0] + e[0][...]


def _ep_act(accs, e, o, carry, i, ni):
    x = accs[0]
    o[0][...] = (jnp.where(e[0][...] > 0.5, x, 1.0) * _sigmoid(x)).astype(o[0].dtype)


def _sample_post_kernel(q_ref, k_ref, ga_ref, gb_ref, gq_ref, gk_ref, qo_ref, ko_ref, sa_ref, sb_ref):
    _group_norm_store(q_ref[...], gq_ref[...], EPS, True, [qo_ref])
    _group_norm_store(k_ref[...], gk_ref[...], EPS, True, [ko_ref])
    sa_ref[...] = _sigmoid(ga_ref[...])
    sb_ref[...] = _sigmoid(gb_ref[...])


def _group_norm_store(y, gain, eps, use_mean, out_refs, rows=None):
    tn = y.shape[1]
    for g in range(tn // LANES):
        sl = slice(g * LANES, (g + 1) * LANES)
        blk = y[:, sl]
        ss = jnp.sum(blk * blk, axis=-1, keepdims=True)
        if use_mean:
            ss = ss / LANES
        val = blk * lax.rsqrt(ss + eps) * gain[:, sl]
        for r in out_refs:
            if rows is None:
                r[:, sl] = val.astype(r.dtype)
            else:
                r[rows, sl] = val.astype(r.dtype)


def _ep_qknorm(accs, e, o, carry, i, ni):
    _group_norm_store(accs[0], e[0][...], EPS, True, o)


def _ep_copy2(accs, e, o, carry, i, ni):
    for r in o:
        r[...] = accs[0].astype(r.dtype)


def _ep_gdn_conv(accs, e, o, carry, i, ni, *, width):
    raw = accs[0]
    taps = [e[j][...] for j in range(width)]
    nflag = e[width][...]
    gain = e[width + 1][...]
    y, head = _conv_taps(raw, carry[0], taps, i, jnp.zeros(carry[0].shape, F32))

    def post(val, rows):
        val = _silu(val)
        tn = val.shape[1]
        for g in range(tn // LANES):
            sl = slice(g * LANES, (g + 1) * LANES)
            blk = val[:, sl]
            ss = jnp.sum(blk * blk, axis=-1, keepdims=True)
            scale = jnp.where(nflag[:, sl] > 0.5, lax.rsqrt(ss + EPS), 1.0) * gain[:, sl]
            o[0][rows, sl] = (blk * scale).astype(o[0].dtype)

    post(y, slice(None))
    post(head, slice(0, SUBLANES))
    o[1][...] = raw[raw.shape[0] - SUBLANES:]


def _ep_branch(accs, e, o, carry, i, ni):
    o[0][...] = (e[0][...].astype(F32) * accs[0] + e[1][...].astype(F32) * accs[1]).astype(o[0].dtype)


def _ep_residual(accs, e, o, carry, i, ni):
    o[0][...] = e[0][...] + e[1][...] * accs[0]


def _ep_ffn_prompt(accs, e, o, carry, i, ni, *, width):
    outs = []
    for d in range(2):
        taps = [e[d * (width + 1) + j][...] for j in range(width)]
        bias = e[d * (width + 1) + width][...]
        y, head = _conv_taps(accs[d], carry[d], taps, i, jnp.zeros(carry[d].shape, F32))
        outs.append((y + bias, head + bias))
        o[1 + d][...] = accs[d][accs[d].shape[0] - SUBLANES:]
    o[0][...] = (_silu(outs[0][0]) * outs[1][0]).astype(o[0].dtype)
    o[0][0:SUBLANES, :] = (_silu(outs[0][1]) * outs[1][1]).astype(o[0].dtype)


def _ep_ffn_sample(accs, e, o, carry, i, ni, *, width):
    vals = []
    per = (width - 1) + width + 1
    for d in range(2):
        base = d * per
        y = accs[d] * e[base + (width - 1) + width - 1][...]
        for j in range(width - 1):
            y = y + e[base + j][...] * e[base + (width - 1) + j][...]
        vals.append(y + e[base + per - 1][...])
        o[1 + d][...] = accs[d]
    o[0][...] = (_silu(vals[0]) * vals[1]).astype(o[0].dtype)


def _pad_cols(a, n):
    return a if a.shape[-1] == n else jnp.pad(a, [(0, 0)] * (a.ndim - 1) + [(0, n - a.shape[-1])])


def kernel(x_prompt, x_sample, c_prompt, c_sample, cache_k, cache_v, state_gdn, state_gdn_conv, state_ffn_conv, page_table, w_ada, b_ada, norm1_g, norm2_g, w_in, gdn_conv_w, gdn_a_log, gdn_dt_bias, gdn_norm_g, diff_q_norm_g, diff_k_norm_g, diff_lambda, diff_subln_g, w_branch_a, w_branch_b, w_o, w_up, ffn_conv_w, ffn_conv_b, w_down):
    depth = w_in.shape[0]
    assert depth == 1 and x_prompt.shape[0] == 1 and x_sample.shape[1] == 1
    lyr = 0
    lam_init = 0.8 - 0.6 * math.exp(-0.3 * lyr)
    d = x_prompt.shape[-1]
    l = x_prompt.shape[1]
    nb = x_sample.shape[0]
    h_a, dk, dva = state_gdn.shape[2:]
    h_b = cache_k.shape[3]
    dq = cache_k.shape[4] // 2
    dvb = cache_v.shape[4]
    assert dk == LANES and dva == LANES and dq == LANES and dvb == 2 * dq
    d_ff = w_down.shape[1]
    gw = gdn_conv_w.shape[1]
    fw = ffn_conv_w.shape[1]
    hd_a = h_a * dk
    hd_b = h_b * dvb
    n_pages = page_table.shape[1]
    page = cache_k.shape[2]

    w_in0 = w_in[lyr]
    o_qkv, o_za = 0, 3 * hd_a
    o_b, o_a = o_za + hd_a, o_za + hd_a + h_a
    o_qb = o_a + h_a
    o_kb, o_vb = o_qb + hd_b, o_qb + 2 * hd_b
    o_ga, o_gb = o_vb + hd_b, o_vb + hd_b + d
    assert o_gb + d == w_in0.shape[1]
    seg = lambda a, n: w_in0[:, a:a + n].astype(BF16)
    w_ba = jnp.concatenate([_pad_cols(seg(o_b, h_a), LANES), _pad_cols(seg(o_a, h_a), LANES)], axis=1)
    w_all = jnp.concatenate([seg(o_qkv, 3 * hd_a), seg(o_za, hd_a), seg(o_ga, d), seg(o_gb, d),
                             seg(o_qb, hd_b), seg(o_kb, hd_b), seg(o_vb, hd_b), w_ba], axis=1)
    c_qkv, c_za = 0, 3 * hd_a
    c_ga = c_za + hd_a
    c_gb = c_ga + d
    c_qb = c_gb + d
    c_kb = c_qb + hd_b
    c_vb = c_kb + hd_b
    c_ba = c_vb + hd_b
    n_all = c_ba + 2 * LANES

    ffp = -(-d_ff // 512) * 512
    wu = w_up[lyr]
    w_upg = _pad_cols(wu[:, :d_ff].astype(BF16), ffp)
    w_upv = _pad_cols(wu[:, d_ff:].astype(BF16), ffp)
    w_dn = jnp.pad(w_down[lyr].astype(BF16), ((0, ffp - d_ff), (0, 0)))
    w_o_b = w_o[lyr].astype(BF16)
    w_ba_b = w_branch_a[lyr].astype(BF16)
    w_bb_b = w_branch_b[lyr].astype(BF16)

    row = lambda v: v.reshape(1, -1).astype(F32)
    tile_rows = lambda v, reps: jnp.tile(row(v), (1, reps))

    c_all = jnp.concatenate([c_prompt, c_sample], axis=0).astype(F32)
    mp = -(-c_all.shape[0] // 16) * 16
    c_all = jnp.pad(c_all, ((0, mp - c_all.shape[0]), (0, 0)))
    (mod,) = _matmul([c_all], [(w_ada[lyr], 0)], [(0, 0)], 6 * d, _ep_bias, [(F32, "tile")],
                     tm=mp, tn=_tile(6 * d, 512, LANES), extras=[(row(b_ada[lyr]), "row", 0)],
                     a_fn=_silu)
    mod_p = [mod[0:1, j * d:(j + 1) * d] for j in range(6)]
    mod_s = [mod[1:1 + nb, j * d:(j + 1) * d] for j in range(6)]

    gq = tile_rows(diff_q_norm_g[lyr], 2 * h_b) * (dq ** -0.5)
    gk = tile_rows(diff_k_norm_g[lyr], 2 * h_b)
    act_flag = jnp.concatenate([jnp.ones((1, hd_a), F32), jnp.zeros((1, 2 * d), F32)], axis=1)
    cw = gdn_conv_w[lyr].astype(F32)
    conv_nflag = jnp.concatenate([jnp.ones((1, 2 * hd_a), F32), jnp.zeros((1, hd_a), F32)], axis=1)
    conv_gain = jnp.concatenate([jnp.full((1, hd_a), dk ** -0.5, F32), jnp.ones((1, 2 * hd_a), F32)], axis=1)
    gnorm = row(gdn_norm_g[lyr])
    subln = row(diff_subln_g[lyr])
    lamp = diff_lambda[lyr].astype(F32)
    fcw = ffn_conv_w[lyr].astype(F32)
    fcb = ffn_conv_b[lyr].astype(F32)
    fcw_g = [_pad_cols(fcw[j:j + 1, :d_ff], ffp) for j in range(fw)]
    fcw_v = [_pad_cols(fcw[j:j + 1, d_ff:], ffp) for j in range(fw)]
    fcb_g = _pad_cols(fcb[None, :d_ff], ffp)
    fcb_v = _pad_cols(fcb[None, d_ff:], ffp)

    xp = x_prompt[0].astype(F32)
    tm = _tile(l, TILES["mm_m"], 16)
    tn = TILES["mm_n"]
    h1 = _normmod(xp, row(norm1_g[lyr]), mod_p[1], mod_p[0])

    tn_c = _tile(3 * hd_a, tn, LANES)
    qkv_c, qkv_tail = _matmul(
        [h1], [(w_all, c_qkv)], [(0, 0)], 3 * hd_a,
        functools.partial(_ep_gdn_conv, width=gw), [(BF16, "tile"), (F32, "tail")],
        tm=_tile(l, 512, 16), tn=tn_c, order="nm", carry_rows=1,
        extras=[(cw[j:j + 1], "row", 0) for j in range(gw)] + [(conv_nflag, "row", 0), (conv_gain, "row", 0)])
    n_act = hd_a + 2 * d
    tn_a = _tile(math.gcd(hd_a, d), tn, LANES)
    (act_p,) = _matmul([h1], [(w_all, c_za)], [(0, 0)], n_act, _ep_act, [(BF16, "tile")],
                       tm=tm, tn=tn_a, extras=[(act_flag, "row", 0)])
    (ba_p,) = _matmul([h1], [(w_all, c_ba)], [(0, 0)], 2 * LANES, _ep_plain, [(F32, "tile")],
                      tm=tm, tn=2 * LANES)
    tn_b = _tile(hd_b, tn, LANES)
    (qb_p,) = _matmul([h1], [(w_all, c_qb)], [(0, 0)], hd_b, _ep_qknorm, [(BF16, "tile")],
                      tm=tm, tn=tn_b, extras=[(gq, "row", 0)])
    kb_p, kb_p16 = _matmul([h1], [(w_all, c_kb)], [(0, 0)], hd_b, _ep_qknorm,
                           [(F32, "tile"), (BF16, "tile")], tm=tm, tn=tn_b, extras=[(gk, "row", 0)])
    vb_p, vb_p16 = _matmul([h1], [(w_all, c_vb)], [(0, 0)], hd_b, _ep_copy2,
                           [(F32, "tile"), (BF16, "tile")], tm=tm, tn=tn_b)

    oa_p, s_p = _gdn_prompt(qkv_c, ba_p, act_p, gdn_a_log[lyr].astype(F32),
                            gdn_dt_bias[lyr].astype(F32), gnorm, h_a, dk)
    ob_p = _attn_prompt(qb_p, kb_p16, vb_p16, lamp, subln, h_b, dq, lam_init)

    tn_d = _tile(d, tn, LANES)
    (mixed_p,) = _matmul([oa_p, ob_p], [(w_ba_b, 0), (w_bb_b, 0)], [(0, 0), (1, 1)], d, _ep_branch,
                         [(BF16, "tile")], tm=tm, tn=tn_d,
                         extras=[(act_p, "tile", hd_a), (act_p, "tile", hd_a + d)])
    (x2_p,) = _matmul([mixed_p], [(w_o_b, 0)], [(0, 0)], d, _ep_residual, [(F32, "tile")],
                      tm=tm, tn=tn_d, extras=[(xp, "tile", 0), (mod_p[2], "row", 0)])
    h2 = _normmod(x2_p, row(norm2_g[lyr]), mod_p[4], mod_p[3])
    tn_u = _tile(ffp, TILES["up_n"], LANES)
    ffn_ex = ([(fcw_g[j], "row", 0) for j in range(fw)] + [(fcb_g, "row", 0)]
              + [(fcw_v[j], "row", 0) for j in range(fw)] + [(fcb_v, "row", 0)])
    act_f, tail_g, tail_v = _matmul(
        [h2], [(w_upg, 0), (w_upv, 0)], [(0, 0), (0, 1)], ffp,
        functools.partial(_ep_ffn_prompt, width=fw),
        [(BF16, "tile"), (F32, "tail"), (F32, "tail")],
        tm=_tile(l, TILES["up_m"], 16), tn=tn_u, order="nm", carry_rows=2, extras=ffn_ex)
    (y_p,) = _matmul([act_f], [(w_dn, 0)], [(0, 0)], d, _ep_residual, [(F32, "tile")],
                     tm=tm, tn=tn_d, tk=_tile(ffp, TILES["down_k"], LANES),
                     extras=[(x2_p, "tile", 0), (mod_p[5], "row", 0)])

    xs = x_sample[:, 0, :].astype(F32)
    tns = TILES["small_n"]
    h1s = _normmod(xs, row(norm1_g[lyr]), mod_s[1], mod_s[0])
    (z_s,) = _matmul([h1s], [(w_all, 0)], [(0, 0)], n_all, _ep_plain, [(F32, "tile")],
                     tm=nb, tn=_tile(n_all, tns, LANES))
    oa_s, s_s, gbuf_s = _gdn_sample(z_s[:, None, :], state_gdn_conv[lyr].astype(F32),
                                    state_gdn[lyr].astype(F32), cw, gdn_a_log[lyr].astype(F32),
                                    gdn_dt_bias[lyr].astype(F32), gnorm, h_a, dk, c_ba)
    qb_s = z_s[:, c_qb:c_qb + hd_b]
    kb_raw_s = z_s[:, c_kb:c_kb + hd_b]
    vb_s = z_s[:, c_vb:c_vb + hd_b]
    qn_s, kn_s, sga_s, sgb_s = pl.pallas_call(
        _sample_post_kernel,
        out_shape=[jax.ShapeDtypeStruct((nb, hd_b), F32), jax.ShapeDtypeStruct((nb, hd_b), F32),
                   jax.ShapeDtypeStruct((nb, d), F32), jax.ShapeDtypeStruct((nb, d), F32)],
    )(qb_s, kb_raw_s, z_s[:, c_ga:c_ga + d], z_s[:, c_gb:c_gb + d], gq, gk)
    ob_s = _attn_decode(qn_s, kn_s, vb_s, cache_k, cache_v, page_table.astype(jnp.int32), lamp, subln,
                        h_b, dq, lam_init, lyr)

    tn_ds = _tile(d, tns, LANES)
    (mixed_s,) = _matmul([oa_s[:, 0, :], ob_s[:, 0, :]], [(w_ba_b, 0), (w_bb_b, 0)], [(0, 0), (1, 1)], d,
                         _ep_branch, [(BF16, "tile")], tm=nb, tn=tn_ds,
                         extras=[(sga_s, "tile", 0), (sgb_s, "tile", 0)])
    (x2_s,) = _matmul([mixed_s], [(w_o_b, 0)], [(0, 0)], d, _ep_residual, [(F32, "tile")],
                      tm=nb, tn=tn_ds, extras=[(xs, "tile", 0), (mod_s[2], "tile", 0)])
    h2s = _normmod(x2_s, row(norm2_g[lyr]), mod_s[4], mod_s[3])
    fbuf = state_ffn_conv[lyr].astype(F32)
    st_g = [_pad_cols(fbuf[:, j, :d_ff], ffp) for j in range(fw - 1)]
    st_v = [_pad_cols(fbuf[:, j, d_ff:], ffp) for j in range(fw - 1)]
    ffn_ex_s = ([(a, "tile", 0) for a in st_g] + [(fcw_g[j], "row", 0) for j in range(fw)] + [(fcb_g, "row", 0)]
                + [(a, "tile", 0) for a in st_v] + [(fcw_v[j], "row", 0) for j in range(fw)] + [(fcb_v, "row", 0)])
    act_s, up_g, up_v = _matmul(
        [h2s], [(w_upg, 0), (w_upv, 0)], [(0, 0), (0, 1)], ffp,
        functools.partial(_ep_ffn_sample, width=fw),
        [(BF16, "tile"), (F32, "tile"), (F32, "tile")],
        tm=nb, tn=_tile(ffp, tns, LANES), extras=ffn_ex_s)
    (y_s,) = _matmul([act_s], [(w_dn, 0)], [(0, 0)], d, _ep_residual, [(F32, "tile")],
                     tm=nb, tn=tn_ds, tk=_tile(ffp, TILES["down_k"], LANES),
                     extras=[(x2_s, "tile", 0), (mod_s[5], "tile", 0)])

    y_prompt = y_p[None].astype(x_prompt.dtype)
    y_sample = y_s[:, None, :].astype(x_sample.dtype)
    k_prompt = kb_p.reshape(1, 1, l, h_b, 2 * dq)
    v_prompt = vb_p.reshape(1, 1, l, h_b, dvb)
    gdn_state_prompt = s_p[None, None]
    gdn_conv_prompt = qkv_tail[SUBLANES - (gw - 1):][None, None]
    ffn_conv_prompt = jnp.concatenate([tail_g[SUBLANES - (fw - 1):, :d_ff],
                                       tail_v[SUBLANES - (fw - 1):, :d_ff]], axis=1)[None, None]
    k_sample = kn_s.reshape(1, nb, 1, h_b, 2 * dq)
    v_sample = vb_s.reshape(1, nb, 1, h_b, dvb)
    gdn_state_sample = s_s[None]
    gdn_conv_sample = gbuf_s[None]
    up_new = jnp.concatenate([up_g[:, :d_ff], up_v[:, :d_ff]], axis=1)
    ffn_conv_sample = jnp.concatenate([fbuf[:, 1:, :], up_new[:, None, :]], axis=1)[None]
    return (y_prompt, y_sample, k_prompt, v_prompt, gdn_state_prompt, gdn_conv_prompt,
            ffn_conv_prompt, k_sample, v_sample, gdn_state_sample, gdn_conv_sample, ffn_conv_sample)
```

```python
import functools
import math

import jax
import jax.numpy as jnp
from jax import lax
from jax.experimental import pallas as pl
from jax.experimental.pallas import tpu as pltpu

F32 = jnp.float32
BF16 = jnp.bfloat16

LANES = 128
SUBLANES = 8
VMEM_LIMIT_BYTES = 56 * 1024 * 1024

EPS = 1e-6
SUBLN_EPS = 1e-5
GDN_CHUNK = 128
ATTN_SLAB = 1024
NEG_BIG = -1e30

TILES = dict(
    norm_rows=256,
    mm_m=1024, mm_n=512,
    up_n=256,
    down_k=5504,
    attn=1024,
    gdn_heads=8,
    decode_pages=4,
    small_n=1024,
)

NT_DIMS = (((1,), (1,)), ((), ()))


def _params(*sem):
    return pltpu.CompilerParams(dimension_semantics=sem, vmem_limit_bytes=VMEM_LIMIT_BYTES)


def _tile(dim, pref, align):
    if dim <= pref:
        return dim
    t = (pref // align) * align
    while t >= align:
        if dim % t == 0:
            return t
        t -= align
    return dim


def _sigmoid(x):
    return 1.0 / (1.0 + jnp.exp(-x))


def _silu(x):
    return x * _sigmoid(x)


def _softplus(x):
    return jnp.maximum(x, 0.0) + jnp.log1p(jnp.exp(-jnp.abs(x)))


def _normmod_kernel(x_ref, g_ref, sc_ref, sh_ref, o_ref):
    x = x_ref[...]
    ms = jnp.mean(x * x, axis=-1, keepdims=True)
    h = x * lax.rsqrt(ms + EPS) * g_ref[...]
    o_ref[...] = (h * (1.0 + sc_ref[...]) + sh_ref[...]).astype(o_ref.dtype)


def _normmod(x, g, sc, sh):
    m, d = x.shape
    tm = _tile(m, TILES["norm_rows"], SUBLANES)
    per_row = sc.shape[0] != 1
    mod_spec = (pl.BlockSpec((tm, d), lambda i: (i, 0)) if per_row
                else pl.BlockSpec((1, d), lambda i: (0, 0)))
    return pl.pallas_call(
        _normmod_kernel,
        grid=(m // tm,),
        in_specs=[pl.BlockSpec((tm, d), lambda i: (i, 0)),
                  pl.BlockSpec((1, d), lambda i: (0, 0)), mod_spec, mod_spec],
        out_specs=pl.BlockSpec((tm, d), lambda i: (i, 0)),
        out_shape=jax.ShapeDtypeStruct((m, d), BF16),
        compiler_params=_params("parallel"),
        name="normmod",
    )(x, g, sc, sh)


def _matmul(name, as_, ws, dots, n, epilogue, outs, *, tm, tn, extras=(), order="mn",
            tk=None, a_fn=None, carry_rows=0, cast_once=False, row_split=1):
    m = as_[0].shape[0]
    kdim = as_[0].shape[1]
    tk = kdim if tk is None else tk
    nk = kdim // tk
    assert kdim % tk == 0 and m % tm == 0 and n % tn == 0 and tm % row_split == 0
    if nk > 1:
        assert all(a.shape[1] == kdim for a in as_) and row_split == 1 and not cast_once
    assert not cast_once or order == "nm"
    ni, nj = m // tm, n // tn
    na, nw, ne, no, nd = len(as_), len(ws), len(extras), len(outs), len(dots)
    rows = tm // row_split

    if order == "mn":
        grid = (ni, nj, nk)
        ij = lambda g0, g1: (g0, g1)
    else:
        grid = (nj, ni, nk)
        ij = lambda g0, g1: (g1, g0)

    def a_map(g0, g1, k):
        return (ij(g0, g1)[0], k)

    def w_map(off):
        return lambda g0, g1, k: (k, ij(g0, g1)[1] + off // tn)

    def row_map(off):
        return lambda g0, g1, k: (0, ij(g0, g1)[1] + off // tn)

    def tile_map(off):
        return lambda g0, g1, k: (ij(g0, g1)[0], ij(g0, g1)[1] + off // tn)

    in_specs = []
    for a in as_:
        in_specs.append(pl.BlockSpec((tm, tk if nk > 1 else a.shape[1]), a_map))
    for w, off in ws:
        assert off % tn == 0
        in_specs.append(pl.BlockSpec((tk if nk > 1 else w.shape[0], tn), w_map(off)))
    for arr, kind, off in extras:
        assert off % tn == 0
        if kind == "row":
            in_specs.append(pl.BlockSpec((1, tn), row_map(off)))
        else:
            in_specs.append(pl.BlockSpec((tm, tn), tile_map(off)))
    out_specs, out_shapes = [], []
    for dtype, kind in outs:
        if kind == "tile":
            out_specs.append(pl.BlockSpec((tm, tn), tile_map(0)))
            out_shapes.append(jax.ShapeDtypeStruct((m, n), dtype))
        else:
            out_specs.append(pl.BlockSpec((SUBLANES, tn), row_map(0)))
            out_shapes.append(jax.ShapeDtypeStruct((SUBLANES, n), dtype))
    scratch = []
    if nk > 1:
        scratch += [pltpu.VMEM((tm, tn), F32) for _ in range(nd)]
    if cast_once:
        scratch += [pltpu.VMEM((w.shape[0], tn), BF16) for w, _ in ws]
    scratch += [pltpu.VMEM((SUBLANES, tn), F32) for _ in range(carry_rows)]

    def body(*refs):
        a_refs = refs[:na]
        w_refs = refs[na:na + nw]
        e_refs = refs[na + nw:na + nw + ne]
        o_refs = refs[na + nw + ne:na + nw + ne + no]
        s_refs = list(refs[na + nw + ne + no:])
        acc_refs = [s_refs.pop(0) for _ in range(nd)] if nk > 1 else []
        wb_refs = [s_refs.pop(0) for _ in range(nw)] if cast_once else list(w_refs)
        carry_refs = s_refs
        i, _ = ij(pl.program_id(0), pl.program_id(1))
        k = pl.program_id(2)

        def partial(d, r):
            ai, wi = dots[d]
            a = a_refs[ai][r * rows:(r + 1) * rows, :] if row_split > 1 else a_refs[ai][...]
            if a_fn is not None:
                a = a_fn(a)
            return jnp.dot(a.astype(BF16), wb_refs[wi][...].astype(BF16),
                           preferred_element_type=F32)

        if cast_once:
            @pl.when(i == 0)
            def _():
                for wi in range(nw):
                    wb_refs[wi][...] = w_refs[wi][...].astype(BF16)

        if nk == 1:
            for r in range(row_split):
                if row_split > 1:
                    piece = pl.ds(r * rows, rows)
                    ev = [e.at[piece] if extras[x][1] == "tile" else e for x, e in enumerate(e_refs)]
                    ov = [o.at[piece] if outs[x][1] == "tile" else o for x, o in enumerate(o_refs)]
                else:
                    ev, ov = e_refs, o_refs
                epilogue([partial(d, r) for d in range(nd)], ev, ov, carry_refs,
                         (i == 0) if r == 0 else False, r == row_split - 1)
        else:
            @pl.when(k == 0)
            def _():
                for d in range(nd):
                    acc_refs[d][...] = jnp.zeros((tm, tn), F32)

            for d in range(nd):
                acc_refs[d][...] += partial(d, 0)

            @pl.when(k == nk - 1)
            def _():
                epilogue([acc_refs[d][...] for d in range(nd)], e_refs, o_refs,
                         carry_refs, i == 0, True)

    return pl.pallas_call(
        body,
        grid=grid,
        in_specs=in_specs,
        out_specs=out_specs,
        out_shape=out_shapes,
        scratch_shapes=scratch,
        compiler_params=_params("arbitrary", "arbitrary", "arbitrary"),
        name=name,
    )(*as_, *[w for w, _ in ws], *[e for e, _, _ in extras])


def _conv_taps(raw, carry_ref, taps, first):
    width = len(taps)
    if first is not False:
        @pl.when(first)
        def _():
            carry_ref[...] = jnp.zeros(carry_ref.shape, F32)

    prev = carry_ref[...]
    rowid = lax.broadcasted_iota(jnp.int32, prev.shape, 0)
    y = raw * taps[width - 1]
    head = raw[0:SUBLANES] * taps[width - 1]
    for s in range(1, width):
        tap = taps[width - 1 - s]
        shifted = pltpu.roll(raw, s, axis=0)
        y = y + shifted * tap
        head = head + jnp.where(rowid < s, pltpu.roll(prev, s, axis=0),
                                shifted[0:SUBLANES]) * tap
    carry_ref[...] = raw[raw.shape[0] - SUBLANES:]
    return y, head


def _gdn_gates(ba, alog_row, dtb_row):
    return _sigmoid(ba), -jnp.exp(alog_row) * _softplus(ba + dtb_row)


def _gdn_prep_kernel(ba_ref, alog_ref, dtb_ref, beta_ref, gc_ref, gct_ref, *, heads_pad, h_a):
    c = GDN_CHUNK
    beta, g = _gdn_gates(ba_ref[...], alog_ref[...], dtb_ref[...])
    beta_ref[...] = beta
    row = lax.broadcasted_iota(jnp.int32, (c, LANES), 0)
    s = 1
    while s < c:
        g = g + jnp.where(row >= s, pltpu.roll(g, s, axis=0), 0.0)
        s *= 2
    gc_ref[...] = g
    gct_ref[...] = g.T[h_a:h_a + heads_pad, :]


def _gdn_intra_kernel(q_ref, k_ref, v_ref, beta_ref, gc_ref, gct_ref,
                      uv_ref, w_ref, aqk_ref, qdec_ref, kdect_ref, *, hb, h_a):
    c = GDN_CHUNK
    grp = pl.program_id(1)
    lane = lax.broadcasted_iota(jnp.int32, (c, LANES), 1)
    row = lax.broadcasted_iota(jnp.int32, (c, c), 0)
    col = lax.broadcasted_iota(jnp.int32, (c, c), 1)
    eye = jnp.where(row == col, 1.0, 0.0)
    levels = range(1, int(math.log2(c)))
    below = [((row >> (lvl + 1)) == (col >> (lvl + 1))) & ((row >> lvl) != (col >> lvl))
             for lvl in levels]
    pair = (row >> 1) == (col >> 1)
    beta_all = beta_ref[...]
    gc_all = gc_ref[...]
    heads = range(hb)
    sls = [slice(j * LANES, (j + 1) * LANES) for j in heads]
    beta_c, gcc, kb, mm, t = [], [], [], [], []
    for j in heads:
        h = grp * hb + j
        beta_c.append(jnp.sum(jnp.where(lane == h, beta_all, 0.0), axis=1, keepdims=True))
        gcc.append(jnp.sum(jnp.where(lane == h + h_a, gc_all, 0.0), axis=1, keepdims=True))
        gcr = gct_ref[pl.ds(h, 1), :]
        dec_incl = jnp.exp(jnp.where(row >= col, gcc[j] - gcr, NEG_BIG))
        k = k_ref[:, sls[j]]
        kb.append(k.astype(F32) * beta_c[j])
        mm.append(lax.dot_general(kb[j].astype(BF16), k, NT_DIMS, preferred_element_type=F32)
                  * jnp.where(row > col, dec_incl, 0.0))
        aqk = lax.dot_general(q_ref[:, sls[j]], k, NT_DIMS, preferred_element_type=F32) * dec_incl
        aqk_ref[:, sls[j]] = aqk.astype(BF16)
        t.append(eye - jnp.where(pair, mm[j], 0.0))
    for msk in below:
        tb = [t[j].astype(BF16) for j in heads]
        bt = [jnp.dot(jnp.where(msk, mm[j], 0.0).astype(BF16), tb[j], preferred_element_type=F32)
              for j in heads]
        t = [t[j] - jnp.dot(tb[j], bt[j].astype(BF16), preferred_element_type=F32) for j in heads]
    for j in heads:
        tb = t[j].astype(BF16)
        egc = jnp.exp(gcc[j])
        gl = gcc[j][c - 1:c, :]
        vb = v_ref[:, sls[j]].astype(F32) * beta_c[j]
        uv_ref[:, sls[j]] = jnp.dot(tb, vb.astype(BF16), preferred_element_type=F32)
        w_ref[:, sls[j]] = jnp.dot(tb, (kb[j] * egc).astype(BF16),
                                   preferred_element_type=F32).astype(BF16)
        qdec_ref[:, sls[j]] = (q_ref[:, sls[j]].astype(F32) * egc).astype(BF16)
        kdect_ref[sls[j], :] = (k_ref[:, sls[j]].astype(F32) * jnp.exp(gl - gcc[j])).T.astype(BF16)


def _gdn_rec_kernel(uv_ref, w_ref, aqk_ref, qdec_ref, kdect_ref, gct_ref, za_ref, gn_ref,
                    o_ref, sout_ref, s_ref, *, hb):
    c = GDN_CHUNK
    n = pl.program_id(1)
    g0 = pl.program_id(0)

    @pl.when(n == 0)
    def _():
        s_ref[...] = jnp.zeros(s_ref.shape, F32)

    for j in range(hb):
        sl = slice(j * LANES, (j + 1) * LANES)
        s = s_ref[j]
        sb = s.astype(BF16)
        u = uv_ref[:, sl] - jnp.dot(w_ref[:, sl], sb, preferred_element_type=F32)
        ub = u.astype(BF16)
        o = (jnp.dot(qdec_ref[:, sl], sb, preferred_element_type=F32)
             + jnp.dot(aqk_ref[:, sl], ub, preferred_element_type=F32))
        gl = jnp.exp(gct_ref[pl.ds(g0 * hb + j, 1), c - 1:c])
        s_ref[j] = gl * s + jnp.dot(kdect_ref[sl, :], ub, preferred_element_type=F32)
        ms = jnp.mean(o * o, axis=-1, keepdims=True)
        o_ref[:, sl] = (o * lax.rsqrt(ms + EPS) * gn_ref[...]
                        * za_ref[:, sl].astype(F32)).astype(o_ref.dtype)

    @pl.when(n == pl.num_programs(1) - 1)
    def _():
        sout_ref[...] = s_ref[...]


def _gate_rows(a_log, dt_bias, h_a):
    alog_row = jnp.zeros((1, LANES), F32).at[0, h_a:2 * h_a].set(a_log)
    dtb_row = jnp.zeros((1, LANES), F32).at[0, h_a:2 * h_a].set(dt_bias)
    return alog_row, dtb_row


def _gdn_prompt(qkv, ba, za_silu, a_log, dt_bias, gnorm, h_a, dk):
    l = qkv.shape[0]
    c = GDN_CHUNK
    assert l % c == 0 and dk == LANES and 2 * h_a <= LANES
    nchunk = l // c
    hp = max(SUBLANES, -(-h_a // SUBLANES) * SUBLANES)
    alog_row, dtb_row = _gate_rows(a_log, dt_bias, h_a)

    beta, gc, gct = pl.pallas_call(
        functools.partial(_gdn_prep_kernel, heads_pad=hp, h_a=h_a),
        grid=(nchunk,),
        in_specs=[pl.BlockSpec((c, LANES), lambda n: (n, 0)),
                  pl.BlockSpec((1, LANES), lambda n: (0, 0)),
                  pl.BlockSpec((1, LANES), lambda n: (0, 0))],
        out_specs=[pl.BlockSpec((c, LANES), lambda n: (n, 0)),
                   pl.BlockSpec((c, LANES), lambda n: (n, 0)),
                   pl.BlockSpec((hp, c), lambda n: (0, n))],
        out_shape=[jax.ShapeDtypeStruct((l, LANES), F32),
                   jax.ShapeDtypeStruct((l, LANES), F32),
                   jax.ShapeDtypeStruct((hp, l), F32)],
        compiler_params=_params("parallel"),
        name="gdn_prep",
    )(ba, alog_row, dtb_row)

    hd = h_a * dk
    hb = _tile(h_a, TILES["gdn_heads"], 1)
    ng = h_a // hb
    blk = lambda off: pl.BlockSpec((c, hb * LANES), lambda n, g: (n, g + off))
    uv, w, aqk, qdec, kdect = pl.pallas_call(
        functools.partial(_gdn_intra_kernel, hb=hb, h_a=h_a),
        grid=(nchunk, ng),
        in_specs=[blk(0), blk(ng), blk(2 * ng),
                  pl.BlockSpec((c, LANES), lambda n, g: (n, 0)),
                  pl.BlockSpec((c, LANES), lambda n, g: (n, 0)),
                  pl.BlockSpec((hp, c), lambda n, g: (0, n))],
        out_specs=[blk(0), blk(0), blk(0), blk(0),
                   pl.BlockSpec((hb * LANES, c), lambda n, g: (g, n))],
        out_shape=[jax.ShapeDtypeStruct((l, hd), F32),
                   jax.ShapeDtypeStruct((l, hd), BF16),
                   jax.ShapeDtypeStruct((l, hd), BF16),
                   jax.ShapeDtypeStruct((l, hd), BF16),
                   jax.ShapeDtypeStruct((hd, l), BF16)],
        compiler_params=_params("parallel", "arbitrary"),
        name="gdn_intra",
    )(qkv, qkv, qkv, beta, gc, gct)

    hr = _tile(h_a, 4, 1)
    wide = lambda: pl.BlockSpec((c, hr * LANES), lambda g, n: (n, g))
    oa, s_fin = pl.pallas_call(
        functools.partial(_gdn_rec_kernel, hb=hr),
        grid=(h_a // hr, nchunk),
        in_specs=[wide(), wide(), wide(), wide(),
                  pl.BlockSpec((hr * LANES, c), lambda g, n: (g, n)),
                  pl.BlockSpec((hp, c), lambda g, n: (0, n)),
                  wide(),
                  pl.BlockSpec((1, LANES), lambda g, n: (0, 0))],
        out_specs=[wide(), pl.BlockSpec((hr, dk, LANES), lambda g, n: (g, 0, 0))],
        out_shape=[jax.ShapeDtypeStruct((l, hd), BF16),
                   jax.ShapeDtypeStruct((h_a, dk, LANES), F32)],
        scratch_shapes=[pltpu.VMEM((hr, dk, LANES), F32)],
        compiler_params=_params("parallel", "arbitrary"),
        name="gdn_rec",
    )(uv, w, aqk, qdec, kdect, gct, za_silu, gnorm)
    return oa, s_fin


def _gdn_sample_kernel(z_ref, buf_ref, s_ref, cw_ref, alog_ref, dtb_ref, gn_ref,
                       o_ref, snew_ref, bufnew_ref, *, h_a, dk, width):
    hd = h_a * dk
    raw = z_ref[0][:, 0:3 * hd]
    buf = buf_ref[0]
    y = raw * cw_ref[width - 1:width, :]
    for j in range(width - 1):
        y = y + buf[j:j + 1, :] * cw_ref[j:j + 1, :]
    y = _silu(y)
    bufnew_ref[0, 0:width - 2, :] = buf[1:width - 1, :]
    bufnew_ref[0, width - 2:width - 1, :] = raw
    za = z_ref[0][:, 3 * hd:4 * hd]
    beta_row, g_row = _gdn_gates(z_ref[0][:, 4 * hd:4 * hd + LANES], alog_ref[...], dtb_ref[...])
    lane1 = lax.broadcasted_iota(jnp.int32, (1, LANES), 1)
    row = lax.broadcasted_iota(jnp.int32, (dk, LANES), 0)
    col = lax.broadcasted_iota(jnp.int32, (dk, LANES), 1)
    eye = row == col
    for h in range(h_a):
        q = y[:, h * dk:(h + 1) * dk]
        k = y[:, hd + h * dk:hd + (h + 1) * dk]
        v = y[:, 2 * hd + h * dk:2 * hd + (h + 1) * dk]
        q = q * lax.rsqrt(jnp.sum(q * q, axis=-1, keepdims=True) + EPS) * (dk ** -0.5)
        k = k * lax.rsqrt(jnp.sum(k * k, axis=-1, keepdims=True) + EPS)
        beta = jnp.sum(jnp.where(lane1 == h, beta_row, 0.0), axis=1, keepdims=True)
        a = jnp.exp(jnp.sum(jnp.where(lane1 == h + h_a, g_row, 0.0), axis=1, keepdims=True))
        k_col = jnp.sum(jnp.where(eye, k, 0.0), axis=1, keepdims=True)
        q_col = jnp.sum(jnp.where(eye, q, 0.0), axis=1, keepdims=True)
        s = a * s_ref[0, h]
        u = beta * (v - jnp.sum(s * k_col, axis=0, keepdims=True))
        s_new = s + k_col * u
        snew_ref[0, h] = s_new
        o = jnp.sum(s_new * q_col, axis=0, keepdims=True)
        ms = jnp.mean(o * o, axis=-1, keepdims=True)
        o_ref[0, :, h * dk:(h + 1) * dk] = (
            o * lax.rsqrt(ms + EPS) * gn_ref[...] * _silu(za[:, h * dk:(h + 1) * dk])
        ).astype(o_ref.dtype)


def _gdn_sample(z3, buf, s0, conv_w, a_log, dt_bias, gnorm, h_a, dk):
    b, _, ncol = z3.shape
    width = conv_w.shape[0]
    hd = h_a * dk
    alog_row, dtb_row = _gate_rows(a_log, dt_bias, h_a)
    return pl.pallas_call(
        functools.partial(_gdn_sample_kernel, h_a=h_a, dk=dk, width=width),
        grid=(b,),
        in_specs=[pl.BlockSpec((1, 1, ncol), lambda i: (i, 0, 0)),
                  pl.BlockSpec((1, width - 1, 3 * hd), lambda i: (i, 0, 0)),
                  pl.BlockSpec((1, h_a, dk, LANES), lambda i: (i, 0, 0, 0)),
                  pl.BlockSpec((width, 3 * hd), lambda i: (0, 0)),
                  pl.BlockSpec((1, LANES), lambda i: (0, 0)),
                  pl.BlockSpec((1, LANES), lambda i: (0, 0)),
                  pl.BlockSpec((1, LANES), lambda i: (0, 0))],
        out_specs=[pl.BlockSpec((1, 1, hd), lambda i: (i, 0, 0)),
                   pl.BlockSpec((1, h_a, dk, LANES), lambda i: (i, 0, 0, 0)),
                   pl.BlockSpec((1, width - 1, 3 * hd), lambda i: (i, 0, 0))],
        out_shape=[jax.ShapeDtypeStruct((b, 1, hd), BF16),
                   jax.ShapeDtypeStruct((b, h_a, dk, LANES), F32),
                   jax.ShapeDtypeStruct((b, width - 1, 3 * hd), F32)],
        compiler_params=_params("parallel"),
        name="gdn_sample",
    )(z3, buf, s0, conv_w, alog_row, dtb_row, gnorm)


def _diff_lambda(lamp_ref, lam_init):
    lp = lamp_ref[...]
    e1 = jnp.exp(jnp.sum(lp[0:1] * lp[1:2], axis=1, keepdims=True))
    e2 = jnp.exp(jnp.sum(lp[2:3] * lp[3:4], axis=1, keepdims=True))
    return e1 - e2 + lam_init


def _attn_prompt_kernel(qi_ref, kj_ref, q_ref, k_ref, v_ref, lamp_ref, g_ref, o_ref,
                        m_ref, l_ref, acc_ref, *, dq, lam_init):
    p = pl.program_id(1)
    qi = qi_ref[p]
    kj = kj_ref[p]

    @pl.when(kj == 0)
    def _():
        m_ref[...] = jnp.full(m_ref.shape, NEG_BIG, F32)
        l_ref[...] = jnp.zeros(l_ref.shape, F32)
        acc_ref[...] = jnp.zeros(acc_ref.shape, F32)

    def step(masked):
        v = v_ref[...]
        t = q_ref.shape[0]
        slab = min(t, ATTN_SLAB)
        for c in range(2):
            s = lax.dot_general(q_ref[:, c * dq:(c + 1) * dq], k_ref[:, c * dq:(c + 1) * dq],
                                NT_DIMS, preferred_element_type=F32)
            p_parts, alpha_parts = [], []
            for r in range(0, t, slab):
                sr = s[r:r + slab, :]
                if masked:
                    row = lax.broadcasted_iota(jnp.int32, sr.shape, 0) + r
                    col = lax.broadcasted_iota(jnp.int32, sr.shape, 1)
                    sr = jnp.where(row >= col, sr, NEG_BIG)
                m_prev = m_ref[c, r:r + slab, :]
                m_new = jnp.maximum(m_prev, jnp.max(sr, axis=1, keepdims=True))
                alpha = jnp.exp(m_prev - m_new)
                pm = jnp.exp(sr - m_new)
                l_ref[c, r:r + slab, :] = alpha * l_ref[c, r:r + slab, :] + jnp.sum(pm, axis=1, keepdims=True)
                m_ref[c, r:r + slab, :] = m_new
                p_parts.append(pm.astype(BF16))
                alpha_parts.append(alpha)
            p = jnp.concatenate(p_parts, axis=0)
            alpha = jnp.concatenate(alpha_parts, axis=0)
            acc_ref[c] = alpha * acc_ref[c] + jnp.dot(p, v, preferred_element_type=F32)

    @pl.when(kj < qi)
    def _():
        step(False)

    @pl.when(kj == qi)
    def _():
        step(True)
        lam = _diff_lambda(lamp_ref, lam_init)
        o = acc_ref[0] / l_ref[0] - lam * (acc_ref[1] / l_ref[1])
        ms = jnp.mean(o * o, axis=-1, keepdims=True)
        o_ref[...] = (o * lax.rsqrt(ms + SUBLN_EPS) * g_ref[...]
                      * (1.0 - lam_init)).astype(o_ref.dtype)


def _attn_prompt(q, k, v, cols, lam_params, subln_g, h_b, dq, lam_init):
    l = q.shape[0]
    dv = 2 * dq
    t = _tile(l, TILES["attn"], LANES)
    nb = l // t
    pairs = [(i, j) for i in range(nb) for j in range(i + 1)]
    qi_tab = jnp.asarray([a for a, _ in pairs], jnp.int32)
    kj_tab = jnp.asarray([b for _, b in pairs], jnp.int32)
    cq, ck, cv = (c // dv for c in cols)
    grid_spec = pltpu.PrefetchScalarGridSpec(
        num_scalar_prefetch=2,
        grid=(h_b, len(pairs)),
        in_specs=[pl.BlockSpec((t, dv), lambda h, p, qi, kj: (qi[p], h + cq)),
                  pl.BlockSpec((t, dv), lambda h, p, qi, kj: (kj[p], h + ck)),
                  pl.BlockSpec((t, dv), lambda h, p, qi, kj: (kj[p], h + cv)),
                  pl.BlockSpec((4, dq), lambda h, p, qi, kj: (0, 0)),
                  pl.BlockSpec((1, dv), lambda h, p, qi, kj: (0, 0))],
        out_specs=pl.BlockSpec((t, dv), lambda h, p, qi, kj: (qi[p], h)),
        scratch_shapes=[pltpu.VMEM((2, t, 1), F32), pltpu.VMEM((2, t, 1), F32),
                        pltpu.VMEM((2, t, dv), F32)],
    )
    return pl.pallas_call(
        functools.partial(_attn_prompt_kernel, dq=dq, lam_init=lam_init),
        grid_spec=grid_spec,
        out_shape=jax.ShapeDtypeStruct((l, h_b * dv), BF16),
        compiler_params=_params("parallel", "arbitrary"),
        name="attn_prompt",
    )(qi_tab, kj_tab, q, k, v, lam_params, subln_g)


def _attn_decode_kernel(pt_ref, q_ref, kn_ref, vn_ref, *rest, h_b, dq, lam_init, pp):
    del pt_ref
    kc_refs = rest[:pp]
    vc_refs = rest[pp:2 * pp]
    lamp_ref, g_ref, o_ref, m_ref, l_ref, acc_ref = rest[2 * pp:]
    p = pl.program_id(1)
    dv = 2 * dq
    page = kc_refs[0].shape[2]
    assert h_b & (h_b - 1) == 0
    qm = q_ref[0]

    @pl.when(p == 0)
    def _():
        m_ref[...] = jnp.sum(qm * kn_ref[0], axis=1, keepdims=True)
        l_ref[...] = jnp.ones(l_ref.shape, F32)
        acc_ref[...] = vn_ref[0]

    qb = qm.astype(BF16)
    scores = []
    for j in range(pp):
        k2 = kc_refs[j][0, 0].reshape(page * h_b, dv).astype(BF16)
        s = lax.dot_general(qb, k2, NT_DIMS, preferred_element_type=F32)
        sub = lax.broadcasted_iota(jnp.int32, s.shape, 0)
        lane = lax.broadcasted_iota(jnp.int32, s.shape, 1)
        scores.append(jnp.where((lane & (h_b - 1)) == (sub >> 1), s, NEG_BIG))
    m_prev = m_ref[...]
    m_new = m_prev
    for s in scores:
        m_new = jnp.maximum(m_new, jnp.max(s, axis=1, keepdims=True))
    alpha = jnp.exp(m_prev - m_new)
    l_new = alpha * l_ref[...]
    acc = alpha * acc_ref[...]
    for j in range(pp):
        pm = jnp.exp(scores[j] - m_new)
        l_new = l_new + jnp.sum(pm, axis=1, keepdims=True)
        v2 = vc_refs[j][0, 0].reshape(page * h_b, dv).astype(BF16)
        acc = acc + jnp.dot(pm.astype(BF16), v2, preferred_element_type=F32)
    l_ref[...] = l_new
    acc_ref[...] = acc
    m_ref[...] = m_new

    @pl.when(p == pl.num_programs(1) - 1)
    def _():
        lam = _diff_lambda(lamp_ref, lam_init)
        for h in range(h_b):
            o1 = acc_ref[2 * h:2 * h + 1, :] / l_ref[2 * h:2 * h + 1, :]
            o2 = acc_ref[2 * h + 1:2 * h + 2, :] / l_ref[2 * h + 1:2 * h + 2, :]
            o = o1 - lam * o2
            ms = jnp.mean(o * o, axis=-1, keepdims=True)
            o_ref[0, :, h * dv:(h + 1) * dv] = (o * lax.rsqrt(ms + SUBLN_EPS) * g_ref[...]
                                                * (1.0 - lam_init)).astype(o_ref.dtype)


def _attn_decode(qn, kn, vn, cache_k, cache_v, page_table, lam_params, subln_g, h_b, dq,
                 lam_init, lyr):
    b = qn.shape[0]
    n_pages = page_table.shape[1]
    page = cache_k.shape[2]
    dv = 2 * dq
    nsub = 2 * h_b
    pp = _tile(n_pages, TILES["decode_pages"], 1)
    zeros = jnp.zeros((b, h_b, dq), F32)

    def sub_rows(x):
        x4 = x.reshape(b, h_b, 2, dq)
        return jnp.stack([jnp.concatenate([x4[:, :, 0], zeros], axis=-1),
                          jnp.concatenate([zeros, x4[:, :, 1]], axis=-1)], axis=2).reshape(b, nsub, dv)

    qm = sub_rows(qn)
    km = sub_rows(kn)
    vm = jnp.repeat(vn.reshape(b, h_b, 1, dv), 2, axis=2).reshape(b, nsub, dv)
    rows = lambda: pl.BlockSpec((1, nsub, dv), lambda i, p, pt: (i, 0, 0))

    def pages(j):
        return pl.BlockSpec((1, 1, page, h_b, dv), lambda i, p, pt: (lyr, pt[i, p * pp + j], 0, 0, 0))

    grid_spec = pltpu.PrefetchScalarGridSpec(
        num_scalar_prefetch=1,
        grid=(b, n_pages // pp),
        in_specs=([rows(), rows(), rows()] + [pages(j) for j in range(pp)]
                  + [pages(j) for j in range(pp)]
                  + [pl.BlockSpec((4, dq), lambda i, p, pt: (0, 0)),
                     pl.BlockSpec((1, dv), lambda i, p, pt: (0, 0))]),
        out_specs=pl.BlockSpec((1, 1, h_b * dv), lambda i, p, pt: (i, 0, 0)),
        scratch_shapes=[pltpu.VMEM((nsub, 1), F32), pltpu.VMEM((nsub, 1), F32),
                        pltpu.VMEM((nsub, dv), F32)],
    )
    return pl.pallas_call(
        functools.partial(_attn_decode_kernel, h_b=h_b, dq=dq, lam_init=lam_init, pp=pp),
        grid_spec=grid_spec,
        out_shape=jax.ShapeDtypeStruct((b, 1, h_b * dv), BF16),
        compiler_params=_params("parallel", "arbitrary"),
        name="attn_decode",
    )(page_table, qm, km, vm, *([cache_k] * pp), *([cache_v] * pp), lam_params, subln_g)


def _ep_plain(accs, e, o, carry, first, last):
    o[0][...] = accs[0].astype(o[0].dtype)


def _ep_bias(accs, e, o, carry, first, last):
    o[0][...] = accs[0] + e[0][...]


def _ep_silu(accs, e, o, carry, first, last):
    o[0][...] = _silu(accs[0]).astype(o[0].dtype)


def _ep_sigmoid(accs, e, o, carry, first, last):
    o[0][...] = _sigmoid(accs[0]).astype(o[0].dtype)


def _group_norm_store(y, gain, eps, out_refs):
    tn = y.shape[1]
    for g in range(tn // LANES):
        sl = slice(g * LANES, (g + 1) * LANES)
        blk = y[:, sl]
        ms = jnp.sum(blk * blk, axis=-1, keepdims=True) / LANES
        val = blk * lax.rsqrt(ms + eps) * gain[:, sl]
        for r in out_refs:
            r[:, sl] = val.astype(r.dtype)


def _ep_qknorm(accs, e, o, carry, first, last):
    _group_norm_store(accs[0], e[0][...], EPS, o)


def _ep_copy2(accs, e, o, carry, first, last):
    for r in o:
        r[...] = accs[0].astype(r.dtype)


def _sample_post_kernel(q_ref, k_ref, ga_ref, gb_ref, gq_ref, gk_ref, qo_ref, ko_ref, sa_ref, sb_ref):
    _group_norm_store(q_ref[...], gq_ref[...], EPS, [qo_ref])
    _group_norm_store(k_ref[...], gk_ref[...], EPS, [ko_ref])
    sa_ref[...] = _sigmoid(ga_ref[...])
    sb_ref[...] = _sigmoid(gb_ref[...])


def _ep_gdn_conv(accs, e, o, carry, first, last, *, width):
    raw = accs[0]
    taps = [e[j][...] for j in range(width)]
    nflag = e[width][...]
    gain = e[width + 1][...]
    y, head = _conv_taps(raw, carry[0], taps, first)

    def post(val, rows):
        val = _silu(val)
        tn = val.shape[1]
        for g in range(tn // LANES):
            sl = slice(g * LANES, (g + 1) * LANES)
            blk = val[:, sl]
            ss = jnp.sum(blk * blk, axis=-1, keepdims=True)
            scale = jnp.where(nflag[:, sl] > 0.5, lax.rsqrt(ss + EPS), 1.0) * gain[:, sl]
            o[0][rows, sl] = (blk * scale).astype(o[0].dtype)

    post(y, slice(None))
    post(head, slice(0, SUBLANES))
    if last:
        o[1][...] = raw[raw.shape[0] - SUBLANES:]


def _ep_branch(accs, e, o, carry, first, last):
    o[0][...] = (e[0][...].astype(F32) * accs[0] + e[1][...].astype(F32) * accs[1]).astype(o[0].dtype)


def _ep_residual(accs, e, o, carry, first, last):
    o[0][...] = e[0][...] + e[1][...] * accs[0]


def _ep_ffn_prompt(accs, e, o, carry, first, last, *, width):
    outs = []
    for d in range(2):
        taps = [e[d * (width + 1) + j][...] for j in range(width)]
        bias = e[d * (width + 1) + width][...]
        y, head = _conv_taps(accs[d], carry[d], taps, first)
        outs.append((y + bias, head + bias))
        if last:
            o[1 + d][...] = accs[d][accs[d].shape[0] - SUBLANES:]
    o[0][...] = (_silu(outs[0][0]) * outs[1][0]).astype(o[0].dtype)
    o[0][0:SUBLANES, :] = (_silu(outs[0][1]) * outs[1][1]).astype(o[0].dtype)


def _ep_ffn_sample(accs, e, o, carry, first, last, *, width):
    vals = []
    per = (width - 1) + width + 1
    for d in range(2):
        base = d * per
        y = accs[d] * e[base + (width - 1) + width - 1][...]
        for j in range(width - 1):
            y = y + e[base + j][...] * e[base + (width - 1) + j][...]
        vals.append(y + e[base + per - 1][...])
        o[1 + d][...] = accs[d]
    o[0][...] = (_silu(vals[0]) * vals[1]).astype(o[0].dtype)


def kernel(x_prompt, x_sample, c_prompt, c_sample, cache_k, cache_v, state_gdn, state_gdn_conv, state_ffn_conv, page_table, w_ada, b_ada, norm1_g, norm2_g, w_in, gdn_conv_w, gdn_a_log, gdn_dt_bias, gdn_norm_g, diff_q_norm_g, diff_k_norm_g, diff_lambda, diff_subln_g, w_branch_a, w_branch_b, w_o, w_up, ffn_conv_w, ffn_conv_b, w_down):
    depth = w_in.shape[0]
    assert depth == 1 and x_prompt.shape[0] == 1 and x_sample.shape[1] == 1
    lyr = 0
    lam_init = 0.8 - 0.6 * math.exp(-0.3 * lyr)
    d = x_prompt.shape[-1]
    l = x_prompt.shape[1]
    nb = x_sample.shape[0]
    h_a, dk, dva = state_gdn.shape[2:]
    h_b = cache_k.shape[3]
    dq = cache_k.shape[4] // 2
    dvb = cache_v.shape[4]
    assert dk == LANES and dva == LANES and dq == LANES and dvb == 2 * dq
    d_ff = w_down.shape[1]
    gw = gdn_conv_w.shape[1]
    fw = ffn_conv_w.shape[1]
    hd_a = h_a * dk
    hd_b = h_b * dvb

    w_in0 = w_in[lyr]
    c_za = 3 * hd_a
    c_ba = c_za + hd_a
    n_head = c_ba + LANES
    c_tail = c_ba + 2 * h_a
    assert c_tail + 3 * hd_b + 2 * d == w_in0.shape[1] and 2 * h_a <= LANES
    w_tail = w_in0[:, c_tail:].astype(BF16)
    t_qb, t_kb, t_vb, t_ga, t_gb = 0, hd_b, 2 * hd_b, 3 * hd_b, 3 * hd_b + d
    w_up0 = w_up[lyr]
    w_dn = w_down[lyr].astype(BF16)
    w_o_b = w_o[lyr].astype(BF16)
    w_ba_b = w_branch_a[lyr].astype(BF16)
    w_bb_b = w_branch_b[lyr].astype(BF16)

    row = lambda v: v.reshape(1, -1).astype(F32)
    tile_rows = lambda v, reps: jnp.tile(row(v), (1, reps))

    c_all = jnp.concatenate([c_prompt, c_sample], axis=0).astype(F32)
    mp = -(-c_all.shape[0] // 16) * 16
    c_all = jnp.pad(c_all, ((0, mp - c_all.shape[0]), (0, 0)))
    (mod,) = _matmul("adaln", [c_all], [(w_ada[lyr], 0)], [(0, 0)], 6 * d, _ep_bias, [(F32, "tile")],
                     tm=mp, tn=_tile(6 * d, 512, LANES), extras=[(row(b_ada[lyr]), "row", 0)],
                     a_fn=_silu)
    mod_p = [mod[0:1, j * d:(j + 1) * d] for j in range(6)]
    mod_s = [mod[1:1 + nb, j * d:(j + 1) * d] for j in range(6)]

    gq = tile_rows(diff_q_norm_g[lyr], 2 * h_b) * (dq ** -0.5)
    gk = tile_rows(diff_k_norm_g[lyr], 2 * h_b)
    cw = gdn_conv_w[lyr].astype(F32)
    conv_nflag = jnp.concatenate([jnp.ones((1, 2 * hd_a), F32), jnp.zeros((1, hd_a), F32)], axis=1)
    conv_gain = jnp.concatenate([jnp.full((1, hd_a), dk ** -0.5, F32), jnp.ones((1, 2 * hd_a), F32)], axis=1)
    gnorm = row(gdn_norm_g[lyr])
    subln = row(diff_subln_g[lyr])
    lamp = diff_lambda[lyr].astype(F32)
    fcw = ffn_conv_w[lyr].astype(F32)
    fcb = row(ffn_conv_b[lyr])
    a_log = gdn_a_log[lyr].astype(F32)
    dt_bias = gdn_dt_bias[lyr].astype(F32)

    xp = x_prompt[0].astype(F32)
    tm = _tile(l, TILES["mm_m"], 16)
    tn = TILES["mm_n"]
    split = 2 if tm % 32 == 0 else 1
    h1 = _normmod(xp, row(norm1_g[lyr]), mod_p[1], mod_p[0])

    qkv_c, qkv_tail = _matmul(
        "p_qkv_conv", [h1], [(w_in0, 0)], [(0, 0)], 3 * hd_a,
        functools.partial(_ep_gdn_conv, width=gw), [(BF16, "tile"), (F32, "tail")],
        tm=tm, tn=_tile(hd_a, tn, LANES), order="nm", carry_rows=1, cast_once=True, row_split=split,
        extras=[(cw[j:j + 1], "row", 0) for j in range(gw)] + [(conv_nflag, "row", 0), (conv_gain, "row", 0)])
    (za_p,) = _matmul("p_za", [h1], [(w_in0, c_za)], [(0, 0)], hd_a, _ep_silu, [(BF16, "tile")],
                      tm=tm, tn=_tile(hd_a, tn, LANES), order="nm", cast_once=True, row_split=split)
    (ba_p,) = _matmul("p_ba", [h1], [(w_in0, c_ba)], [(0, 0)], LANES, _ep_plain, [(F32, "tile")],
                      tm=tm, tn=LANES, order="nm", cast_once=True)
    tn_b = _tile(hd_b, tn, LANES)
    (qb_p,) = _matmul("p_qb", [h1], [(w_tail, t_qb)], [(0, 0)], hd_b, _ep_qknorm, [(BF16, "tile")],
                      tm=tm, tn=tn_b, extras=[(gq, "row", 0)], row_split=split)
    kb_p, kb_p16 = _matmul("p_kb", [h1], [(w_tail, t_kb)], [(0, 0)], hd_b, _ep_qknorm,
                           [(F32, "tile"), (BF16, "tile")], tm=tm, tn=tn_b, extras=[(gk, "row", 0)],
                           row_split=split)
    vb_p, vb_p16 = _matmul("p_vb", [h1], [(w_tail, t_vb)], [(0, 0)], hd_b, _ep_copy2,
                           [(F32, "tile"), (BF16, "tile")], tm=tm, tn=tn_b, row_split=split)
    tn_d = _tile(d, tn, LANES)
    (gates_p,) = _matmul("p_gates", [h1], [(w_tail, t_ga)], [(0, 0)], 2 * d, _ep_sigmoid, [(BF16, "tile")],
                         tm=tm, tn=tn_d, row_split=split)

    oa_p, s_p = _gdn_prompt(qkv_c, ba_p, za_p, a_log, dt_bias, gnorm, h_a, dk)
    ob_p = _attn_prompt(qb_p, kb_p16, vb_p16, (0, 0, 0), lamp, subln, h_b, dq, lam_init)

    (mixed_p,) = _matmul("p_branch", [oa_p, ob_p], [(w_ba_b, 0), (w_bb_b, 0)], [(0, 0), (1, 1)], d,
                         _ep_branch, [(BF16, "tile")], tm=tm, tn=tn_d, row_split=split,
                         extras=[(gates_p, "tile", 0), (gates_p, "tile", d)])
    (x2_p,) = _matmul("p_wo", [mixed_p], [(w_o_b, 0)], [(0, 0)], d, _ep_residual, [(F32, "tile")],
                      tm=tm, tn=tn_d, row_split=split, extras=[(xp, "tile", 0), (mod_p[2], "row", 0)])
    h2 = _normmod(x2_p, row(norm2_g[lyr]), mod_p[4], mod_p[3])
    tn_u = _tile(d_ff, TILES["up_n"], LANES)
    assert d_ff % tn_u == 0
    ffn_rows = ([(fcw[j:j + 1], "row", 0) for j in range(fw)] + [(fcb, "row", 0)]
                + [(fcw[j:j + 1], "row", d_ff) for j in range(fw)] + [(fcb, "row", d_ff)])
    act_f, tail_g, tail_v = _matmul(
        "p_ffn_up", [h2], [(w_up0, 0), (w_up0, d_ff)], [(0, 0), (0, 1)], d_ff,
        functools.partial(_ep_ffn_prompt, width=fw),
        [(BF16, "tile"), (F32, "tail"), (F32, "tail")],
        tm=tm, tn=tn_u, order="nm", carry_rows=2, cast_once=True, row_split=split, extras=ffn_rows)
    (y_p,) = _matmul("p_down", [act_f], [(w_dn, 0)], [(0, 0)], d, _ep_residual, [(F32, "tile")],
                     tm=tm, tn=tn_d, tk=_tile(d_ff, TILES["down_k"], LANES),
                     extras=[(x2_p, "tile", 0), (mod_p[5], "row", 0)])

    xs = x_sample[:, 0, :].astype(F32)
    tns = TILES["small_n"]
    h1s = _normmod(xs, row(norm1_g[lyr]), mod_s[1], mod_s[0])
    (zh_s,) = _matmul("s_in_head", [h1s], [(w_in0, 0)], [(0, 0)], n_head, _ep_plain, [(F32, "tile")],
                      tm=nb, tn=_tile(n_head, tns, LANES))
    (zt_s,) = _matmul("s_in_tail", [h1s], [(w_tail, 0)], [(0, 0)], w_tail.shape[1], _ep_plain,
                      [(F32, "tile")], tm=nb, tn=_tile(w_tail.shape[1], tns, LANES))
    oa_s, s_s, gbuf_s = _gdn_sample(zh_s[:, None, :], state_gdn_conv[lyr].astype(F32),
                                    state_gdn[lyr].astype(F32), cw, a_log, dt_bias, gnorm, h_a, dk)
    vb_s = zt_s[:, t_vb:t_vb + hd_b]
    qn_s, kn_s, sga_s, sgb_s = pl.pallas_call(
        _sample_post_kernel,
        out_shape=[jax.ShapeDtypeStruct((nb, hd_b), F32), jax.ShapeDtypeStruct((nb, hd_b), F32),
                   jax.ShapeDtypeStruct((nb, d), F32), jax.ShapeDtypeStruct((nb, d), F32)],
        name="sample_post",
    )(zt_s[:, t_qb:t_qb + hd_b], zt_s[:, t_kb:t_kb + hd_b], zt_s[:, t_ga:t_ga + d],
      zt_s[:, t_gb:t_gb + d], gq, gk)
    ob_s = _attn_decode(qn_s, kn_s, vb_s, cache_k, cache_v, page_table.astype(jnp.int32), lamp, subln,
                        h_b, dq, lam_init, lyr)

    tn_ds = _tile(d, tns, LANES)
    (mixed_s,) = _matmul("s_branch", [oa_s[:, 0, :], ob_s[:, 0, :]], [(w_ba_b, 0), (w_bb_b, 0)],
                         [(0, 0), (1, 1)], d, _ep_branch, [(BF16, "tile")], tm=nb, tn=tn_ds,
                         extras=[(sga_s, "tile", 0), (sgb_s, "tile", 0)])
    (x2_s,) = _matmul("s_wo", [mixed_s], [(w_o_b, 0)], [(0, 0)], d, _ep_residual, [(F32, "tile")],
                      tm=nb, tn=tn_ds, extras=[(xs, "tile", 0), (mod_s[2], "tile", 0)])
    h2s = _normmod(x2_s, row(norm2_g[lyr]), mod_s[4], mod_s[3])
    fbuf = state_ffn_conv[lyr].astype(F32)
    ffn_ex_s = []
    for off in (0, d_ff):
        ffn_ex_s += [(fbuf[:, j, :], "tile", off) for j in range(fw - 1)]
        ffn_ex_s += [(fcw[j:j + 1], "row", off) for j in range(fw)] + [(fcb, "row", off)]
    tn_us = _tile(d_ff, tns, LANES)
    act_s, up_g, up_v = _matmul(
        "s_ffn_up", [h2s], [(w_up0, 0), (w_up0, d_ff)], [(0, 0), (0, 1)], d_ff,
        functools.partial(_ep_ffn_sample, width=fw),
        [(BF16, "tile"), (F32, "tile"), (F32, "tile")],
        tm=nb, tn=tn_us, extras=ffn_ex_s)
    (y_s,) = _matmul("s_down", [act_s], [(w_dn, 0)], [(0, 0)], d, _ep_residual, [(F32, "tile")],
                     tm=nb, tn=tn_ds, tk=_tile(d_ff, TILES["down_k"], LANES),
                     extras=[(x2_s, "tile", 0), (mod_s[5], "tile", 0)])

    y_prompt = y_p[None].astype(x_prompt.dtype)
    y_sample = y_s[:, None, :].astype(x_sample.dtype)
    k_prompt = kb_p.reshape(1, 1, l, h_b, 2 * dq)
    v_prompt = vb_p.reshape(1, 1, l, h_b, dvb)
    gdn_state_prompt = s_p[None, None]
    gdn_conv_prompt = qkv_tail[SUBLANES - (gw - 1):][None, None]
    ffn_conv_prompt = jnp.concatenate([tail_g[SUBLANES - (fw - 1):], tail_v[SUBLANES - (fw - 1):]],
                                      axis=1)[None, None]
    k_sample = kn_s.reshape(1, nb, 1, h_b, 2 * dq)
    v_sample = vb_s.reshape(1, nb, 1, h_b, dvb)
    gdn_state_sample = s_s[None]
    gdn_conv_sample = gbuf_s[None]
    up_new = jnp.concatenate([up_g, up_v], axis=1)
    ffn_conv_sample = jnp.concatenate([fbuf[:, 1:, :], up_new[:, None, :]], axis=1)[None]
    return (y_prompt, y_sample, k_prompt, v_prompt, gdn_state_prompt, gdn_conv_prompt,
            ffn_conv_prompt, k_sample, v_sample, gdn_state_sample, gdn_conv_sample, ffn_conv_sample)
```

```python
import functools
import math

import jax
import jax.numpy as jnp
from jax import lax
from jax.experimental import pallas as pl
from jax.experimental.pallas import tpu as pltpu

F32 = jnp.float32
BF16 = jnp.bfloat16

LANES = 128
SUBLANES = 8
VMEM_LIMIT_BYTES = 56 * 1024 * 1024

EPS = 1e-6
SUBLN_EPS = 1e-5
GDN_CHUNK = 128
MAX_UNSHIFTED_SCORE = 40.0
NEG_BIG = -1e30

TILES = dict(
    norm_rows=256,
    mm_m=1024, mm_n=512,
    up_n=256,
    down_k=5504,
    attn=1024,
    gdn_heads=8,
    decode_pages=8,
    small_n=1024,
)

NT_DIMS = (((1,), (1,)), ((), ()))


def _params(*sem):
    return pltpu.CompilerParams(dimension_semantics=sem, vmem_limit_bytes=VMEM_LIMIT_BYTES)


def _tile(dim, pref, align):
    if dim <= pref:
        return dim
    t = (pref // align) * align
    while t >= align:
        if dim % t == 0:
            return t
        t -= align
    return dim


def _sigmoid(x):
    return 1.0 / (1.0 + jnp.exp(-x))


def _silu(x):
    return x * _sigmoid(x)


def _softplus(x):
    return jnp.maximum(x, 0.0) + jnp.log1p(jnp.exp(-jnp.abs(x)))


def _normmod_kernel(x_ref, g_ref, sc_ref, sh_ref, o_ref):
    x = x_ref[...]
    ms = jnp.mean(x * x, axis=-1, keepdims=True)
    h = x * lax.rsqrt(ms + EPS) * g_ref[...]
    o_ref[...] = (h * (1.0 + sc_ref[...]) + sh_ref[...]).astype(o_ref.dtype)


def _normmod(x, g, sc, sh):
    m, d = x.shape
    tm = _tile(m, TILES["norm_rows"], SUBLANES)
    per_row = sc.shape[0] != 1
    mod_spec = (pl.BlockSpec((tm, d), lambda i: (i, 0)) if per_row
                else pl.BlockSpec((1, d), lambda i: (0, 0)))
    return pl.pallas_call(
        _normmod_kernel,
        grid=(m // tm,),
        in_specs=[pl.BlockSpec((tm, d), lambda i: (i, 0)),
                  pl.BlockSpec((1, d), lambda i: (0, 0)), mod_spec, mod_spec],
        out_specs=pl.BlockSpec((tm, d), lambda i: (i, 0)),
        out_shape=jax.ShapeDtypeStruct((m, d), BF16),
        compiler_params=_params("parallel"),
        name="normmod",
    )(x, g, sc, sh)


def _matmul(name, as_, ws, dots, n, epilogue, outs, *, tm, tn, extras=(), order="mn",
            tk=None, a_fn=None, carry_rows=0, cast_once=False, row_split=1, wt=False):
    m = as_[0].shape[0]
    kdim = as_[0].shape[1]
    tk = kdim if tk is None else tk
    nk = kdim // tk
    assert kdim % tk == 0 and m % tm == 0 and n % tn == 0 and tm % row_split == 0
    if nk > 1:
        assert all(a.shape[1] == kdim for a in as_) and row_split == 1 and not cast_once
    assert not cast_once or order == "nm"
    ni, nj = m // tm, n // tn
    na, nw, ne, no, nd = len(as_), len(ws), len(extras), len(outs), len(dots)
    rows = tm // row_split

    if order == "mn":
        grid = (ni, nj, nk)
        ij = lambda g0, g1: (g0, g1)
    else:
        grid = (nj, ni, nk)
        ij = lambda g0, g1: (g1, g0)

    def a_map(g0, g1, k):
        return (ij(g0, g1)[0], k)

    def w_map(off):
        return lambda g0, g1, k: (k, ij(g0, g1)[1] + off // tn)

    def row_map(off):
        return lambda g0, g1, k: (0, ij(g0, g1)[1] + off // tn)

    def tile_map(off):
        return lambda g0, g1, k: (ij(g0, g1)[0], ij(g0, g1)[1] + off // tn)

    in_specs = []
    for a in as_:
        in_specs.append(pl.BlockSpec((tm, tk if nk > 1 else a.shape[1]), a_map))
    for w, off in ws:
        if wt:
            assert nk == 1 and off % SUBLANES == 0
            in_specs.append(pl.BlockSpec(
                (pl.Element(tn), pl.Element(w.shape[1])),
                functools.partial(
                    lambda g0, g1, k, off: (pl.multiple_of(off + ij(g0, g1)[1] * tn, SUBLANES), 0),
                    off=off)))
        else:
            assert off % tn == 0
            in_specs.append(pl.BlockSpec((tk if nk > 1 else w.shape[0], tn), w_map(off)))
    for arr, kind, off in extras:
        assert off % tn == 0
        if kind == "row":
            in_specs.append(pl.BlockSpec((1, tn), row_map(off)))
        else:
            in_specs.append(pl.BlockSpec((tm, tn), tile_map(off)))
    out_specs, out_shapes = [], []
    for dtype, kind in outs:
        if kind == "tile":
            out_specs.append(pl.BlockSpec((tm, tn), tile_map(0)))
            out_shapes.append(jax.ShapeDtypeStruct((m, n), dtype))
        else:
            out_specs.append(pl.BlockSpec((SUBLANES, tn), row_map(0)))
            out_shapes.append(jax.ShapeDtypeStruct((SUBLANES, n), dtype))
    scratch = []
    if nk > 1:
        scratch += [pltpu.VMEM((tm, tn), F32) for _ in range(nd)]
    if cast_once:
        scratch += [pltpu.VMEM((tn, w.shape[1]) if wt else (w.shape[0], tn), BF16) for w, _ in ws]
    scratch += [pltpu.VMEM((SUBLANES, tn), F32) for _ in range(carry_rows)]

    def body(*refs):
        a_refs = refs[:na]
        w_refs = refs[na:na + nw]
        e_refs = refs[na + nw:na + nw + ne]
        o_refs = refs[na + nw + ne:na + nw + ne + no]
        s_refs = list(refs[na + nw + ne + no:])
        acc_refs = [s_refs.pop(0) for _ in range(nd)] if nk > 1 else []
        wb_refs = [s_refs.pop(0) for _ in range(nw)] if cast_once else list(w_refs)
        carry_refs = s_refs
        i, _ = ij(pl.program_id(0), pl.program_id(1))
        k = pl.program_id(2)

        def partial(d, r):
            ai, wi = dots[d]
            a = a_refs[ai][r * rows:(r + 1) * rows, :] if row_split > 1 else a_refs[ai][...]
            if a_fn is not None:
                a = a_fn(a)
            w = wb_refs[wi][...].astype(BF16)
            if wt:
                return lax.dot_general(a.astype(BF16), w, NT_DIMS, preferred_element_type=F32)
            return jnp.dot(a.astype(BF16), w, preferred_element_type=F32)

        if cast_once:
            @pl.when(i == 0)
            def _():
                for wi in range(nw):
                    wb_refs[wi][...] = w_refs[wi][...].astype(BF16)

        if nk == 1:
            for r in range(row_split):
                if row_split > 1:
                    piece = pl.ds(r * rows, rows)
                    ev = [e.at[piece] if extras[x][1] == "tile" else e for x, e in enumerate(e_refs)]
                    ov = [o.at[piece] if outs[x][1] == "tile" else o for x, o in enumerate(o_refs)]
                else:
                    ev, ov = e_refs, o_refs
                epilogue([partial(d, r) for d in range(nd)], ev, ov, carry_refs,
                         (i == 0) if r == 0 else False, r == row_split - 1)
        else:
            @pl.when(k == 0)
            def _():
                for d in range(nd):
                    acc_refs[d][...] = jnp.zeros((tm, tn), F32)

            for d in range(nd):
                acc_refs[d][...] += partial(d, 0)

            @pl.when(k == nk - 1)
            def _():
                epilogue([acc_refs[d][...] for d in range(nd)], e_refs, o_refs,
                         carry_refs, i == 0, True)

    return pl.pallas_call(
        body,
        grid=grid,
        in_specs=in_specs,
        out_specs=out_specs,
        out_shape=out_shapes,
        scratch_shapes=scratch,
        compiler_params=_params("arbitrary", "arbitrary", "arbitrary"),
        name=name,
    )(*as_, *[w for w, _ in ws], *[e for e, _, _ in extras])


def _conv_taps(raw, carry_ref, taps, first):
    width = len(taps)
    if first is not False:
        @pl.when(first)
        def _():
            carry_ref[...] = jnp.zeros(carry_ref.shape, F32)

    prev = carry_ref[...]
    rowid = lax.broadcasted_iota(jnp.int32, prev.shape, 0)
    y = raw * taps[width - 1]
    head = raw[0:SUBLANES] * taps[width - 1]
    for s in range(1, width):
        tap = taps[width - 1 - s]
        shifted = pltpu.roll(raw, s, axis=0)
        y = y + shifted * tap
        head = head + jnp.where(rowid < s, pltpu.roll(prev, s, axis=0),
                                shifted[0:SUBLANES]) * tap
    carry_ref[...] = raw[raw.shape[0] - SUBLANES:]
    return y, head


def _gdn_gates(ba, alog_row, dtb_row):
    return _sigmoid(ba), -jnp.exp(alog_row) * _softplus(ba + dtb_row)


def _gdn_prep_kernel(h_ref, wba_ref, alog_ref, dtb_ref, beta_ref, gc_ref, gct_ref, *, heads_pad, h_a):
    c = GDN_CHUNK
    ba = lax.dot_general(h_ref[...], wba_ref[...].astype(BF16), NT_DIMS, preferred_element_type=F32)
    beta, g = _gdn_gates(ba, alog_ref[...], dtb_ref[...])
    beta_ref[...] = beta
    row = lax.broadcasted_iota(jnp.int32, (c, LANES), 0)
    s = 1
    while s < c:
        g = g + jnp.where(row >= s, pltpu.roll(g, s, axis=0), 0.0)
        s *= 2
    gc_ref[...] = g
    gct_ref[...] = g.T[h_a:h_a + heads_pad, :]


def _gdn_intra_kernel(q_ref, k_ref, v_ref, beta_ref, gc_ref, gct_ref,
                      uv_ref, w_ref, aqk_ref, qdec_ref, kdect_ref, *, hb, h_a):
    c = GDN_CHUNK
    grp = pl.program_id(1)
    lane = lax.broadcasted_iota(jnp.int32, (c, LANES), 1)
    row = lax.broadcasted_iota(jnp.int32, (c, c), 0)
    col = lax.broadcasted_iota(jnp.int32, (c, c), 1)
    eye = jnp.where(row == col, 1.0, 0.0)
    levels = range(1, int(math.log2(c)))
    below = [((row >> (lvl + 1)) == (col >> (lvl + 1))) & ((row >> lvl) != (col >> lvl))
             for lvl in levels]
    pair = (row >> 1) == (col >> 1)
    beta_all = beta_ref[...]
    gc_all = gc_ref[...]
    heads = range(hb)
    sls = [slice(j * LANES, (j + 1) * LANES) for j in heads]
    beta_c, gcc, kb, mm, t = [], [], [], [], []
    for j in heads:
        h = grp * hb + j
        beta_c.append(jnp.sum(jnp.where(lane == h, beta_all, 0.0), axis=1, keepdims=True))
        gcc.append(jnp.sum(jnp.where(lane == h + h_a, gc_all, 0.0), axis=1, keepdims=True))
        gcr = gct_ref[pl.ds(h, 1), :]
        dec_incl = jnp.exp(jnp.where(row >= col, gcc[j] - gcr, NEG_BIG))
        k = k_ref[:, sls[j]]
        kb.append(k.astype(F32) * beta_c[j])
        mm.append(lax.dot_general(kb[j].astype(BF16), k, NT_DIMS, preferred_element_type=F32)
                  * jnp.where(row > col, dec_incl, 0.0))
        aqk = lax.dot_general(q_ref[:, sls[j]], k, NT_DIMS, preferred_element_type=F32) * dec_incl
        aqk_ref[:, sls[j]] = aqk.astype(BF16)
        t.append(eye - jnp.where(pair, mm[j], 0.0))
    for msk in below:
        tb = [t[j].astype(BF16) for j in heads]
        bt = [jnp.dot(jnp.where(msk, mm[j], 0.0).astype(BF16), tb[j], preferred_element_type=F32)
              for j in heads]
        t = [t[j] - jnp.dot(tb[j], bt[j].astype(BF16), preferred_element_type=F32) for j in heads]
    for j in heads:
        tb = t[j].astype(BF16)
        egc = jnp.exp(gcc[j])
        gl = gcc[j][c - 1:c, :]
        vb = v_ref[:, sls[j]].astype(F32) * beta_c[j]
        uv_ref[:, sls[j]] = jnp.dot(tb, vb.astype(BF16), preferred_element_type=F32)
        w_ref[:, sls[j]] = jnp.dot(tb, (kb[j] * egc).astype(BF16),
                                   preferred_element_type=F32).astype(BF16)
        qdec_ref[:, sls[j]] = (q_ref[:, sls[j]].astype(F32) * egc).astype(BF16)
        kdect_ref[sls[j], :] = (k_ref[:, sls[j]].astype(F32) * jnp.exp(gl - gcc[j])).T.astype(BF16)


def _gdn_rec_kernel(uv_ref, w_ref, aqk_ref, qdec_ref, kdect_ref, gct_ref, za_ref, gn_ref,
                    o_ref, sout_ref, s_ref, *, hb):
    c = GDN_CHUNK
    n = pl.program_id(1)
    g0 = pl.program_id(0)

    @pl.when(n == 0)
    def _():
        s_ref[...] = jnp.zeros(s_ref.shape, F32)

    heads = range(hb)
    sls = [slice(j * LANES, (j + 1) * LANES) for j in heads]
    s = [s_ref[j] for j in heads]
    sb = [s[j].astype(BF16) for j in heads]
    ub = [(uv_ref[:, sls[j]] - jnp.dot(w_ref[:, sls[j]], sb[j], preferred_element_type=F32)
           ).astype(BF16) for j in heads]
    oq = [jnp.dot(qdec_ref[:, sls[j]], sb[j], preferred_element_type=F32) for j in heads]
    for j in heads:
        gl = jnp.exp(gct_ref[pl.ds(g0 * hb + j, 1), c - 1:c])
        s_ref[j] = gl * s[j] + jnp.dot(kdect_ref[sls[j], :], ub[j], preferred_element_type=F32)
    for j in heads:
        o = oq[j] + jnp.dot(aqk_ref[:, sls[j]], ub[j], preferred_element_type=F32)
        ms = jnp.mean(o * o, axis=-1, keepdims=True)
        o_ref[:, sls[j]] = (o * lax.rsqrt(ms + EPS) * gn_ref[...]
                            * za_ref[:, sls[j]].astype(F32)).astype(o_ref.dtype)

    @pl.when(n == pl.num_programs(1) - 1)
    def _():
        sout_ref[...] = s_ref[...]


def _gate_rows(a_log, dt_bias, h_a):
    alog_row = jnp.zeros((1, LANES), F32).at[0, h_a:2 * h_a].set(a_log)
    dtb_row = jnp.zeros((1, LANES), F32).at[0, h_a:2 * h_a].set(dt_bias)
    return alog_row, dtb_row


def _gdn_prompt(qkv, h1, w_t, c_ba, za_silu, a_log, dt_bias, gnorm, h_a, dk):
    l = qkv.shape[0]
    kdim = h1.shape[1]
    c = GDN_CHUNK
    assert l % c == 0 and dk == LANES and 2 * h_a <= LANES and c_ba % LANES == 0
    nchunk = l // c
    hp = max(SUBLANES, -(-h_a // SUBLANES) * SUBLANES)
    alog_row, dtb_row = _gate_rows(a_log, dt_bias, h_a)

    beta, gc, gct = pl.pallas_call(
        functools.partial(_gdn_prep_kernel, heads_pad=hp, h_a=h_a),
        grid=(nchunk,),
        in_specs=[pl.BlockSpec((c, kdim), lambda n: (n, 0)),
                  pl.BlockSpec((LANES, kdim), lambda n: (c_ba // LANES, 0)),
                  pl.BlockSpec((1, LANES), lambda n: (0, 0)),
                  pl.BlockSpec((1, LANES), lambda n: (0, 0))],
        out_specs=[pl.BlockSpec((c, LANES), lambda n: (n, 0)),
                   pl.BlockSpec((c, LANES), lambda n: (n, 0)),
                   pl.BlockSpec((hp, c), lambda n: (0, n))],
        out_shape=[jax.ShapeDtypeStruct((l, LANES), F32),
                   jax.ShapeDtypeStruct((l, LANES), F32),
                   jax.ShapeDtypeStruct((hp, l), F32)],
        compiler_params=_params("parallel"),
        name="gdn_prep",
    )(h1, w_t, alog_row, dtb_row)

    hd = h_a * dk
    hb = _tile(h_a, TILES["gdn_heads"], 1)
    ng = h_a // hb
    blk = lambda off: pl.BlockSpec((c, hb * LANES), lambda n, g: (n, g + off))
    uv, w, aqk, qdec, kdect = pl.pallas_call(
        functools.partial(_gdn_intra_kernel, hb=hb, h_a=h_a),
        grid=(nchunk, ng),
        in_specs=[blk(0), blk(ng), blk(2 * ng),
                  pl.BlockSpec((c, LANES), lambda n, g: (n, 0)),
                  pl.BlockSpec((c, LANES), lambda n, g: (n, 0)),
                  pl.BlockSpec((hp, c), lambda n, g: (0, n))],
        out_specs=[blk(0), blk(0), blk(0), blk(0),
                   pl.BlockSpec((hb * LANES, c), lambda n, g: (g, n))],
        out_shape=[jax.ShapeDtypeStruct((l, hd), F32),
                   jax.ShapeDtypeStruct((l, hd), BF16),
                   jax.ShapeDtypeStruct((l, hd), BF16),
                   jax.ShapeDtypeStruct((l, hd), BF16),
                   jax.ShapeDtypeStruct((hd, l), BF16)],
        compiler_params=_params("parallel", "arbitrary"),
        name="gdn_intra",
    )(qkv, qkv, qkv, beta, gc, gct)

    hr = hb
    wide = lambda: pl.BlockSpec((c, hr * LANES), lambda g, n: (n, g))
    oa, s_fin = pl.pallas_call(
        functools.partial(_gdn_rec_kernel, hb=hr),
        grid=(h_a // hr, nchunk),
        in_specs=[wide(), wide(), wide(), wide(),
                  pl.BlockSpec((hr * LANES, c), lambda g, n: (g, n)),
                  pl.BlockSpec((hp, c), lambda g, n: (0, n)),
                  wide(),
                  pl.BlockSpec((1, LANES), lambda g, n: (0, 0))],
        out_specs=[wide(), pl.BlockSpec((hr, dk, LANES), lambda g, n: (g, 0, 0))],
        out_shape=[jax.ShapeDtypeStruct((l, hd), BF16),
                   jax.ShapeDtypeStruct((h_a, dk, LANES), F32)],
        scratch_shapes=[pltpu.VMEM((hr, dk, LANES), F32)],
        compiler_params=_params("parallel", "arbitrary"),
        name="gdn_rec",
    )(uv, w, aqk, qdec, kdect, gct, za_silu, gnorm)
    return oa, s_fin


def _gdn_sample_kernel(z_ref, buf_ref, s_ref, cw_ref, alog_ref, dtb_ref, gn_ref,
                       o_ref, snew_ref, bufnew_ref, *, h_a, dk, width):
    hd = h_a * dk
    raw = z_ref[0][:, 0:3 * hd]
    buf = buf_ref[0]
    y = raw * cw_ref[width - 1:width, :]
    for j in range(width - 1):
        y = y + buf[j:j + 1, :] * cw_ref[j:j + 1, :]
    y = _silu(y)
    bufnew_ref[0, 0:width - 2, :] = buf[1:width - 1, :]
    bufnew_ref[0, width - 2:width - 1, :] = raw
    za = z_ref[0][:, 3 * hd:4 * hd]
    beta_row, g_row = _gdn_gates(z_ref[0][:, 4 * hd:4 * hd + LANES], alog_ref[...], dtb_ref[...])
    lane1 = lax.broadcasted_iota(jnp.int32, (1, LANES), 1)
    row = lax.broadcasted_iota(jnp.int32, (dk, LANES), 0)
    col = lax.broadcasted_iota(jnp.int32, (dk, LANES), 1)
    eye = row == col
    for h in range(h_a):
        q = y[:, h * dk:(h + 1) * dk]
        k = y[:, hd + h * dk:hd + (h + 1) * dk]
        v = y[:, 2 * hd + h * dk:2 * hd + (h + 1) * dk]
        q = q * lax.rsqrt(jnp.sum(q * q, axis=-1, keepdims=True) + EPS) * (dk ** -0.5)
        k = k * lax.rsqrt(jnp.sum(k * k, axis=-1, keepdims=True) + EPS)
        beta = jnp.sum(jnp.where(lane1 == h, beta_row, 0.0), axis=1, keepdims=True)
        a = jnp.exp(jnp.sum(jnp.where(lane1 == h + h_a, g_row, 0.0), axis=1, keepdims=True))
        k_col = jnp.sum(jnp.where(eye, k, 0.0), axis=1, keepdims=True)
        q_col = jnp.sum(jnp.where(eye, q, 0.0), axis=1, keepdims=True)
        s = a * s_ref[0, h]
        u = beta * (v - jnp.sum(s * k_col, axis=0, keepdims=True))
        s_new = s + k_col * u
        snew_ref[0, h] = s_new
        o = jnp.sum(s_new * q_col, axis=0, keepdims=True)
        ms = jnp.mean(o * o, axis=-1, keepdims=True)
        o_ref[0, :, h * dk:(h + 1) * dk] = (
            o * lax.rsqrt(ms + EPS) * gn_ref[...] * _silu(za[:, h * dk:(h + 1) * dk])
        ).astype(o_ref.dtype)


def _gdn_sample(z3, buf, s0, conv_w, a_log, dt_bias, gnorm, h_a, dk):
    b, _, ncol = z3.shape
    width = conv_w.shape[0]
    hd = h_a * dk
    alog_row, dtb_row = _gate_rows(a_log, dt_bias, h_a)
    return pl.pallas_call(
        functools.partial(_gdn_sample_kernel, h_a=h_a, dk=dk, width=width),
        grid=(b,),
        in_specs=[pl.BlockSpec((1, 1, ncol), lambda i: (i, 0, 0)),
                  pl.BlockSpec((1, width - 1, 3 * hd), lambda i: (i, 0, 0)),
                  pl.BlockSpec((1, h_a, dk, LANES), lambda i: (i, 0, 0, 0)),
                  pl.BlockSpec((width, 3 * hd), lambda i: (0, 0)),
                  pl.BlockSpec((1, LANES), lambda i: (0, 0)),
                  pl.BlockSpec((1, LANES), lambda i: (0, 0)),
                  pl.BlockSpec((1, LANES), lambda i: (0, 0))],
        out_specs=[pl.BlockSpec((1, 1, hd), lambda i: (i, 0, 0)),
                   pl.BlockSpec((1, h_a, dk, LANES), lambda i: (i, 0, 0, 0)),
                   pl.BlockSpec((1, width - 1, 3 * hd), lambda i: (i, 0, 0))],
        out_shape=[jax.ShapeDtypeStruct((b, 1, hd), BF16),
                   jax.ShapeDtypeStruct((b, h_a, dk, LANES), F32),
                   jax.ShapeDtypeStruct((b, width - 1, 3 * hd), F32)],
        compiler_params=_params("parallel"),
        name="gdn_sample",
    )(z3, buf, s0, conv_w, alog_row, dtb_row, gnorm)


def _diff_lambda(lamp_ref, lam_init):
    lp = lamp_ref[...]
    e1 = jnp.exp(jnp.sum(lp[0:1] * lp[1:2], axis=1, keepdims=True))
    e2 = jnp.exp(jnp.sum(lp[2:3] * lp[3:4], axis=1, keepdims=True))
    return e1 - e2 + lam_init


def _attn_prompt_kernel(qi_ref, kj_ref, q_ref, k_ref, v_ref, lamp_ref, g_ref, o_ref,
                        m_ref, l_ref, acc_ref, *, dq, lam_init):
    p = pl.program_id(1)
    qi = qi_ref[p]
    kj = kj_ref[p]

    @pl.when(kj == 0)
    def _():
        m_ref[...] = jnp.full(m_ref.shape, NEG_BIG, F32)
        l_ref[...] = jnp.zeros(l_ref.shape, F32)
        acc_ref[...] = jnp.zeros(acc_ref.shape, F32)

    def step(masked):
        v = v_ref[...]
        for c in range(2):
            s = lax.dot_general(q_ref[:, c * dq:(c + 1) * dq], k_ref[:, c * dq:(c + 1) * dq],
                                NT_DIMS, preferred_element_type=F32)
            if masked:
                row = lax.broadcasted_iota(jnp.int32, s.shape, 0)
                col = lax.broadcasted_iota(jnp.int32, s.shape, 1)
                s = jnp.where(row >= col, s, NEG_BIG)
            m_prev = m_ref[c]
            m_new = jnp.maximum(m_prev, jnp.max(s, axis=1, keepdims=True))
            alpha = jnp.exp2(m_prev - m_new)
            pm = jnp.exp2(s - m_new)
            l_ref[c] = alpha * l_ref[c] + jnp.sum(pm, axis=1, keepdims=True)
            acc_ref[c] = alpha * acc_ref[c] + jnp.dot(pm.astype(BF16), v,
                                                      preferred_element_type=F32)
            m_ref[c] = m_new

    @pl.when(kj < qi)
    def _():
        step(False)

    @pl.when(kj == qi)
    def _():
        step(True)
        _attn_finish(acc_ref, l_ref[0], l_ref[1], lamp_ref, g_ref, o_ref, lam_init)


def _attn_finish(acc_ref, l0, l1, lamp_ref, g_ref, o_ref, lam_init):
    lam = _diff_lambda(lamp_ref, lam_init)
    o = acc_ref[0] / l0 - lam * (acc_ref[1] / l1)
    ms = jnp.mean(o * o, axis=-1, keepdims=True)
    o_ref[...] = (o * lax.rsqrt(ms + SUBLN_EPS) * g_ref[...] * (1.0 - lam_init)).astype(o_ref.dtype)


def _attn_prompt_bounded_kernel(qi_ref, kj_ref, q_ref, k_ref, v_ref, lamp_ref, g_ref, o_ref,
                                l_ref, acc_ref, *, dq, lam_init):
    p = pl.program_id(1)
    qi = qi_ref[p]
    kj = kj_ref[p]

    @pl.when(kj == 0)
    def _():
        l_ref[...] = jnp.zeros(l_ref.shape, F32)
        acc_ref[...] = jnp.zeros(acc_ref.shape, F32)

    def step(masked):
        v = v_ref[...]
        for c in range(2):
            s = lax.dot_general(q_ref[:, c * dq:(c + 1) * dq], k_ref[:, c * dq:(c + 1) * dq],
                                NT_DIMS, preferred_element_type=F32)
            pm = jnp.exp2(s)
            if masked:
                row = lax.broadcasted_iota(jnp.int32, s.shape, 0)
                col = lax.broadcasted_iota(jnp.int32, s.shape, 1)
                pm = jnp.where(row >= col, pm, 0.0)
            part = pm[:, 0:LANES]
            for g in range(1, pm.shape[1] // LANES):
                part = part + pm[:, g * LANES:(g + 1) * LANES]
            l_ref[c] += part
            acc_ref[c] += jnp.dot(pm.astype(BF16), v, preferred_element_type=F32)

    @pl.when(kj < qi)
    def _():
        step(False)

    @pl.when(kj == qi)
    def _():
        step(True)
        _attn_finish(acc_ref, jnp.sum(l_ref[0], axis=1, keepdims=True),
                     jnp.sum(l_ref[1], axis=1, keepdims=True), lamp_ref, g_ref, o_ref, lam_init)


def _attn_prompt(q, k, v, score_bound, lam_params, subln_g, h_b, dq, lam_init):
    l = q.shape[0]
    dv = 2 * dq
    t = _tile(l, TILES["attn"], LANES)
    nb = l // t
    pairs = [(i, j) for i in range(nb) for j in range(i + 1)]
    qi_tab = jnp.asarray([a for a, _ in pairs], jnp.int32)
    kj_tab = jnp.asarray([b for _, b in pairs], jnp.int32)

    def call(body, scratch, name):
        grid_spec = pltpu.PrefetchScalarGridSpec(
            num_scalar_prefetch=2,
            grid=(h_b, len(pairs)),
            in_specs=[pl.BlockSpec((t, dv), lambda h, p, qi, kj: (qi[p], h)),
                      pl.BlockSpec((t, dv), lambda h, p, qi, kj: (kj[p], h)),
                      pl.BlockSpec((t, dv), lambda h, p, qi, kj: (kj[p], h)),
                      pl.BlockSpec((4, dq), lambda h, p, qi, kj: (0, 0)),
                      pl.BlockSpec((1, dv), lambda h, p, qi, kj: (0, 0))],
            out_specs=pl.BlockSpec((t, dv), lambda h, p, qi, kj: (qi[p], h)),
            scratch_shapes=scratch,
        )
        return pl.pallas_call(
            functools.partial(body, dq=dq, lam_init=lam_init),
            grid_spec=grid_spec,
            out_shape=jax.ShapeDtypeStruct((l, h_b * dv), BF16),
            compiler_params=_params("parallel", "arbitrary"),
            name=name,
        )(qi_tab, kj_tab, q, k, v, lam_params, subln_g)

    bounded = lambda: call(_attn_prompt_bounded_kernel,
                           [pltpu.VMEM((2, t, LANES), F32), pltpu.VMEM((2, t, dv), F32)],
                           "attn_prompt_bounded")
    online = lambda: call(_attn_prompt_kernel,
                          [pltpu.VMEM((2, t, 1), F32), pltpu.VMEM((2, t, 1), F32),
                           pltpu.VMEM((2, t, dv), F32)], "attn_prompt_online")
    return lax.cond(score_bound <= MAX_UNSHIFTED_SCORE, bounded, online)


def _attn_decode_kernel(pt_ref, q_ref, kn_ref, vn_ref, *rest, h_b, dq, lam_init, pp):
    del pt_ref
    kc_refs = rest[:pp]
    vc_refs = rest[pp:2 * pp]
    lamp_ref, g_ref, o_ref, m_ref, l_ref, acc_ref = rest[2 * pp:]
    p = pl.program_id(1)
    dv = 2 * dq
    page = kc_refs[0].shape[2]
    assert h_b & (h_b - 1) == 0
    qm = q_ref[0]

    @pl.when(p == 0)
    def _():
        m_ref[...] = jnp.sum(qm * kn_ref[0], axis=1, keepdims=True)
        l_ref[...] = jnp.ones(l_ref.shape, F32)
        acc_ref[...] = vn_ref[0]

    qb = qm.astype(BF16)
    scores = []
    for j in range(pp):
        k2 = kc_refs[j][0, 0].reshape(page * h_b, dv).astype(BF16)
        s = lax.dot_general(qb, k2, NT_DIMS, preferred_element_type=F32)
        sub = lax.broadcasted_iota(jnp.int32, s.shape, 0)
        lane = lax.broadcasted_iota(jnp.int32, s.shape, 1)
        scores.append(jnp.where((lane & (h_b - 1)) == (sub >> 1), s, NEG_BIG))
    m_prev = m_ref[...]
    m_new = m_prev
    for s in scores:
        m_new = jnp.maximum(m_new, jnp.max(s, axis=1, keepdims=True))
    alpha = jnp.exp(m_prev - m_new)
    l_new = alpha * l_ref[...]
    acc = alpha * acc_ref[...]
    for j in range(pp):
        pm = jnp.exp(scores[j] - m_new)
        l_new = l_new + jnp.sum(pm, axis=1, keepdims=True)
        v2 = vc_refs[j][0, 0].reshape(page * h_b, dv).astype(BF16)
        acc = acc + jnp.dot(pm.astype(BF16), v2, preferred_element_type=F32)
    l_ref[...] = l_new
    acc_ref[...] = acc
    m_ref[...] = m_new

    @pl.when(p == pl.num_programs(1) - 1)
    def _():
        lam = _diff_lambda(lamp_ref, lam_init)
        for h in range(h_b):
            o1 = acc_ref[2 * h:2 * h + 1, :] / l_ref[2 * h:2 * h + 1, :]
            o2 = acc_ref[2 * h + 1:2 * h + 2, :] / l_ref[2 * h + 1:2 * h + 2, :]
            o = o1 - lam * o2
            ms = jnp.mean(o * o, axis=-1, keepdims=True)
            o_ref[0, :, h * dv:(h + 1) * dv] = (o * lax.rsqrt(ms + SUBLN_EPS) * g_ref[...]
                                                * (1.0 - lam_init)).astype(o_ref.dtype)


def _attn_decode(qn, kn, vn, cache_k, cache_v, page_table, lam_params, subln_g, h_b, dq,
                 lam_init, lyr):
    b = qn.shape[0]
    n_pages = page_table.shape[1]
    page = cache_k.shape[2]
    dv = 2 * dq
    nsub = 2 * h_b
    pp = _tile(n_pages, TILES["decode_pages"], 1)
    zeros = jnp.zeros((b, h_b, dq), F32)

    def sub_rows(x):
        x4 = x.reshape(b, h_b, 2, dq)
        return jnp.stack([jnp.concatenate([x4[:, :, 0], zeros], axis=-1),
                          jnp.concatenate([zeros, x4[:, :, 1]], axis=-1)], axis=2).reshape(b, nsub, dv)

    qm = sub_rows(qn)
    km = sub_rows(kn)
    vm = jnp.repeat(vn.reshape(b, h_b, 1, dv), 2, axis=2).reshape(b, nsub, dv)
    rows = lambda: pl.BlockSpec((1, nsub, dv), lambda i, p, pt: (i, 0, 0))

    def pages(j):
        return pl.BlockSpec((1, 1, page, h_b, dv), lambda i, p, pt: (lyr, pt[i, p * pp + j], 0, 0, 0))

    grid_spec = pltpu.PrefetchScalarGridSpec(
        num_scalar_prefetch=1,
        grid=(b, n_pages // pp),
        in_specs=([rows(), rows(), rows()] + [pages(j) for j in range(pp)]
                  + [pages(j) for j in range(pp)]
                  + [pl.BlockSpec((4, dq), lambda i, p, pt: (0, 0)),
                     pl.BlockSpec((1, dv), lambda i, p, pt: (0, 0))]),
        out_specs=pl.BlockSpec((1, 1, h_b * dv), lambda i, p, pt: (i, 0, 0)),
        scratch_shapes=[pltpu.VMEM((nsub, 1), F32), pltpu.VMEM((nsub, 1), F32),
                        pltpu.VMEM((nsub, dv), F32)],
    )
    return pl.pallas_call(
        functools.partial(_attn_decode_kernel, h_b=h_b, dq=dq, lam_init=lam_init, pp=pp),
        grid_spec=grid_spec,
        out_shape=jax.ShapeDtypeStruct((b, 1, h_b * dv), BF16),
        compiler_params=_params("parallel", "arbitrary"),
        name="attn_decode",
    )(page_table, qm, km, vm, *([cache_k] * pp), *([cache_v] * pp), lam_params, subln_g)


def _ep_plain(accs, e, o, carry, first, last):
    o[0][...] = accs[0].astype(o[0].dtype)


def _ep_bias(accs, e, o, carry, first, last):
    o[0][...] = accs[0] + e[0][...]


def _ep_silu(accs, e, o, carry, first, last):
    o[0][...] = _silu(accs[0]).astype(o[0].dtype)


def _ep_sigmoid(accs, e, o, carry, first, last):
    o[0][...] = _sigmoid(accs[0]).astype(o[0].dtype)


def _group_norm_store(y, gain, eps, out_refs):
    tn = y.shape[1]
    for g in range(tn // LANES):
        sl = slice(g * LANES, (g + 1) * LANES)
        blk = y[:, sl]
        ms = jnp.sum(blk * blk, axis=-1, keepdims=True) / LANES
        val = blk * lax.rsqrt(ms + eps) * gain[:, sl]
        for r in out_refs:
            r[:, sl] = val.astype(r.dtype)


def _ep_qknorm(accs, e, o, carry, first, last):
    _group_norm_store(accs[0], e[0][...], EPS, o)


def _ep_copy2(accs, e, o, carry, first, last):
    for r in o:
        r[...] = accs[0].astype(r.dtype)


def _sample_post_kernel(q_ref, k_ref, ga_ref, gb_ref, gq_ref, gk_ref, qo_ref, ko_ref, sa_ref, sb_ref):
    _group_norm_store(q_ref[...], gq_ref[...], EPS, [qo_ref])
    _group_norm_store(k_ref[...], gk_ref[...], EPS, [ko_ref])
    sa_ref[...] = _sigmoid(ga_ref[...])
    sb_ref[...] = _sigmoid(gb_ref[...])


def _ep_gdn_conv(accs, e, o, carry, first, last, *, width):
    raw = accs[0]
    taps = [e[j][...] for j in range(width)]
    nflag = e[width][...]
    gain = e[width + 1][...]
    y, head = _conv_taps(raw, carry[0], taps, first)

    def post(val, rows):
        val = _silu(val)
        tn = val.shape[1]
        for g in range(tn // LANES):
            sl = slice(g * LANES, (g + 1) * LANES)
            blk = val[:, sl]
            ss = jnp.sum(blk * blk, axis=-1, keepdims=True)
            scale = jnp.where(nflag[:, sl] > 0.5, lax.rsqrt(ss + EPS), 1.0) * gain[:, sl]
            o[0][rows, sl] = (blk * scale).astype(o[0].dtype)

    post(y, slice(None))
    post(head, slice(0, SUBLANES))
    if last:
        o[1][...] = raw[raw.shape[0] - SUBLANES:]


def _ep_branch(accs, e, o, carry, first, last):
    o[0][...] = (e[0][...].astype(F32) * accs[0] + e[1][...].astype(F32) * accs[1]).astype(o[0].dtype)


def _ep_residual(accs, e, o, carry, first, last):
    o[0][...] = e[0][...] + e[1][...] * accs[0]


def _ep_ffn_prompt(accs, e, o, carry, first, last, *, width):
    outs = []
    for d in range(2):
        taps = [e[d * (width + 1) + j][...] for j in range(width)]
        bias = e[d * (width + 1) + width][...]
        y, head = _conv_taps(accs[d], carry[d], taps, first)
        outs.append((y + bias, head + bias))
        if last:
            o[1 + d][...] = accs[d][accs[d].shape[0] - SUBLANES:]
    o[0][...] = (_silu(outs[0][0]) * outs[1][0]).astype(o[0].dtype)
    o[0][0:SUBLANES, :] = (_silu(outs[0][1]) * outs[1][1]).astype(o[0].dtype)


def _ep_ffn_sample(accs, e, o, carry, first, last, *, width):
    vals = []
    per = (width - 1) + width + 1
    for d in range(2):
        base = d * per
        y = accs[d] * e[base + (width - 1) + width - 1][...]
        for j in range(width - 1):
            y = y + e[base + j][...] * e[base + (width - 1) + j][...]
        vals.append(y + e[base + per - 1][...])
        o[1 + d][...] = accs[d]
    o[0][...] = (_silu(vals[0]) * vals[1]).astype(o[0].dtype)


def kernel(x_prompt, x_sample, c_prompt, c_sample, cache_k, cache_v, state_gdn, state_gdn_conv, state_ffn_conv, page_table, w_ada, b_ada, norm1_g, norm2_g, w_in, gdn_conv_w, gdn_a_log, gdn_dt_bias, gdn_norm_g, diff_q_norm_g, diff_k_norm_g, diff_lambda, diff_subln_g, w_branch_a, w_branch_b, w_o, w_up, ffn_conv_w, ffn_conv_b, w_down):
    depth = w_in.shape[0]
    assert depth == 1 and x_prompt.shape[0] == 1 and x_sample.shape[1] == 1
    lyr = 0
    lam_init = 0.8 - 0.6 * math.exp(-0.3 * lyr)
    d = x_prompt.shape[-1]
    l = x_prompt.shape[1]
    nb = x_sample.shape[0]
    h_a, dk, dva = state_gdn.shape[2:]
    h_b = cache_k.shape[3]
    dq = cache_k.shape[4] // 2
    dvb = cache_v.shape[4]
    assert dk == LANES and dva == LANES and dq == LANES and dvb == 2 * dq
    d_ff = w_down.shape[1]
    gw = gdn_conv_w.shape[1]
    fw = ffn_conv_w.shape[1]
    hd_a = h_a * dk
    hd_b = h_b * dvb

    w_t = jnp.swapaxes(w_in[lyr], 0, 1)
    c_za = 3 * hd_a
    c_ba = c_za + hd_a
    n_head = c_ba + LANES
    c_qb = c_ba + 2 * h_a
    c_kb, c_vb, c_ga, c_gb = c_qb + hd_b, c_qb + 2 * hd_b, c_qb + 3 * hd_b, c_qb + 3 * hd_b + d
    n_tail = 3 * hd_b + 2 * d
    assert c_qb + n_tail == w_t.shape[0] and 2 * h_a <= LANES
    t_qb, t_kb, t_vb, t_ga, t_gb = 0, hd_b, 2 * hd_b, 3 * hd_b, 3 * hd_b + d
    w_up0 = w_up[lyr]
    w_dn = w_down[lyr].astype(BF16)
    w_o_b = w_o[lyr]
    w_ba_b = w_branch_a[lyr]
    w_bb_b = w_branch_b[lyr]

    row = lambda v: v.reshape(1, -1).astype(F32)
    tile_rows = lambda v, reps: jnp.tile(row(v), (1, reps))

    c_all = jnp.concatenate([c_prompt, c_sample], axis=0).astype(F32)
    mp = -(-c_all.shape[0] // 16) * 16
    c_all = jnp.pad(c_all, ((0, mp - c_all.shape[0]), (0, 0)))
    (mod,) = _matmul("adaln", [c_all], [(w_ada[lyr], 0)], [(0, 0)], 6 * d, _ep_bias, [(F32, "tile")],
                     tm=mp, tn=_tile(6 * d, 512, LANES), extras=[(row(b_ada[lyr]), "row", 0)],
                     a_fn=_silu)
    mod_p = [mod[0:1, j * d:(j + 1) * d] for j in range(6)]
    mod_s = [mod[1:1 + nb, j * d:(j + 1) * d] for j in range(6)]

    gq = tile_rows(diff_q_norm_g[lyr], 2 * h_b) * (dq ** -0.5)
    gk = tile_rows(diff_k_norm_g[lyr], 2 * h_b)
    gq2 = gq * math.log2(math.e)
    score_bound = dq * jnp.max(jnp.abs(gq2)) * jnp.max(jnp.abs(gk))
    cw = gdn_conv_w[lyr].astype(F32)
    conv_nflag = jnp.concatenate([jnp.ones((1, 2 * hd_a), F32), jnp.zeros((1, hd_a), F32)], axis=1)
    conv_gain = jnp.concatenate([jnp.full((1, hd_a), dk ** -0.5, F32), jnp.ones((1, 2 * hd_a), F32)], axis=1)
    gnorm = row(gdn_norm_g[lyr])
    subln = row(diff_subln_g[lyr])
    lamp = diff_lambda[lyr].astype(F32)
    fcw = ffn_conv_w[lyr].astype(F32)
    fcb = row(ffn_conv_b[lyr])
    a_log = gdn_a_log[lyr].astype(F32)
    dt_bias = gdn_dt_bias[lyr].astype(F32)

    xp = x_prompt[0].astype(F32)
    tm = _tile(l, TILES["mm_m"], 16)
    tn = TILES["mm_n"]
    split = 4 if tm % 64 == 0 else 1
    h1 = _normmod(xp, row(norm1_g[lyr]), mod_p[1], mod_p[0])

    stationary = dict(tm=tm, order="nm", cast_once=True, row_split=split, wt=True)
    qkv_c, qkv_tail = _matmul(
        "p_qkv_conv", [h1], [(w_t, 0)], [(0, 0)], 3 * hd_a,
        functools.partial(_ep_gdn_conv, width=gw), [(BF16, "tile"), (F32, "tail")],
        tn=_tile(hd_a, tn, LANES), carry_rows=1, **stationary,
        extras=[(cw[j:j + 1], "row", 0) for j in range(gw)] + [(conv_nflag, "row", 0), (conv_gain, "row", 0)])
    (za_p,) = _matmul("p_za", [h1], [(w_t, c_za)], [(0, 0)], hd_a, _ep_silu, [(BF16, "tile")],
                      tn=_tile(hd_a, tn, LANES), **stationary)
    tn_b = _tile(hd_b, tn, LANES)
    (qb_p,) = _matmul("p_qb", [h1], [(w_t, c_qb)], [(0, 0)], hd_b, _ep_qknorm, [(BF16, "tile")],
                      tn=tn_b, extras=[(gq2, "row", 0)], **stationary)
    kb_p, kb_p16 = _matmul("p_kb", [h1], [(w_t, c_kb)], [(0, 0)], hd_b, _ep_qknorm,
                           [(F32, "tile"), (BF16, "tile")], tn=tn_b, extras=[(gk, "row", 0)],
                           **stationary)
    vb_p, vb_p16 = _matmul("p_vb", [h1], [(w_t, c_vb)], [(0, 0)], hd_b, _ep_copy2,
                           [(F32, "tile"), (BF16, "tile")], tn=tn_b, **stationary)
    tn_d = _tile(d, tn, LANES)
    (gates_p,) = _matmul("p_gates", [h1], [(w_t, c_ga)], [(0, 0)], 2 * d, _ep_sigmoid, [(BF16, "tile")],
                         tn=tn_d, **stationary)

    oa_p, s_p = _gdn_prompt(qkv_c, h1, w_t, c_ba, za_p, a_log, dt_bias, gnorm, h_a, dk)
    ob_p = _attn_prompt(qb_p, kb_p16, vb_p16, score_bound, lamp, subln, h_b, dq, lam_init)

    (mixed_p,) = _matmul("p_branch", [oa_p, ob_p], [(w_ba_b, 0), (w_bb_b, 0)], [(0, 0), (1, 1)], d,
                         _ep_branch, [(BF16, "tile")], tm=tm, tn=tn_d, row_split=split,
                         order="nm", cast_once=True,
                         extras=[(gates_p, "tile", 0), (gates_p, "tile", d)])
    (x2_p,) = _matmul("p_wo", [mixed_p], [(w_o_b, 0)], [(0, 0)], d, _ep_residual, [(F32, "tile")],
                      tm=tm, tn=tn_d, row_split=split, order="nm", cast_once=True,
                      extras=[(xp, "tile", 0), (mod_p[2], "row", 0)])
    h2 = _normmod(x2_p, row(norm2_g[lyr]), mod_p[4], mod_p[3])
    tn_u = _tile(d_ff, TILES["up_n"], LANES)
    assert d_ff % tn_u == 0
    ffn_rows = ([(fcw[j:j + 1], "row", 0) for j in range(fw)] + [(fcb, "row", 0)]
                + [(fcw[j:j + 1], "row", d_ff) for j in range(fw)] + [(fcb, "row", d_ff)])
    act_f, tail_g, tail_v = _matmul(
        "p_ffn_up", [h2], [(w_up0, 0), (w_up0, d_ff)], [(0, 0), (0, 1)], d_ff,
        functools.partial(_ep_ffn_prompt, width=fw),
        [(BF16, "tile"), (F32, "tail"), (F32, "tail")],
        tm=tm, tn=tn_u, order="nm", carry_rows=2, cast_once=True, row_split=split, extras=ffn_rows)
    (y_p,) = _matmul("p_down", [act_f], [(w_dn, 0)], [(0, 0)], d, _ep_residual, [(F32, "tile")],
                     tm=tm, tn=tn_d, tk=_tile(d_ff, TILES["down_k"], LANES),
                     extras=[(x2_p, "tile", 0), (mod_p[5], "row", 0)])

    xs = x_sample[:, 0, :].astype(F32)
    tns = TILES["small_n"]
    h1s = _normmod(xs, row(norm1_g[lyr]), mod_s[1], mod_s[0])
    (zh_s,) = _matmul("s_in_head", [h1s], [(w_t, 0)], [(0, 0)], n_head, _ep_plain, [(F32, "tile")],
                      tm=nb, tn=_tile(n_head, tns, LANES), wt=True)
    (zt_s,) = _matmul("s_in_tail", [h1s], [(w_t, c_qb)], [(0, 0)], n_tail, _ep_plain,
                      [(F32, "tile")], tm=nb, tn=_tile(n_tail, tns, LANES), wt=True)
    oa_s, s_s, gbuf_s = _gdn_sample(zh_s[:, None, :], state_gdn_conv[lyr].astype(F32),
                                    state_gdn[lyr].astype(F32), cw, a_log, dt_bias, gnorm, h_a, dk)
    vb_s = zt_s[:, t_vb:t_vb + hd_b]
    qn_s, kn_s, sga_s, sgb_s = pl.pallas_call(
        _sample_post_kernel,
        out_shape=[jax.ShapeDtypeStruct((nb, hd_b), F32), jax.ShapeDtypeStruct((nb, hd_b), F32),
                   jax.ShapeDtypeStruct((nb, d), F32), jax.ShapeDtypeStruct((nb, d), F32)],
        name="sample_post",
    )(zt_s[:, t_qb:t_qb + hd_b], zt_s[:, t_kb:t_kb + hd_b], zt_s[:, t_ga:t_ga + d],
      zt_s[:, t_gb:t_gb + d], gq, gk)
    ob_s = _attn_decode(qn_s, kn_s, vb_s, cache_k, cache_v, page_table.astype(jnp.int32), lamp, subln,
                        h_b, dq, lam_init, lyr)

    tn_ds = _tile(d, tns, LANES)
    (mixed_s,) = _matmul("s_branch", [oa_s[:, 0, :], ob_s[:, 0, :]], [(w_ba_b, 0), (w_bb_b, 0)],
                         [(0, 0), (1, 1)], d, _ep_branch, [(BF16, "tile")], tm=nb, tn=tn_ds,
                         extras=[(sga_s, "tile", 0), (sgb_s, "tile", 0)])
    (x2_s,) = _matmul("s_wo", [mixed_s], [(w_o_b, 0)], [(0, 0)], d, _ep_residual, [(F32, "tile")],
                      tm=nb, tn=tn_ds, extras=[(xs, "tile", 0), (mod_s[2], "tile", 0)])
    h2s = _normmod(x2_s, row(norm2_g[lyr]), mod_s[4], mod_s[3])
    fbuf = state_ffn_conv[lyr].astype(F32)
    ffn_ex_s = []
    for off in (0, d_ff):
        ffn_ex_s += [(fbuf[:, j, :], "tile", off) for j in range(fw - 1)]
        ffn_ex_s += [(fcw[j:j + 1], "row", off) for j in range(fw)] + [(fcb, "row", off)]
    tn_us = _tile(d_ff, tns, LANES)
    act_s, up_g, up_v = _matmul(
        "s_ffn_up", [h2s], [(w_up0, 0), (w_up0, d_ff)], [(0, 0), (0, 1)], d_ff,
        functools.partial(_ep_ffn_sample, width=fw),
        [(BF16, "tile"), (F32, "tile"), (F32, "tile")],
        tm=nb, tn=tn_us, extras=ffn_ex_s)
    (y_s,) = _matmul("s_down", [act_s], [(w_dn, 0)], [(0, 0)], d, _ep_residual, [(F32, "tile")],
                     tm=nb, tn=tn_ds, tk=_tile(d_ff, TILES["down_k"], LANES),
                     extras=[(x2_s, "tile", 0), (mod_s[5], "tile", 0)])

    y_prompt = y_p[None].astype(x_prompt.dtype)
    y_sample = y_s[:, None, :].astype(x_sample.dtype)
    k_prompt = kb_p.reshape(1, 1, l, h_b, 2 * dq)
    v_prompt = vb_p.reshape(1, 1, l, h_b, dvb)
    gdn_state_prompt = s_p[None, None]
    gdn_conv_prompt = qkv_tail[SUBLANES - (gw - 1):][None, None]
    ffn_conv_prompt = jnp.concatenate([tail_g[SUBLANES - (fw - 1):], tail_v[SUBLANES - (fw - 1):]],
                                      axis=1)[None, None]
    k_sample = kn_s.reshape(1, nb, 1, h_b, 2 * dq)
    v_sample = vb_s.reshape(1, nb, 1, h_b, dvb)
    gdn_state_sample = s_s[None]
    gdn_conv_sample = gbuf_s[None]
    up_new = jnp.concatenate([up_g, up_v], axis=1)
    ffn_conv_sample = jnp.concatenate([fbuf[:, 1:, :], up_new[:, None, :]], axis=1)[None]
    return (y_prompt, y_sample, k_prompt, v_prompt, gdn_state_prompt, gdn_conv_prompt,
            ffn_conv_prompt, k_sample, v_sample, gdn_state_sample, gdn_conv_sample, ffn_conv_sample)
```

```python
import functools
import math

import jax
import jax.numpy as jnp
from jax import lax
from jax.experimental import pallas as pl
from jax.experimental.pallas import tpu as pltpu

F32 = jnp.float32
BF16 = jnp.bfloat16

LANES = 128
SUBLANES = 8
VMEM_LIMIT_BYTES = 56 * 1024 * 1024

EPS = 1e-6
SUBLN_EPS = 1e-5
GDN_CHUNK = 128
MAX_UNSHIFTED_SCORE = 40.0
NEG_BIG = -1e30

TILES = dict(
    norm_rows=256,
    mm_m=1024, mm_n=512,
    up_n=256,
    down_k=5504,
    attn=1024,
    gdn_heads=8,
    decode_pages=8,
    small_n=1024,
)

NT_DIMS = (((1,), (1,)), ((), ()))


def _params(*sem):
    return pltpu.CompilerParams(dimension_semantics=sem, vmem_limit_bytes=VMEM_LIMIT_BYTES)


def _tile(dim, pref, align):
    if dim <= pref:
        return dim
    t = (pref // align) * align
    while t >= align:
        if dim % t == 0:
            return t
        t -= align
    return dim


def _sigmoid(x):
    return 1.0 / (1.0 + jnp.exp(-x))


def _silu(x):
    return x * _sigmoid(x)


def _softplus(x):
    return jnp.maximum(x, 0.0) + jnp.log1p(jnp.exp(-jnp.abs(x)))


def _normmod_kernel(x_ref, g_ref, sc_ref, sh_ref, o_ref):
    x = x_ref[...]
    ms = jnp.mean(x * x, axis=-1, keepdims=True)
    h = x * lax.rsqrt(ms + EPS) * g_ref[...]
    o_ref[...] = (h * (1.0 + sc_ref[...]) + sh_ref[...]).astype(o_ref.dtype)


def _normmod(x, g, sc, sh):
    m, d = x.shape
    tm = _tile(m, TILES["norm_rows"], SUBLANES)
    per_row = sc.shape[0] != 1
    mod_spec = (pl.BlockSpec((tm, d), lambda i: (i, 0)) if per_row
                else pl.BlockSpec((1, d), lambda i: (0, 0)))
    return pl.pallas_call(
        _normmod_kernel,
        grid=(m // tm,),
        in_specs=[pl.BlockSpec((tm, d), lambda i: (i, 0)),
                  pl.BlockSpec((1, d), lambda i: (0, 0)), mod_spec, mod_spec],
        out_specs=pl.BlockSpec((tm, d), lambda i: (i, 0)),
        out_shape=jax.ShapeDtypeStruct((m, d), BF16),
        compiler_params=_params("parallel"),
        name="normmod",
    )(x, g, sc, sh)


def _matmul(name, as_, ws, dots, n, epilogue, outs, *, tm, tn, extras=(), order="mn",
            tk=None, a_fn=None, carry_rows=0, cast_once=False, row_split=1, wt=False, side=None):
    m = as_[0].shape[0]
    kdim = as_[0].shape[1]
    tk = kdim if tk is None else tk
    nk = kdim // tk
    assert kdim % tk == 0 and m % tm == 0 and n % tn == 0 and tm % row_split == 0
    if nk > 1:
        assert all(a.shape[1] == kdim for a in as_) and row_split == 1 and not cast_once
    assert not cast_once or order == "nm"
    ni, nj = m // tm, n // tn
    na, nw, ne, no, nd = len(as_), len(ws), len(extras), len(outs), len(dots)
    rows = tm // row_split

    if order == "mn":
        grid = (ni, nj, nk)
        ij = lambda g0, g1: (g0, g1)
    else:
        grid = (nj, ni, nk)
        ij = lambda g0, g1: (g1, g0)

    def a_map(g0, g1, k):
        return (ij(g0, g1)[0], k)

    def w_map(off):
        return lambda g0, g1, k: (k, ij(g0, g1)[1] + off // tn)

    def row_map(off):
        return lambda g0, g1, k: (0, ij(g0, g1)[1] + off // tn)

    def tile_map(off):
        return lambda g0, g1, k: (ij(g0, g1)[0], ij(g0, g1)[1] + off // tn)

    in_specs = []
    for a in as_:
        in_specs.append(pl.BlockSpec((tm, tk if nk > 1 else a.shape[1]), a_map))
    for w, off in ws:
        if wt:
            assert nk == 1 and off % SUBLANES == 0
            in_specs.append(pl.BlockSpec(
                (pl.Element(tn), pl.Element(w.shape[1])),
                functools.partial(
                    lambda g0, g1, k, off: (pl.multiple_of(off + ij(g0, g1)[1] * tn, SUBLANES), 0),
                    off=off)))
        else:
            assert off % tn == 0
            in_specs.append(pl.BlockSpec((tk if nk > 1 else w.shape[0], tn), w_map(off)))
    for arr, kind, off in extras:
        assert off % tn == 0
        if kind == "row":
            in_specs.append(pl.BlockSpec((1, tn), row_map(off)))
        else:
            in_specs.append(pl.BlockSpec((tm, tn), tile_map(off)))
    out_specs, out_shapes = [], []
    for dtype, kind in outs:
        if kind == "tile":
            out_specs.append(pl.BlockSpec((tm, tn), tile_map(0)))
            out_shapes.append(jax.ShapeDtypeStruct((m, n), dtype))
        else:
            out_specs.append(pl.BlockSpec((SUBLANES, tn), row_map(0)))
            out_shapes.append(jax.ShapeDtypeStruct((SUBLANES, n), dtype))
    nside = 0
    if side is not None:
        assert cast_once and side.shape[1] == kdim
        nside = nw
        in_specs.append(pl.BlockSpec(side.shape, lambda g0, g1, k: (0, 0)))
        for _ in range(nw):
            out_specs.append(pl.BlockSpec((side.shape[0], tn), row_map(0)))
            out_shapes.append(jax.ShapeDtypeStruct((side.shape[0], n), F32))
    scratch = []
    if nk > 1:
        scratch += [pltpu.VMEM((tm, tn), F32) for _ in range(nd)]
    if cast_once:
        scratch += [pltpu.VMEM((tn, w.shape[1]) if wt else (w.shape[0], tn), BF16) for w, _ in ws]
    scratch += [pltpu.VMEM((SUBLANES + rows, tn), F32) for _ in range(carry_rows)]

    def body(*refs):
        a_refs = refs[:na]
        w_refs = refs[na:na + nw]
        e_refs = refs[na + nw:na + nw + ne]
        nin = na + nw + ne + (1 if nside else 0)
        side_ref = refs[nin - 1] if nside else None
        o_refs = refs[nin:nin + no]
        side_out = refs[nin + no:nin + no + nside]
        s_refs = list(refs[nin + no + nside:])
        acc_refs = [s_refs.pop(0) for _ in range(nd)] if nk > 1 else []
        wb_refs = [s_refs.pop(0) for _ in range(nw)] if cast_once else list(w_refs)
        carry_refs = s_refs
        i, _ = ij(pl.program_id(0), pl.program_id(1))
        k = pl.program_id(2)

        def partial(d, r):
            ai, wi = dots[d]
            a = a_refs[ai][r * rows:(r + 1) * rows, :] if row_split > 1 else a_refs[ai][...]
            if a_fn is not None:
                a = a_fn(a)
            w = wb_refs[wi][...].astype(BF16)
            if wt:
                return lax.dot_general(a.astype(BF16), w, NT_DIMS, preferred_element_type=F32)
            return jnp.dot(a.astype(BF16), w, preferred_element_type=F32)

        if cast_once:
            @pl.when(i == 0)
            def _():
                for wi in range(nw):
                    wb_refs[wi][...] = w_refs[wi][...].astype(BF16)
                for wi in range(nside):
                    wb = wb_refs[wi][...]
                    side_out[wi][...] = (
                        lax.dot_general(side_ref[...], wb, NT_DIMS, preferred_element_type=F32)
                        if wt else jnp.dot(side_ref[...], wb, preferred_element_type=F32))

        if nk == 1:
            for r in range(row_split):
                if row_split > 1:
                    piece = pl.ds(r * rows, rows)
                    ev = [e.at[piece] if extras[x][1] == "tile" else e for x, e in enumerate(e_refs)]
                    ov = [o.at[piece] if outs[x][1] == "tile" else o for x, o in enumerate(o_refs)]
                else:
                    ev, ov = e_refs, o_refs
                epilogue([partial(d, r) for d in range(nd)], ev, ov, carry_refs,
                         (i == 0) if r == 0 else False, r == row_split - 1)
        else:
            @pl.when(k == 0)
            def _():
                for d in range(nd):
                    acc_refs[d][...] = jnp.zeros((tm, tn), F32)

            for d in range(nd):
                acc_refs[d][...] += partial(d, 0)

            @pl.when(k == nk - 1)
            def _():
                epilogue([acc_refs[d][...] for d in range(nd)], e_refs, o_refs,
                         carry_refs, i == 0, True)

    return pl.pallas_call(
        body,
        grid=grid,
        in_specs=in_specs,
        out_specs=out_specs,
        out_shape=out_shapes,
        scratch_shapes=scratch,
        compiler_params=_params("arbitrary", "arbitrary", "arbitrary"),
        name=name,
    )(*as_, *[w for w, _ in ws], *[e for e, _, _ in extras], *([side] if nside else []))


def _conv_taps(raw, win_ref, taps, first):
    width = len(taps)
    rows = raw.shape[0]
    if first is not False:
        @pl.when(first)
        def _():
            win_ref[0:SUBLANES, :] = jnp.zeros((SUBLANES, raw.shape[1]), F32)

    win_ref[SUBLANES:SUBLANES + rows, :] = raw
    y = win_ref[SUBLANES:SUBLANES + rows, :] * taps[width - 1]
    for s in range(1, width):
        y = y + win_ref[SUBLANES - s:SUBLANES - s + rows, :] * taps[width - 1 - s]
    win_ref[0:SUBLANES, :] = win_ref[rows:rows + SUBLANES, :]
    return y


def _gdn_gates(ba, alog_row, dtb_row):
    return _sigmoid(ba), -jnp.exp(alog_row) * _softplus(ba + dtb_row)


def _gdn_prep_kernel(h_ref, hs_ref, wba_ref, alog_ref, dtb_ref, beta_ref, gc_ref, gct_ref, bas_ref,
                     *, heads_pad, h_a):
    c = GDN_CHUNK
    wba = wba_ref[...].astype(BF16)

    @pl.when(pl.program_id(0) == 0)
    def _():
        bas_ref[...] = lax.dot_general(hs_ref[...], wba, NT_DIMS, preferred_element_type=F32)

    ba = lax.dot_general(h_ref[...], wba, NT_DIMS, preferred_element_type=F32)
    beta, g = _gdn_gates(ba, alog_ref[...], dtb_ref[...])
    beta_ref[...] = beta
    row = lax.broadcasted_iota(jnp.int32, (c, LANES), 0)
    s = 1
    while s < c:
        g = g + jnp.where(row >= s, pltpu.roll(g, s, axis=0), 0.0)
        s *= 2
    gc_ref[...] = g
    gct_ref[...] = g.T[h_a:h_a + heads_pad, :]


def _gdn_intra_kernel(q_ref, k_ref, v_ref, beta_ref, gc_ref, gct_ref,
                      uv_ref, w_ref, aqk_ref, qdec_ref, kdect_ref, *, hb, h_a):
    c = GDN_CHUNK
    grp = pl.program_id(1)
    lane = lax.broadcasted_iota(jnp.int32, (c, LANES), 1)
    row = lax.broadcasted_iota(jnp.int32, (c, c), 0)
    col = lax.broadcasted_iota(jnp.int32, (c, c), 1)
    eye = jnp.where(row == col, 1.0, 0.0)
    levels = range(1, int(math.log2(c)))
    below = [((row >> (lvl + 1)) == (col >> (lvl + 1))) & ((row >> lvl) != (col >> lvl))
             for lvl in levels]
    pair = (row >> 1) == (col >> 1)
    beta_all = beta_ref[...]
    gc_all = gc_ref[...]
    heads = range(hb)
    sls = [slice(j * LANES, (j + 1) * LANES) for j in heads]
    beta_c, gcc, kb, mm, t = [], [], [], [], []
    for j in heads:
        h = grp * hb + j
        beta_c.append(jnp.sum(jnp.where(lane == h, beta_all, 0.0), axis=1, keepdims=True))
        gcc.append(jnp.sum(jnp.where(lane == h + h_a, gc_all, 0.0), axis=1, keepdims=True))
        gcr = gct_ref[pl.ds(h, 1), :]
        dec_incl = jnp.exp(jnp.where(row >= col, gcc[j] - gcr, NEG_BIG))
        k = k_ref[:, sls[j]]
        kb.append(k.astype(F32) * beta_c[j])
        mm.append(lax.dot_general(kb[j].astype(BF16), k, NT_DIMS, preferred_element_type=F32)
                  * jnp.where(row > col, dec_incl, 0.0))
        aqk = lax.dot_general(q_ref[:, sls[j]], k, NT_DIMS, preferred_element_type=F32) * dec_incl
        aqk_ref[:, sls[j]] = aqk.astype(BF16)
        t.append(eye - jnp.where(pair, mm[j], 0.0))
    for msk in below:
        tb = [t[j].astype(BF16) for j in heads]
        bt = [jnp.dot(jnp.where(msk, mm[j], 0.0).astype(BF16), tb[j], preferred_element_type=F32)
              for j in heads]
        t = [t[j] - jnp.dot(tb[j], bt[j].astype(BF16), preferred_element_type=F32) for j in heads]
    for j in heads:
        tb = t[j].astype(BF16)
        egc = jnp.exp(gcc[j])
        gl = gcc[j][c - 1:c, :]
        vb = v_ref[:, sls[j]].astype(F32) * beta_c[j]
        uv_ref[:, sls[j]] = jnp.dot(tb, vb.astype(BF16), preferred_element_type=F32)
        w_ref[:, sls[j]] = jnp.dot(tb, (kb[j] * egc).astype(BF16),
                                   preferred_element_type=F32).astype(BF16)
        qdec_ref[:, sls[j]] = (q_ref[:, sls[j]].astype(F32) * egc).astype(BF16)
        kdect_ref[sls[j], :] = (k_ref[:, sls[j]].astype(F32) * jnp.exp(gl - gcc[j])).T.astype(BF16)


def _gdn_rec_kernel(uv_ref, w_ref, aqk_ref, qdec_ref, kdect_ref, gct_ref, za_ref, gn_ref,
                    o_ref, sout_ref, s_ref, *, hb):
    c = GDN_CHUNK
    n = pl.program_id(1)
    g0 = pl.program_id(0)

    @pl.when(n == 0)
    def _():
        s_ref[...] = jnp.zeros(s_ref.shape, F32)

    heads = range(hb)
    sls = [slice(j * LANES, (j + 1) * LANES) for j in heads]
    s = [s_ref[j] for j in heads]
    sb = [s[j].astype(BF16) for j in heads]
    ub = [(uv_ref[:, sls[j]] - jnp.dot(w_ref[:, sls[j]], sb[j], preferred_element_type=F32)
           ).astype(BF16) for j in heads]
    oq = [jnp.dot(qdec_ref[:, sls[j]], sb[j], preferred_element_type=F32) for j in heads]
    for j in heads:
        gl = jnp.exp(gct_ref[pl.ds(g0 * hb + j, 1), c - 1:c])
        s_ref[j] = gl * s[j] + jnp.dot(kdect_ref[sls[j], :], ub[j], preferred_element_type=F32)
    for j in heads:
        o = oq[j] + jnp.dot(aqk_ref[:, sls[j]], ub[j], preferred_element_type=F32)
        ms = jnp.mean(o * o, axis=-1, keepdims=True)
        o_ref[:, sls[j]] = (o * lax.rsqrt(ms + EPS) * gn_ref[...]
                            * za_ref[:, sls[j]].astype(F32)).astype(o_ref.dtype)

    @pl.when(n == pl.num_programs(1) - 1)
    def _():
        sout_ref[...] = s_ref[...]


def _gate_rows(a_log, dt_bias, h_a):
    alog_row = jnp.zeros((1, LANES), F32).at[0, h_a:2 * h_a].set(a_log)
    dtb_row = jnp.zeros((1, LANES), F32).at[0, h_a:2 * h_a].set(dt_bias)
    return alog_row, dtb_row


def _gdn_prompt(qkv, h1, h1s, w_t, c_ba, za_silu, a_log, dt_bias, gnorm, h_a, dk):
    l = qkv.shape[0]
    kdim = h1.shape[1]
    c = GDN_CHUNK
    assert l % c == 0 and dk == LANES and 2 * h_a <= LANES and c_ba % LANES == 0
    nchunk = l // c
    hp = max(SUBLANES, -(-h_a // SUBLANES) * SUBLANES)
    alog_row, dtb_row = _gate_rows(a_log, dt_bias, h_a)

    ms = h1s.shape[0]
    beta, gc, gct, ba_s = pl.pallas_call(
        functools.partial(_gdn_prep_kernel, heads_pad=hp, h_a=h_a),
        grid=(nchunk,),
        in_specs=[pl.BlockSpec((c, kdim), lambda n: (n, 0)),
                  pl.BlockSpec((ms, kdim), lambda n: (0, 0)),
                  pl.BlockSpec((LANES, kdim), lambda n: (c_ba // LANES, 0)),
                  pl.BlockSpec((1, LANES), lambda n: (0, 0)),
                  pl.BlockSpec((1, LANES), lambda n: (0, 0))],
        out_specs=[pl.BlockSpec((c, LANES), lambda n: (n, 0)),
                   pl.BlockSpec((c, LANES), lambda n: (n, 0)),
                   pl.BlockSpec((hp, c), lambda n: (0, n)),
                   pl.BlockSpec((ms, LANES), lambda n: (0, 0))],
        out_shape=[jax.ShapeDtypeStruct((l, LANES), F32),
                   jax.ShapeDtypeStruct((l, LANES), F32),
                   jax.ShapeDtypeStruct((hp, l), F32),
                   jax.ShapeDtypeStruct((ms, LANES), F32)],
        compiler_params=_params("arbitrary"),
        name="gdn_prep",
    )(h1, h1s, w_t, alog_row, dtb_row)

    hd = h_a * dk
    hb = _tile(h_a, TILES["gdn_heads"], 1)
    ng = h_a // hb
    blk = lambda off: pl.BlockSpec((c, hb * LANES), lambda n, g: (n, g + off))
    uv, w, aqk, qdec, kdect = pl.pallas_call(
        functools.partial(_gdn_intra_kernel, hb=hb, h_a=h_a),
        grid=(nchunk, ng),
        in_specs=[blk(0), blk(ng), blk(2 * ng),
                  pl.BlockSpec((c, LANES), lambda n, g: (n, 0)),
                  pl.BlockSpec((c, LANES), lambda n, g: (n, 0)),
                  pl.BlockSpec((hp, c), lambda n, g: (0, n))],
        out_specs=[blk(0), blk(0), blk(0), blk(0),
                   pl.BlockSpec((hb * LANES, c), lambda n, g: (g, n))],
        out_shape=[jax.ShapeDtypeStruct((l, hd), F32),
                   jax.ShapeDtypeStruct((l, hd), BF16),
                   jax.ShapeDtypeStruct((l, hd), BF16),
                   jax.ShapeDtypeStruct((l, hd), BF16),
                   jax.ShapeDtypeStruct((hd, l), BF16)],
        compiler_params=_params("parallel", "arbitrary"),
        name="gdn_intra",
    )(qkv, qkv, qkv, beta, gc, gct)

    hr = hb
    wide = lambda: pl.BlockSpec((c, hr * LANES), lambda g, n: (n, g))
    oa, s_fin = pl.pallas_call(
        functools.partial(_gdn_rec_kernel, hb=hr),
        grid=(h_a // hr, nchunk),
        in_specs=[wide(), wide(), wide(), wide(),
                  pl.BlockSpec((hr * LANES, c), lambda g, n: (g, n)),
                  pl.BlockSpec((hp, c), lambda g, n: (0, n)),
                  wide(),
                  pl.BlockSpec((1, LANES), lambda g, n: (0, 0))],
        out_specs=[wide(), pl.BlockSpec((hr, dk, LANES), lambda g, n: (g, 0, 0))],
        out_shape=[jax.ShapeDtypeStruct((l, hd), BF16),
                   jax.ShapeDtypeStruct((h_a, dk, LANES), F32)],
        scratch_shapes=[pltpu.VMEM((hr, dk, LANES), F32)],
        compiler_params=_params("parallel", "arbitrary"),
        name="gdn_rec",
    )(uv, w, aqk, qdec, kdect, gct, za_silu, gnorm)
    return oa, s_fin, ba_s


def _gdn_sample_kernel(zq_ref, za_ref, ba_ref, buf_ref, s_ref, cw_ref, alog_ref, dtb_ref, gn_ref,
                       o_ref, snew_ref, bufnew_ref, *, h_a, dk, width):
    hd = h_a * dk
    raw = zq_ref[0]
    buf = buf_ref[0]
    y = raw * cw_ref[width - 1:width, :]
    for j in range(width - 1):
        y = y + buf[j:j + 1, :] * cw_ref[j:j + 1, :]
    y = _silu(y)
    bufnew_ref[0, 0:width - 2, :] = buf[1:width - 1, :]
    bufnew_ref[0, width - 2:width - 1, :] = raw
    za = za_ref[0]
    beta_row, g_row = _gdn_gates(ba_ref[0], alog_ref[...], dtb_ref[...])
    lane1 = lax.broadcasted_iota(jnp.int32, (1, LANES), 1)
    row = lax.broadcasted_iota(jnp.int32, (dk, LANES), 0)
    col = lax.broadcasted_iota(jnp.int32, (dk, LANES), 1)
    eye = row == col
    for h in range(h_a):
        q = y[:, h * dk:(h + 1) * dk]
        k = y[:, hd + h * dk:hd + (h + 1) * dk]
        v = y[:, 2 * hd + h * dk:2 * hd + (h + 1) * dk]
        q = q * lax.rsqrt(jnp.sum(q * q, axis=-1, keepdims=True) + EPS) * (dk ** -0.5)
        k = k * lax.rsqrt(jnp.sum(k * k, axis=-1, keepdims=True) + EPS)
        beta = jnp.sum(jnp.where(lane1 == h, beta_row, 0.0), axis=1, keepdims=True)
        a = jnp.exp(jnp.sum(jnp.where(lane1 == h + h_a, g_row, 0.0), axis=1, keepdims=True))
        k_col = jnp.sum(jnp.where(eye, k, 0.0), axis=1, keepdims=True)
        q_col = jnp.sum(jnp.where(eye, q, 0.0), axis=1, keepdims=True)
        s = a * s_ref[0, h]
        u = beta * (v - jnp.sum(s * k_col, axis=0, keepdims=True))
        s_new = s + k_col * u
        snew_ref[0, h] = s_new
        o = jnp.sum(s_new * q_col, axis=0, keepdims=True)
        ms = jnp.mean(o * o, axis=-1, keepdims=True)
        o_ref[0, :, h * dk:(h + 1) * dk] = (
            o * lax.rsqrt(ms + EPS) * gn_ref[...] * _silu(za[:, h * dk:(h + 1) * dk])
        ).astype(o_ref.dtype)


def _gdn_sample(zq, za, ba, buf, s0, conv_w, a_log, dt_bias, gnorm, h_a, dk):
    b = zq.shape[0]
    width = conv_w.shape[0]
    hd = h_a * dk
    alog_row, dtb_row = _gate_rows(a_log, dt_bias, h_a)
    per_seq = lambda n: pl.BlockSpec((1, 1, n), lambda i: (i, 0, 0))
    return pl.pallas_call(
        functools.partial(_gdn_sample_kernel, h_a=h_a, dk=dk, width=width),
        grid=(b,),
        in_specs=[per_seq(3 * hd), per_seq(hd), per_seq(LANES),
                  pl.BlockSpec((1, width - 1, 3 * hd), lambda i: (i, 0, 0)),
                  pl.BlockSpec((1, h_a, dk, LANES), lambda i: (i, 0, 0, 0)),
                  pl.BlockSpec((width, 3 * hd), lambda i: (0, 0)),
                  pl.BlockSpec((1, LANES), lambda i: (0, 0)),
                  pl.BlockSpec((1, LANES), lambda i: (0, 0)),
                  pl.BlockSpec((1, LANES), lambda i: (0, 0))],
        out_specs=[pl.BlockSpec((1, 1, hd), lambda i: (i, 0, 0)),
                   pl.BlockSpec((1, h_a, dk, LANES), lambda i: (i, 0, 0, 0)),
                   pl.BlockSpec((1, width - 1, 3 * hd), lambda i: (i, 0, 0))],
        out_shape=[jax.ShapeDtypeStruct((b, 1, hd), BF16),
                   jax.ShapeDtypeStruct((b, h_a, dk, LANES), F32),
                   jax.ShapeDtypeStruct((b, width - 1, 3 * hd), F32)],
        compiler_params=_params("parallel"),
        name="gdn_sample",
    )(zq[:, None, :], za[:, None, :], ba[:, None, :], buf, s0, conv_w, alog_row, dtb_row, gnorm)


def _diff_lambda(lamp_ref, lam_init):
    lp = lamp_ref[...]
    e1 = jnp.exp(jnp.sum(lp[0:1] * lp[1:2], axis=1, keepdims=True))
    e2 = jnp.exp(jnp.sum(lp[2:3] * lp[3:4], axis=1, keepdims=True))
    return e1 - e2 + lam_init


def _attn_prompt_kernel(qi_ref, kj_ref, q_ref, k_ref, v_ref, lamp_ref, g_ref, o_ref,
                        m_ref, l_ref, acc_ref, *, dq, lam_init):
    p = pl.program_id(1)
    qi = qi_ref[p]
    kj = kj_ref[p]

    @pl.when(kj == 0)
    def _():
        m_ref[...] = jnp.full(m_ref.shape, NEG_BIG, F32)
        l_ref[...] = jnp.zeros(l_ref.shape, F32)
        acc_ref[...] = jnp.zeros(acc_ref.shape, F32)

    def step(masked):
        v = v_ref[...]
        for c in range(2):
            s = lax.dot_general(q_ref[:, c * dq:(c + 1) * dq], k_ref[:, c * dq:(c + 1) * dq],
                                NT_DIMS, preferred_element_type=F32)
            if masked:
                row = lax.broadcasted_iota(jnp.int32, s.shape, 0)
                col = lax.broadcasted_iota(jnp.int32, s.shape, 1)
                s = jnp.where(row >= col, s, NEG_BIG)
            m_prev = m_ref[c]
            m_new = jnp.maximum(m_prev, jnp.max(s, axis=1, keepdims=True))
            alpha = jnp.exp2(m_prev - m_new)
            pm = jnp.exp2(s - m_new)
            l_ref[c] = alpha * l_ref[c] + jnp.sum(pm, axis=1, keepdims=True)
            acc_ref[c] = alpha * acc_ref[c] + jnp.dot(pm.astype(BF16), v,
                                                      preferred_element_type=F32)
            m_ref[c] = m_new

    @pl.when(kj < qi)
    def _():
        step(False)

    @pl.when(kj == qi)
    def _():
        step(True)
        _attn_finish(acc_ref, l_ref[0], l_ref[1], lamp_ref, g_ref, o_ref, lam_init)


def _attn_finish(acc_ref, l0, l1, lamp_ref, g_ref, o_ref, lam_init):
    lam = _diff_lambda(lamp_ref, lam_init)
    o = acc_ref[0] / l0 - lam * (acc_ref[1] / l1)
    ms = jnp.mean(o * o, axis=-1, keepdims=True)
    o_ref[...] = (o * lax.rsqrt(ms + SUBLN_EPS) * g_ref[...] * (1.0 - lam_init)).astype(o_ref.dtype)


def _attn_prompt_bounded_kernel(qi_ref, kj_ref, q_ref, k_ref, v_ref, lamp_ref, g_ref, o_ref,
                                l_ref, acc_ref, *, dq, lam_init):
    p = pl.program_id(1)
    qi = qi_ref[p]
    kj = kj_ref[p]

    @pl.when(kj == 0)
    def _():
        l_ref[...] = jnp.zeros(l_ref.shape, F32)
        acc_ref[...] = jnp.zeros(acc_ref.shape, F32)

    def step(masked):
        v = v_ref[...]
        for c in range(2):
            s = lax.dot_general(q_ref[:, c * dq:(c + 1) * dq], k_ref[:, c * dq:(c + 1) * dq],
                                NT_DIMS, preferred_element_type=F32)
            pm = jnp.exp2(s)
            if masked:
                row = lax.broadcasted_iota(jnp.int32, s.shape, 0)
                col = lax.broadcasted_iota(jnp.int32, s.shape, 1)
                pm = jnp.where(row >= col, pm, 0.0)
            part = pm[:, 0:LANES]
            for g in range(1, pm.shape[1] // LANES):
                part = part + pm[:, g * LANES:(g + 1) * LANES]
            l_ref[c] += part
            acc_ref[c] += jnp.dot(pm.astype(BF16), v, preferred_element_type=F32)

    @pl.when(kj < qi)
    def _():
        step(False)

    @pl.when(kj == qi)
    def _():
        step(True)
        _attn_finish(acc_ref, jnp.sum(l_ref[0], axis=1, keepdims=True),
                     jnp.sum(l_ref[1], axis=1, keepdims=True), lamp_ref, g_ref, o_ref, lam_init)


def _attn_prompt(q, k, v, score_bound, lam_params, subln_g, h_b, dq, lam_init):
    l = q.shape[0]
    dv = 2 * dq
    t = _tile(l, TILES["attn"], LANES)
    nb = l // t
    pairs = [(i, j) for i in range(nb) for j in range(i + 1)]
    qi_tab = jnp.asarray([a for a, _ in pairs], jnp.int32)
    kj_tab = jnp.asarray([b for _, b in pairs], jnp.int32)

    def call(body, scratch, name):
        grid_spec = pltpu.PrefetchScalarGridSpec(
            num_scalar_prefetch=2,
            grid=(h_b, len(pairs)),
            in_specs=[pl.BlockSpec((t, dv), lambda h, p, qi, kj: (qi[p], h)),
                      pl.BlockSpec((t, dv), lambda h, p, qi, kj: (kj[p], h)),
                      pl.BlockSpec((t, dv), lambda h, p, qi, kj: (kj[p], h)),
                      pl.BlockSpec((4, dq), lambda h, p, qi, kj: (0, 0)),
                      pl.BlockSpec((1, dv), lambda h, p, qi, kj: (0, 0))],
            out_specs=pl.BlockSpec((t, dv), lambda h, p, qi, kj: (qi[p], h)),
            scratch_shapes=scratch,
        )
        return pl.pallas_call(
            functools.partial(body, dq=dq, lam_init=lam_init),
            grid_spec=grid_spec,
            out_shape=jax.ShapeDtypeStruct((l, h_b * dv), BF16),
            compiler_params=_params("parallel", "arbitrary"),
            name=name,
        )(qi_tab, kj_tab, q, k, v, lam_params, subln_g)

    bounded = lambda: call(_attn_prompt_bounded_kernel,
                           [pltpu.VMEM((2, t, LANES), F32), pltpu.VMEM((2, t, dv), F32)],
                           "attn_prompt_bounded")
    online = lambda: call(_attn_prompt_kernel,
                          [pltpu.VMEM((2, t, 1), F32), pltpu.VMEM((2, t, 1), F32),
                           pltpu.VMEM((2, t, dv), F32)], "attn_prompt_online")
    return lax.cond(score_bound <= MAX_UNSHIFTED_SCORE, bounded, online)


def _attn_decode_kernel(pt_ref, q_ref, kn_ref, vn_ref, *rest, h_b, dq, lam_init, pp):
    del pt_ref
    kc_refs = rest[:pp]
    vc_refs = rest[pp:2 * pp]
    lamp_ref, g_ref, o_ref, m_ref, l_ref, acc_ref = rest[2 * pp:]
    p = pl.program_id(1)
    dv = 2 * dq
    page = kc_refs[0].shape[2]
    assert h_b & (h_b - 1) == 0
    qm = q_ref[0]

    @pl.when(p == 0)
    def _():
        m_ref[...] = jnp.sum(qm * kn_ref[0], axis=1, keepdims=True)
        l_ref[...] = jnp.ones(l_ref.shape, F32)
        acc_ref[...] = vn_ref[0]

    qb = qm.astype(BF16)
    scores = []
    for j in range(pp):
        k2 = kc_refs[j][0, 0].reshape(page * h_b, dv).astype(BF16)
        s = lax.dot_general(qb, k2, NT_DIMS, preferred_element_type=F32)
        sub = lax.broadcasted_iota(jnp.int32, s.shape, 0)
        lane = lax.broadcasted_iota(jnp.int32, s.shape, 1)
        scores.append(jnp.where((lane & (h_b - 1)) == (sub >> 1), s, NEG_BIG))
    m_prev = m_ref[...]
    m_new = m_prev
    for s in scores:
        m_new = jnp.maximum(m_new, jnp.max(s, axis=1, keepdims=True))
    alpha = jnp.exp(m_prev - m_new)
    l_new = alpha * l_ref[...]
    acc = alpha * acc_ref[...]
    for j in range(pp):
        pm = jnp.exp(scores[j] - m_new)
        l_new = l_new + jnp.sum(pm, axis=1, keepdims=True)
        v2 = vc_refs[j][0, 0].reshape(page * h_b, dv).astype(BF16)
        acc = acc + jnp.dot(pm.astype(BF16), v2, preferred_element_type=F32)
    l_ref[...] = l_new
    acc_ref[...] = acc
    m_ref[...] = m_new

    @pl.when(p == pl.num_programs(1) - 1)
    def _():
        lam = _diff_lambda(lamp_ref, lam_init)
        for h in range(h_b):
            o1 = acc_ref[2 * h:2 * h + 1, :] / l_ref[2 * h:2 * h + 1, :]
            o2 = acc_ref[2 * h + 1:2 * h + 2, :] / l_ref[2 * h + 1:2 * h + 2, :]
            o = o1 - lam * o2
            ms = jnp.mean(o * o, axis=-1, keepdims=True)
            o_ref[0, :, h * dv:(h + 1) * dv] = (o * lax.rsqrt(ms + SUBLN_EPS) * g_ref[...]
                                                * (1.0 - lam_init)).astype(o_ref.dtype)


def _attn_decode(qn, kn, vn, cache_k, cache_v, page_table, lam_params, subln_g, h_b, dq,
                 lam_init, lyr):
    b = qn.shape[0]
    n_pages = page_table.shape[1]
    page = cache_k.shape[2]
    dv = 2 * dq
    nsub = 2 * h_b
    pp = _tile(n_pages, TILES["decode_pages"], 1)
    zeros = jnp.zeros((b, h_b, dq), F32)

    def sub_rows(x):
        x4 = x.reshape(b, h_b, 2, dq)
        return jnp.stack([jnp.concatenate([x4[:, :, 0], zeros], axis=-1),
                          jnp.concatenate([zeros, x4[:, :, 1]], axis=-1)], axis=2).reshape(b, nsub, dv)

    qm = sub_rows(qn)
    km = sub_rows(kn)
    vm = jnp.repeat(vn.reshape(b, h_b, 1, dv), 2, axis=2).reshape(b, nsub, dv)
    rows = lambda: pl.BlockSpec((1, nsub, dv), lambda i, p, pt: (i, 0, 0))

    def pages(j):
        return pl.BlockSpec((1, 1, page, h_b, dv), lambda i, p, pt: (lyr, pt[i, p * pp + j], 0, 0, 0))

    grid_spec = pltpu.PrefetchScalarGridSpec(
        num_scalar_prefetch=1,
        grid=(b, n_pages // pp),
        in_specs=([rows(), rows(), rows()] + [pages(j) for j in range(pp)]
                  + [pages(j) for j in range(pp)]
                  + [pl.BlockSpec((4, dq), lambda i, p, pt: (0, 0)),
                     pl.BlockSpec((1, dv), lambda i, p, pt: (0, 0))]),
        out_specs=pl.BlockSpec((1, 1, h_b * dv), lambda i, p, pt: (i, 0, 0)),
        scratch_shapes=[pltpu.VMEM((nsub, 1), F32), pltpu.VMEM((nsub, 1), F32),
                        pltpu.VMEM((nsub, dv), F32)],
    )
    return pl.pallas_call(
        functools.partial(_attn_decode_kernel, h_b=h_b, dq=dq, lam_init=lam_init, pp=pp),
        grid_spec=grid_spec,
        out_shape=jax.ShapeDtypeStruct((b, 1, h_b * dv), BF16),
        compiler_params=_params("parallel", "arbitrary"),
        name="attn_decode",
    )(page_table, qm, km, vm, *([cache_k] * pp), *([cache_v] * pp), lam_params, subln_g)


def _ep_plain(accs, e, o, carry, first, last):
    o[0][...] = accs[0].astype(o[0].dtype)


def _ep_bias(accs, e, o, carry, first, last):
    o[0][...] = accs[0] + e[0][...]


def _ep_silu(accs, e, o, carry, first, last):
    o[0][...] = _silu(accs[0]).astype(o[0].dtype)


def _ep_sigmoid(accs, e, o, carry, first, last):
    o[0][...] = _sigmoid(accs[0]).astype(o[0].dtype)


def _group_norm_store(y, gain, eps, out_refs):
    tn = y.shape[1]
    for g in range(tn // LANES):
        sl = slice(g * LANES, (g + 1) * LANES)
        blk = y[:, sl]
        ms = jnp.sum(blk * blk, axis=-1, keepdims=True) / LANES
        val = blk * lax.rsqrt(ms + eps) * gain[:, sl]
        for r in out_refs:
            r[:, sl] = val.astype(r.dtype)


def _ep_qknorm(accs, e, o, carry, first, last):
    _group_norm_store(accs[0], e[0][...], EPS, o)


def _ep_copy2(accs, e, o, carry, first, last):
    for r in o:
        r[...] = accs[0].astype(r.dtype)


def _sample_post_kernel(q_ref, k_ref, ga_ref, gb_ref, gq_ref, gk_ref, qo_ref, ko_ref, sa_ref, sb_ref):
    _group_norm_store(q_ref[...], gq_ref[...], EPS, [qo_ref])
    _group_norm_store(k_ref[...], gk_ref[...], EPS, [ko_ref])
    sa_ref[...] = _sigmoid(ga_ref[...])
    sb_ref[...] = _sigmoid(gb_ref[...])


def _ep_gdn_conv(accs, e, o, carry, first, last, *, width):
    raw = accs[0]
    taps = [e[j][...] for j in range(width)]
    nflag = e[width][...]
    gain = e[width + 1][...]
    val = _silu(_conv_taps(raw, carry[0], taps, first))
    for g in range(val.shape[1] // LANES):
        sl = slice(g * LANES, (g + 1) * LANES)
        blk = val[:, sl]
        ss = jnp.sum(blk * blk, axis=-1, keepdims=True)
        scale = jnp.where(nflag[:, sl] > 0.5, lax.rsqrt(ss + EPS), 1.0) * gain[:, sl]
        o[0][:, sl] = (blk * scale).astype(o[0].dtype)
    if last:
        o[1][...] = carry[0][0:SUBLANES, :]


def _ep_branch(accs, e, o, carry, first, last):
    o[0][...] = (e[0][...].astype(F32) * accs[0] + e[1][...].astype(F32) * accs[1]).astype(o[0].dtype)


def _ep_residual(accs, e, o, carry, first, last):
    o[0][...] = e[0][...] + e[1][...] * accs[0]


def _ep_ffn_prompt(accs, e, o, carry, first, last, *, width):
    outs = []
    for d in range(2):
        taps = [e[d * (width + 1) + j][...] for j in range(width)]
        bias = e[d * (width + 1) + width][...]
        outs.append(_conv_taps(accs[d], carry[d], taps, first) + bias)
        if last:
            o[1 + d][...] = carry[d][0:SUBLANES, :]
    o[0][...] = (_silu(outs[0]) * outs[1]).astype(o[0].dtype)


def _ffn_sample_kernel(*refs, width):
    ns = width - 1
    ups = refs[0:2]
    states = (refs[2:2 + ns], refs[2 + ns:2 + 2 * ns])
    taps = refs[2 + 2 * ns:4 + 2 * ns]
    bias = refs[4 + 2 * ns:6 + 2 * ns]
    o_ref = refs[6 + 2 * ns]
    vals = []
    for d in range(2):
        y = ups[d][...] * taps[d][width - 1:width, :]
        for j in range(ns):
            y = y + states[d][j][...] * taps[d][j:j + 1, :]
        vals.append(y + bias[d][...])
    o_ref[...] = (_silu(vals[0]) * vals[1]).astype(o_ref.dtype)


def kernel(x_prompt, x_sample, c_prompt, c_sample, cache_k, cache_v, state_gdn, state_gdn_conv, state_ffn_conv, page_table, w_ada, b_ada, norm1_g, norm2_g, w_in, gdn_conv_w, gdn_a_log, gdn_dt_bias, gdn_norm_g, diff_q_norm_g, diff_k_norm_g, diff_lambda, diff_subln_g, w_branch_a, w_branch_b, w_o, w_up, ffn_conv_w, ffn_conv_b, w_down):
    depth = w_in.shape[0]
    assert depth == 1 and x_prompt.shape[0] == 1 and x_sample.shape[1] == 1
    lyr = 0
    lam_init = 0.8 - 0.6 * math.exp(-0.3 * lyr)
    d = x_prompt.shape[-1]
    l = x_prompt.shape[1]
    nb = x_sample.shape[0]
    h_a, dk, dva = state_gdn.shape[2:]
    h_b = cache_k.shape[3]
    dq = cache_k.shape[4] // 2
    dvb = cache_v.shape[4]
    assert dk == LANES and dva == LANES and dq == LANES and dvb == 2 * dq
    d_ff = w_down.shape[1]
    gw = gdn_conv_w.shape[1]
    fw = ffn_conv_w.shape[1]
    hd_a = h_a * dk
    hd_b = h_b * dvb

    w_t = jnp.swapaxes(w_in[lyr], 0, 1)
    c_za = 3 * hd_a
    c_ba = c_za + hd_a
    n_head = c_ba + LANES
    c_qb = c_ba + 2 * h_a
    c_kb, c_vb, c_ga, c_gb = c_qb + hd_b, c_qb + 2 * hd_b, c_qb + 3 * hd_b, c_qb + 3 * hd_b + d
    n_tail = 3 * hd_b + 2 * d
    assert c_qb + n_tail == w_t.shape[0] and 2 * h_a <= LANES
    t_qb, t_kb, t_vb, t_ga, t_gb = 0, hd_b, 2 * hd_b, 3 * hd_b, 3 * hd_b + d
    w_up0 = w_up[lyr]
    w_dn = w_down[lyr].astype(BF16)
    w_o_b = w_o[lyr]
    w_ba_b = w_branch_a[lyr]
    w_bb_b = w_branch_b[lyr]

    row = lambda v: v.reshape(1, -1).astype(F32)
    tile_rows = lambda v, reps: jnp.tile(row(v), (1, reps))

    c_all = jnp.concatenate([c_prompt, c_sample], axis=0).astype(F32)
    mp = -(-c_all.shape[0] // 16) * 16
    c_all = jnp.pad(c_all, ((0, mp - c_all.shape[0]), (0, 0)))
    (mod,) = _matmul("adaln", [c_all], [(w_ada[lyr], 0)], [(0, 0)], 6 * d, _ep_bias, [(F32, "tile")],
                     tm=mp, tn=_tile(6 * d, 512, LANES), extras=[(row(b_ada[lyr]), "row", 0)],
                     a_fn=_silu)
    mod_p = [mod[0:1, j * d:(j + 1) * d] for j in range(6)]
    mod_s = [mod[1:1 + nb, j * d:(j + 1) * d] for j in range(6)]

    gq = tile_rows(diff_q_norm_g[lyr], 2 * h_b) * (dq ** -0.5)
    gk = tile_rows(diff_k_norm_g[lyr], 2 * h_b)
    gq2 = gq * math.log2(math.e)
    score_bound = dq * jnp.max(jnp.abs(gq2)) * jnp.max(jnp.abs(gk))
    cw = gdn_conv_w[lyr].astype(F32)
    conv_nflag = jnp.concatenate([jnp.ones((1, 2 * hd_a), F32), jnp.zeros((1, hd_a), F32)], axis=1)
    conv_gain = jnp.concatenate([jnp.full((1, hd_a), dk ** -0.5, F32), jnp.ones((1, 2 * hd_a), F32)], axis=1)
    gnorm = row(gdn_norm_g[lyr])
    subln = row(diff_subln_g[lyr])
    lamp = diff_lambda[lyr].astype(F32)
    fcw = ffn_conv_w[lyr].astype(F32)
    fcb = row(ffn_conv_b[lyr])
    a_log = gdn_a_log[lyr].astype(F32)
    dt_bias = gdn_dt_bias[lyr].astype(F32)

    xp = x_prompt[0].astype(F32)
    tm = _tile(l, TILES["mm_m"], 16)
    tn = TILES["mm_n"]
    split = 4 if tm % 64 == 0 else 1
    h1 = _normmod(xp, row(norm1_g[lyr]), mod_p[1], mod_p[0])
    xs = x_sample[:, 0, :].astype(F32)
    h1s = _normmod(xs, row(norm1_g[lyr]), mod_s[1], mod_s[0])

    stationary = dict(tm=tm, order="nm", cast_once=True, row_split=split, wt=True, side=h1s)
    qkv_c, qkv_tail, zq_s = _matmul(
        "p_qkv_conv", [h1], [(w_t, 0)], [(0, 0)], 3 * hd_a,
        functools.partial(_ep_gdn_conv, width=gw), [(BF16, "tile"), (F32, "tail")],
        tn=_tile(hd_a, tn, LANES), carry_rows=1, **stationary,
        extras=[(cw[j:j + 1], "row", 0) for j in range(gw)] + [(conv_nflag, "row", 0), (conv_gain, "row", 0)])
    za_p, za_s = _matmul("p_za", [h1], [(w_t, c_za)], [(0, 0)], hd_a, _ep_silu, [(BF16, "tile")],
                         tn=_tile(hd_a, tn, LANES), **stationary)
    tn_b = _tile(hd_b, tn, LANES)
    qb_p, qb_s = _matmul("p_qb", [h1], [(w_t, c_qb)], [(0, 0)], hd_b, _ep_qknorm, [(BF16, "tile")],
                         tn=tn_b, extras=[(gq2, "row", 0)], **stationary)
    kb_p, kb_p16, kb_s = _matmul("p_kb", [h1], [(w_t, c_kb)], [(0, 0)], hd_b, _ep_qknorm,
                                 [(F32, "tile"), (BF16, "tile")], tn=tn_b, extras=[(gk, "row", 0)],
                                 **stationary)
    vb_p, vb_p16, vb_s = _matmul("p_vb", [h1], [(w_t, c_vb)], [(0, 0)], hd_b, _ep_copy2,
                                 [(F32, "tile"), (BF16, "tile")], tn=tn_b, **stationary)
    tn_d = _tile(d, tn, LANES)
    gates_p, gates_s = _matmul("p_gates", [h1], [(w_t, c_ga)], [(0, 0)], 2 * d, _ep_sigmoid,
                               [(BF16, "tile")], tn=tn_d, **stationary)

    oa_p, s_p, ba_s = _gdn_prompt(qkv_c, h1, h1s, w_t, c_ba, za_p, a_log, dt_bias, gnorm, h_a, dk)
    ob_p = _attn_prompt(qb_p, kb_p16, vb_p16, score_bound, lamp, subln, h_b, dq, lam_init)

    tns = TILES["small_n"]
    oa_s, s_s, gbuf_s = _gdn_sample(zq_s, za_s, ba_s, state_gdn_conv[lyr].astype(F32),
                                    state_gdn[lyr].astype(F32), cw, a_log, dt_bias, gnorm, h_a, dk)
    qn_s, kn_s, sga_s, sgb_s = pl.pallas_call(
        _sample_post_kernel,
        out_shape=[jax.ShapeDtypeStruct((nb, hd_b), F32), jax.ShapeDtypeStruct((nb, hd_b), F32),
                   jax.ShapeDtypeStruct((nb, d), F32), jax.ShapeDtypeStruct((nb, d), F32)],
        name="sample_post",
    )(qb_s, kb_s, gates_s[:, :d], gates_s[:, d:], gq, gk)
    ob_s = _attn_decode(qn_s, kn_s, vb_s, cache_k, cache_v, page_table.astype(jnp.int32), lamp, subln,
                        h_b, dq, lam_init, lyr)
    tn_ds = _tile(d, tns, LANES)
    (mixed_s,) = _matmul("s_branch", [oa_s[:, 0, :], ob_s[:, 0, :]], [(w_ba_b, 0), (w_bb_b, 0)],
                         [(0, 0), (1, 1)], d, _ep_branch, [(BF16, "tile")], tm=nb, tn=tn_ds,
                         extras=[(sga_s, "tile", 0), (sgb_s, "tile", 0)])
    (x2_s,) = _matmul("s_wo", [mixed_s], [(w_o_b, 0)], [(0, 0)], d, _ep_residual, [(F32, "tile")],
                      tm=nb, tn=tn_ds, extras=[(xs, "tile", 0), (mod_s[2], "tile", 0)])
    h2s = _normmod(x2_s, row(norm2_g[lyr]), mod_s[4], mod_s[3])

    (mixed_p,) = _matmul("p_branch", [oa_p, ob_p], [(w_ba_b, 0), (w_bb_b, 0)], [(0, 0), (1, 1)], d,
                         _ep_branch, [(BF16, "tile")], tm=tm, tn=tn_d, row_split=split,
                         order="nm", cast_once=True,
                         extras=[(gates_p, "tile", 0), (gates_p, "tile", d)])
    (x2_p,) = _matmul("p_wo", [mixed_p], [(w_o_b, 0)], [(0, 0)], d, _ep_residual, [(F32, "tile")],
                      tm=tm, tn=tn_d, row_split=split, order="nm", cast_once=True,
                      extras=[(xp, "tile", 0), (mod_p[2], "row", 0)])
    h2 = _normmod(x2_p, row(norm2_g[lyr]), mod_p[4], mod_p[3])
    tn_u = _tile(d_ff, TILES["up_n"], LANES)
    assert d_ff % tn_u == 0
    ffn_rows = ([(fcw[j:j + 1], "row", 0) for j in range(fw)] + [(fcb, "row", 0)]
                + [(fcw[j:j + 1], "row", d_ff) for j in range(fw)] + [(fcb, "row", d_ff)])
    act_f, tail_g, tail_v, up_g, up_v = _matmul(
        "p_ffn_up", [h2], [(w_up0, 0), (w_up0, d_ff)], [(0, 0), (0, 1)], d_ff,
        functools.partial(_ep_ffn_prompt, width=fw),
        [(BF16, "tile"), (F32, "tail"), (F32, "tail")],
        tm=tm, tn=tn_u, order="nm", carry_rows=2, cast_once=True, row_split=split, extras=ffn_rows,
        side=h2s)
    (y_p,) = _matmul("p_down", [act_f], [(w_dn, 0)], [(0, 0)], d, _ep_residual, [(F32, "tile")],
                     tm=tm, tn=tn_d, tk=_tile(d_ff, TILES["down_k"], LANES),
                     extras=[(x2_p, "tile", 0), (mod_p[5], "row", 0)])

    fbuf = state_ffn_conv[lyr].astype(F32)
    tn_f = _tile(d_ff, 8192, LANES)
    nfb = d_ff // tn_f
    half = lambda rows, v: pl.BlockSpec((rows, tn_f), lambda j: (0, j + v * nfb))
    act_s = pl.pallas_call(
        functools.partial(_ffn_sample_kernel, width=fw),
        grid=(nfb,),
        in_specs=([half(nb, 0), half(nb, 0)]
                  + [half(nb, 0) for _ in range(fw - 1)] + [half(nb, 1) for _ in range(fw - 1)]
                  + [half(fw, 0), half(fw, 1), half(1, 0), half(1, 1)]),
        out_specs=half(nb, 0),
        out_shape=jax.ShapeDtypeStruct((nb, d_ff), BF16),
        compiler_params=_params("parallel"),
        name="s_ffn_conv",
    )(up_g, up_v, *[fbuf[:, j, :] for j in range(fw - 1)], *[fbuf[:, j, :] for j in range(fw - 1)],
      fcw, fcw, fcb, fcb)
    (y_s,) = _matmul("s_down", [act_s], [(w_dn, 0)], [(0, 0)], d, _ep_residual, [(F32, "tile")],
                     tm=nb, tn=tn_ds, tk=_tile(d_ff, TILES["down_k"], LANES),
                     extras=[(x2_s, "tile", 0), (mod_s[5], "tile", 0)])

    y_prompt = y_p[None].astype(x_prompt.dtype)
    y_sample = y_s[:, None, :].astype(x_sample.dtype)
    k_prompt = kb_p.reshape(1, 1, l, h_b, 2 * dq)
    v_prompt = vb_p.reshape(1, 1, l, h_b, dvb)
    gdn_state_prompt = s_p[None, None]
    gdn_conv_prompt = qkv_tail[SUBLANES - (gw - 1):][None, None]
    ffn_conv_prompt = jnp.concatenate([tail_g[SUBLANES - (fw - 1):], tail_v[SUBLANES - (fw - 1):]],
                                      axis=1)[None, None]
    k_sample = kn_s.reshape(1, nb, 1, h_b, 2 * dq)
    v_sample = vb_s.reshape(1, nb, 1, h_b, dvb)
    gdn_state_sample = s_s[None]
    gdn_conv_sample = gbuf_s[None]
    up_new = jnp.concatenate([up_g, up_v], axis=1)
    ffn_conv_sample = jnp.concatenate([fbuf[:, 1:, :], up_new[:, None, :]], axis=1)[None]
    return (y_prompt, y_sample, k_prompt, v_prompt, gdn_state_prompt, gdn_conv_prompt,
            ffn_conv_prompt, k_sample, v_sample, gdn_state_sample, gdn_conv_sample, ffn_conv_sample)
```

```python
import functools
import math

import jax
import jax.numpy as jnp
from jax import lax
from jax.experimental import pallas as pl
from jax.experimental.pallas import tpu as pltpu

F32 = jnp.float32
BF16 = jnp.bfloat16

LANES = 128
SUBLANES = 8
VMEM_LIMIT_BYTES = 56 * 1024 * 1024

EPS = 1e-6
SUBLN_EPS = 1e-5
GDN_CHUNK = 128
MAX_UNSHIFTED_SCORE = 40.0
NEG_BIG = -1e30

TILES = dict(
    norm_rows=512,
    mm_m=1024, mm_n=512,
    up_n=256,
    down_k=5504,
    attn=1024,
    gdn_heads=8,
    prep_rows=512,
    decode_pages=8,
    small_n=1024,
)

NT_DIMS = (((1,), (1,)), ((), ()))


def _params(*sem):
    return pltpu.CompilerParams(dimension_semantics=sem, vmem_limit_bytes=VMEM_LIMIT_BYTES)


def _tile(dim, pref, align):
    if dim <= pref:
        return dim
    t = (pref // align) * align
    while t >= align:
        if dim % t == 0:
            return t
        t -= align
    return dim


def _sigmoid(x):
    return 1.0 / (1.0 + jnp.exp(-x))


def _silu(x):
    return x * _sigmoid(x)


def _softplus(x):
    return jnp.maximum(x, 0.0) + jnp.log1p(jnp.exp(-jnp.abs(x)))


def _normmod_kernel(x_ref, g_ref, sc_ref, sh_ref, o_ref):
    x = x_ref[...]
    ms = jnp.mean(x * x, axis=-1, keepdims=True)
    h = x * lax.rsqrt(ms + EPS) * g_ref[...]
    o_ref[...] = (h * (1.0 + sc_ref[...]) + sh_ref[...]).astype(o_ref.dtype)


def _normmod(x, g, sc, sh):
    m, d = x.shape
    tm = _tile(m, TILES["norm_rows"], SUBLANES)
    per_row = sc.shape[0] != 1
    mod_spec = (pl.BlockSpec((tm, d), lambda i: (i, 0)) if per_row
                else pl.BlockSpec((1, d), lambda i: (0, 0)))
    return pl.pallas_call(
        _normmod_kernel,
        grid=(m // tm,),
        in_specs=[pl.BlockSpec((tm, d), lambda i: (i, 0)),
                  pl.BlockSpec((1, d), lambda i: (0, 0)), mod_spec, mod_spec],
        out_specs=pl.BlockSpec((tm, d), lambda i: (i, 0)),
        out_shape=jax.ShapeDtypeStruct((m, d), BF16),
        compiler_params=_params("parallel"),
        name="normmod",
    )(x, g, sc, sh)


def _matmul(name, as_, ws, dots, n, epilogue, outs, *, tm, tn, extras=(), order="mn",
            tk=None, a_fn=None, carry_rows=0, cast_once=False, row_split=1, wt=False, side=None):
    m = as_[0].shape[0]
    kdim = as_[0].shape[1]
    tk = kdim if tk is None else tk
    nk = kdim // tk
    assert kdim % tk == 0 and m % tm == 0 and n % tn == 0 and tm % row_split == 0
    if nk > 1:
        assert all(a.shape[1] == kdim for a in as_) and row_split == 1 and not cast_once
    assert not cast_once or order == "nm"
    ni, nj = m // tm, n // tn
    na, nw, ne, no, nd = len(as_), len(ws), len(extras), len(outs), len(dots)
    rows = tm // row_split

    if order == "mn":
        grid = (ni, nj, nk)
        ij = lambda g0, g1: (g0, g1)
    else:
        grid = (nj, ni, nk)
        ij = lambda g0, g1: (g1, g0)

    def a_map(g0, g1, k):
        return (ij(g0, g1)[0], k)

    def w_map(off):
        return lambda g0, g1, k: (k, ij(g0, g1)[1] + off // tn)

    def row_map(off):
        return lambda g0, g1, k: (0, ij(g0, g1)[1] + off // tn)

    def tile_map(off):
        return lambda g0, g1, k: (ij(g0, g1)[0], ij(g0, g1)[1] + off // tn)

    in_specs = []
    for a in as_:
        in_specs.append(pl.BlockSpec((tm, tk if nk > 1 else a.shape[1]), a_map))
    for w, off in ws:
        if wt:
            assert nk == 1 and off % SUBLANES == 0
            in_specs.append(pl.BlockSpec(
                (pl.Element(tn), pl.Element(w.shape[1])),
                functools.partial(
                    lambda g0, g1, k, off: (pl.multiple_of(off + ij(g0, g1)[1] * tn, SUBLANES), 0),
                    off=off)))
        else:
            assert off % tn == 0
            in_specs.append(pl.BlockSpec((tk if nk > 1 else w.shape[0], tn), w_map(off)))
    for arr, kind, off in extras:
        assert off % tn == 0
        if kind == "row":
            in_specs.append(pl.BlockSpec((1, tn), row_map(off)))
        else:
            in_specs.append(pl.BlockSpec((tm, tn), tile_map(off)))
    out_specs, out_shapes = [], []
    for dtype, kind in outs:
        if kind == "tile":
            out_specs.append(pl.BlockSpec((tm, tn), tile_map(0)))
            out_shapes.append(jax.ShapeDtypeStruct((m, n), dtype))
        else:
            out_specs.append(pl.BlockSpec((SUBLANES, tn), row_map(0)))
            out_shapes.append(jax.ShapeDtypeStruct((SUBLANES, n), dtype))
    nside = 0
    sides = []
    if side is not None:
        sides = list(side) if isinstance(side, (list, tuple)) else [side] * nw
        assert cast_once and len(sides) == nw
        nside = nw
        for sd in sides:
            in_specs.append(pl.BlockSpec(sd.shape, lambda g0, g1, k: (0, 0)))
        for sd in sides:
            out_specs.append(pl.BlockSpec((sd.shape[0], tn), row_map(0)))
            out_shapes.append(jax.ShapeDtypeStruct((sd.shape[0], n), F32))
    scratch = []
    if nk > 1:
        scratch += [pltpu.VMEM((tm, tn), F32) for _ in range(nd)]
    if cast_once:
        scratch += [pltpu.VMEM((tn, w.shape[1]) if wt else (w.shape[0], tn), BF16) for w, _ in ws]
    scratch += [pltpu.VMEM((SUBLANES + rows, tn), F32) for _ in range(carry_rows)]

    def body(*refs):
        a_refs = refs[:na]
        w_refs = refs[na:na + nw]
        e_refs = refs[na + nw:na + nw + ne]
        nin = na + nw + ne + nside
        side_refs = refs[nin - nside:nin]
        o_refs = refs[nin:nin + no]
        side_out = refs[nin + no:nin + no + nside]
        s_refs = list(refs[nin + no + nside:])
        acc_refs = [s_refs.pop(0) for _ in range(nd)] if nk > 1 else []
        wb_refs = [s_refs.pop(0) for _ in range(nw)] if cast_once else list(w_refs)
        carry_refs = s_refs
        i, _ = ij(pl.program_id(0), pl.program_id(1))
        k = pl.program_id(2)

        def partial(d, r):
            ai, wi = dots[d]
            a = a_refs[ai][r * rows:(r + 1) * rows, :] if row_split > 1 else a_refs[ai][...]
            if a_fn is not None:
                a = a_fn(a)
            w = wb_refs[wi][...].astype(BF16)
            if wt:
                return lax.dot_general(a.astype(BF16), w, NT_DIMS, preferred_element_type=F32)
            return jnp.dot(a.astype(BF16), w, preferred_element_type=F32)

        if cast_once:
            @pl.when(i == 0)
            def _():
                for wi in range(nw):
                    wb_refs[wi][...] = w_refs[wi][...].astype(BF16)
                for wi in range(nside):
                    wb = wb_refs[wi][...]
                    sd = side_refs[wi][...]
                    side_out[wi][...] = (
                        lax.dot_general(sd, wb, NT_DIMS, preferred_element_type=F32)
                        if wt else jnp.dot(sd, wb, preferred_element_type=F32))

        if nk == 1:
            for r in range(row_split):
                if row_split > 1:
                    piece = pl.ds(r * rows, rows)
                    ev = [e.at[piece] if extras[x][1] == "tile" else e for x, e in enumerate(e_refs)]
                    ov = [o.at[piece] if outs[x][1] == "tile" else o for x, o in enumerate(o_refs)]
                else:
                    ev, ov = e_refs, o_refs
                epilogue([partial(d, r) for d in range(nd)], ev, ov, carry_refs,
                         (i == 0) if r == 0 else False, r == row_split - 1)
        else:
            @pl.when(k == 0)
            def _():
                for d in range(nd):
                    acc_refs[d][...] = jnp.zeros((tm, tn), F32)

            for d in range(nd):
                acc_refs[d][...] += partial(d, 0)

            @pl.when(k == nk - 1)
            def _():
                epilogue([acc_refs[d][...] for d in range(nd)], e_refs, o_refs,
                         carry_refs, i == 0, True)

    return pl.pallas_call(
        body,
        grid=grid,
        in_specs=in_specs,
        out_specs=out_specs,
        out_shape=out_shapes,
        scratch_shapes=scratch,
        compiler_params=_params("arbitrary", "arbitrary", "arbitrary"),
        name=name,
    )(*as_, *[w for w, _ in ws], *[e for e, _, _ in extras], *sides)


def _conv_taps(raw, win_ref, taps, first):
    width = len(taps)
    rows = raw.shape[0]
    if first is not False:
        @pl.when(first)
        def _():
            win_ref[0:SUBLANES, :] = jnp.zeros((SUBLANES, raw.shape[1]), F32)

    win_ref[SUBLANES:SUBLANES + rows, :] = raw
    y = win_ref[SUBLANES:SUBLANES + rows, :] * taps[width - 1]
    for s in range(1, width):
        y = y + win_ref[SUBLANES - s:SUBLANES - s + rows, :] * taps[width - 1 - s]
    win_ref[0:SUBLANES, :] = win_ref[rows:rows + SUBLANES, :]
    return y


def _gdn_gates(ba, alog_row, dtb_row):
    return _sigmoid(ba), -jnp.exp(alog_row) * _softplus(ba + dtb_row)


def _gdn_prep_kernel(h_ref, hs_ref, wba_ref, alog_ref, dtb_ref, beta_ref, gc_ref, gct_ref, bas_ref,
                     *, heads_pad, h_a):
    c = GDN_CHUNK
    wba = wba_ref[...].astype(BF16)

    @pl.when(pl.program_id(0) == 0)
    def _():
        bas_ref[...] = lax.dot_general(hs_ref[...], wba, NT_DIMS, preferred_element_type=F32)

    ba = lax.dot_general(h_ref[...], wba, NT_DIMS, preferred_element_type=F32)
    beta, g = _gdn_gates(ba, alog_ref[...], dtb_ref[...])
    beta_ref[...] = beta
    row = lax.broadcasted_iota(jnp.int32, (c, LANES), 0)
    for q in range(g.shape[0] // c):
        gq = g[q * c:(q + 1) * c, :]
        s = 1
        while s < c:
            gq = gq + jnp.where(row >= s, pltpu.roll(gq, s, axis=0), 0.0)
            s *= 2
        gc_ref[q * c:(q + 1) * c, :] = gq
        gct_ref[:, q * c:(q + 1) * c] = gq.T[h_a:h_a + heads_pad, :]


def _gdn_intra_kernel(q_ref, k_ref, v_ref, beta_ref, gc_ref, gct_ref,
                      uv_ref, w_ref, aqk_ref, qdec_ref, kdect_ref, *, hb, h_a):
    c = GDN_CHUNK
    grp = pl.program_id(1)
    lane = lax.broadcasted_iota(jnp.int32, (c, LANES), 1)
    row = lax.broadcasted_iota(jnp.int32, (c, c), 0)
    col = lax.broadcasted_iota(jnp.int32, (c, c), 1)
    eye = jnp.where(row == col, 1.0, 0.0)
    levels = range(1, int(math.log2(c)))
    below = [((row >> (lvl + 1)) == (col >> (lvl + 1))) & ((row >> lvl) != (col >> lvl))
             for lvl in levels]
    pair = (row >> 1) == (col >> 1)
    beta_all = beta_ref[...]
    gc_all = gc_ref[...]
    heads = range(hb)
    sls = [slice(j * LANES, (j + 1) * LANES) for j in heads]
    beta_c, gcc, kb, mm, t = [], [], [], [], []
    for j in heads:
        h = grp * hb + j
        beta_c.append(jnp.sum(jnp.where(lane == h, beta_all, 0.0), axis=1, keepdims=True))
        gcc.append(jnp.sum(jnp.where(lane == h + h_a, gc_all, 0.0), axis=1, keepdims=True))
        gcr = gct_ref[pl.ds(h, 1), :]
        dec_incl = jnp.exp(jnp.where(row >= col, gcc[j] - gcr, NEG_BIG))
        k = k_ref[:, sls[j]]
        kb.append(k.astype(F32) * beta_c[j])
        mm.append(lax.dot_general(kb[j].astype(BF16), k, NT_DIMS, preferred_element_type=F32)
                  * jnp.where(row > col, dec_incl, 0.0))
        aqk = lax.dot_general(q_ref[:, sls[j]], k, NT_DIMS, preferred_element_type=F32) * dec_incl
        aqk_ref[:, sls[j]] = aqk.astype(BF16)
        t.append(eye - jnp.where(pair, mm[j], 0.0))
    for msk in below:
        tb = [t[j].astype(BF16) for j in heads]
        bt = [jnp.dot(jnp.where(msk, mm[j], 0.0).astype(BF16), tb[j], preferred_element_type=F32)
              for j in heads]
        t = [t[j] - jnp.dot(tb[j], bt[j].astype(BF16), preferred_element_type=F32) for j in heads]
    for j in heads:
        tb = t[j].astype(BF16)
        egc = jnp.exp(gcc[j])
        gl = gcc[j][c - 1:c, :]
        vb = v_ref[:, sls[j]].astype(F32) * beta_c[j]
        uv_ref[:, sls[j]] = jnp.dot(tb, vb.astype(BF16), preferred_element_type=F32)
        w_ref[:, sls[j]] = jnp.dot(tb, (kb[j] * egc).astype(BF16),
                                   preferred_element_type=F32).astype(BF16)
        qdec_ref[:, sls[j]] = (q_ref[:, sls[j]].astype(F32) * egc).astype(BF16)
        kdect_ref[sls[j], :] = (k_ref[:, sls[j]].astype(F32) * jnp.exp(gl - gcc[j])).T.astype(BF16)


def _gdn_rec_kernel(uv_ref, w_ref, aqk_ref, qdec_ref, kdect_ref, gct_ref, za_ref, gn_ref,
                    o_ref, sout_ref, s_ref, *, hb):
    c = GDN_CHUNK
    n = pl.program_id(1)
    g0 = pl.program_id(0)

    @pl.when(n == 0)
    def _():
        s_ref[...] = jnp.zeros(s_ref.shape, F32)

    heads = range(hb)
    sls = [slice(j * LANES, (j + 1) * LANES) for j in heads]
    s = [s_ref[j] for j in heads]
    sb = [s[j].astype(BF16) for j in heads]
    ub = [(uv_ref[:, sls[j]] - jnp.dot(w_ref[:, sls[j]], sb[j], preferred_element_type=F32)
           ).astype(BF16) for j in heads]
    oq = [jnp.dot(qdec_ref[:, sls[j]], sb[j], preferred_element_type=F32) for j in heads]
    for j in heads:
        gl = jnp.exp(gct_ref[pl.ds(g0 * hb + j, 1), c - 1:c])
        s_ref[j] = gl * s[j] + jnp.dot(kdect_ref[sls[j], :], ub[j], preferred_element_type=F32)
    for j in heads:
        o = oq[j] + jnp.dot(aqk_ref[:, sls[j]], ub[j], preferred_element_type=F32)
        ms = jnp.mean(o * o, axis=-1, keepdims=True)
        o_ref[:, sls[j]] = (o * lax.rsqrt(ms + EPS) * gn_ref[...]
                            * za_ref[:, sls[j]].astype(F32)).astype(o_ref.dtype)

    @pl.when(n == pl.num_programs(1) - 1)
    def _():
        sout_ref[...] = s_ref[...]


def _gate_rows(a_log, dt_bias, h_a):
    alog_row = jnp.zeros((1, LANES), F32).at[0, h_a:2 * h_a].set(a_log)
    dtb_row = jnp.zeros((1, LANES), F32).at[0, h_a:2 * h_a].set(dt_bias)
    return alog_row, dtb_row


def _gdn_prompt(qkv, h1, h1s, w_t, c_ba, za_silu, a_log, dt_bias, gnorm, h_a, dk):
    l = qkv.shape[0]
    kdim = h1.shape[1]
    c = GDN_CHUNK
    assert l % c == 0 and dk == LANES and 2 * h_a <= LANES and c_ba % LANES == 0
    nchunk = l // c
    hp = max(SUBLANES, -(-h_a // SUBLANES) * SUBLANES)
    alog_row, dtb_row = _gate_rows(a_log, dt_bias, h_a)

    ms = h1s.shape[0]
    rp = _tile(l, TILES["prep_rows"], c)
    beta, gc, gct, ba_s = pl.pallas_call(
        functools.partial(_gdn_prep_kernel, heads_pad=hp, h_a=h_a),
        grid=(l // rp,),
        in_specs=[pl.BlockSpec((rp, kdim), lambda n: (n, 0)),
                  pl.BlockSpec((ms, kdim), lambda n: (0, 0)),
                  pl.BlockSpec((LANES, kdim), lambda n: (c_ba // LANES, 0)),
                  pl.BlockSpec((1, LANES), lambda n: (0, 0)),
                  pl.BlockSpec((1, LANES), lambda n: (0, 0))],
        out_specs=[pl.BlockSpec((rp, LANES), lambda n: (n, 0)),
                   pl.BlockSpec((rp, LANES), lambda n: (n, 0)),
                   pl.BlockSpec((hp, rp), lambda n: (0, n)),
                   pl.BlockSpec((ms, LANES), lambda n: (0, 0))],
        out_shape=[jax.ShapeDtypeStruct((l, LANES), F32),
                   jax.ShapeDtypeStruct((l, LANES), F32),
                   jax.ShapeDtypeStruct((hp, l), F32),
                   jax.ShapeDtypeStruct((ms, LANES), F32)],
        compiler_params=_params("arbitrary"),
        name="gdn_prep",
    )(h1, h1s, w_t, alog_row, dtb_row)

    hd = h_a * dk
    hb = _tile(h_a, TILES["gdn_heads"], 1)
    ng = h_a // hb
    blk = lambda off: pl.BlockSpec((c, hb * LANES), lambda n, g: (n, g + off))
    uv, w, aqk, qdec, kdect = pl.pallas_call(
        functools.partial(_gdn_intra_kernel, hb=hb, h_a=h_a),
        grid=(nchunk, ng),
        in_specs=[blk(0), blk(ng), blk(2 * ng),
                  pl.BlockSpec((c, LANES), lambda n, g: (n, 0)),
                  pl.BlockSpec((c, LANES), lambda n, g: (n, 0)),
                  pl.BlockSpec((hp, c), lambda n, g: (0, n))],
        out_specs=[blk(0), blk(0), blk(0), blk(0),
                   pl.BlockSpec((hb * LANES, c), lambda n, g: (g, n))],
        out_shape=[jax.ShapeDtypeStruct((l, hd), F32),
                   jax.ShapeDtypeStruct((l, hd), BF16),
                   jax.ShapeDtypeStruct((l, hd), BF16),
                   jax.ShapeDtypeStruct((l, hd), BF16),
                   jax.ShapeDtypeStruct((hd, l), BF16)],
        compiler_params=_params("parallel", "arbitrary"),
        name="gdn_intra",
    )(qkv, qkv, qkv, beta, gc, gct)

    hr = hb
    wide = lambda: pl.BlockSpec((c, hr * LANES), lambda g, n: (n, g))
    oa, s_fin = pl.pallas_call(
        functools.partial(_gdn_rec_kernel, hb=hr),
        grid=(h_a // hr, nchunk),
        in_specs=[wide(), wide(), wide(), wide(),
                  pl.BlockSpec((hr * LANES, c), lambda g, n: (g, n)),
                  pl.BlockSpec((hp, c), lambda g, n: (0, n)),
                  wide(),
                  pl.BlockSpec((1, LANES), lambda g, n: (0, 0))],
        out_specs=[wide(), pl.BlockSpec((hr, dk, LANES), lambda g, n: (g, 0, 0))],
        out_shape=[jax.ShapeDtypeStruct((l, hd), BF16),
                   jax.ShapeDtypeStruct((h_a, dk, LANES), F32)],
        scratch_shapes=[pltpu.VMEM((hr, dk, LANES), F32)],
        compiler_params=_params("parallel", "arbitrary"),
        name="gdn_rec",
    )(uv, w, aqk, qdec, kdect, gct, za_silu, gnorm)
    return oa, s_fin, ba_s


def _gdn_sample_kernel(zq_ref, za_ref, ba_ref, buf_ref, s_ref, cw_ref, alog_ref, dtb_ref, gn_ref,
                       o_ref, snew_ref, bufnew_ref, *, h_a, dk, width):
    hd = h_a * dk
    raw = zq_ref[0]
    buf = buf_ref[0]
    y = raw * cw_ref[width - 1:width, :]
    for j in range(width - 1):
        y = y + buf[j:j + 1, :] * cw_ref[j:j + 1, :]
    y = _silu(y)
    bufnew_ref[0, 0:width - 2, :] = buf[1:width - 1, :]
    bufnew_ref[0, width - 2:width - 1, :] = raw
    za = za_ref[0]
    beta_row, g_row = _gdn_gates(ba_ref[0], alog_ref[...], dtb_ref[...])
    lane1 = lax.broadcasted_iota(jnp.int32, (1, LANES), 1)
    row = lax.broadcasted_iota(jnp.int32, (dk, LANES), 0)
    col = lax.broadcasted_iota(jnp.int32, (dk, LANES), 1)
    eye = row == col
    for h in range(h_a):
        q = y[:, h * dk:(h + 1) * dk]
        k = y[:, hd + h * dk:hd + (h + 1) * dk]
        v = y[:, 2 * hd + h * dk:2 * hd + (h + 1) * dk]
        q = q * lax.rsqrt(jnp.sum(q * q, axis=-1, keepdims=True) + EPS) * (dk ** -0.5)
        k = k * lax.rsqrt(jnp.sum(k * k, axis=-1, keepdims=True) + EPS)
        beta = jnp.sum(jnp.where(lane1 == h, beta_row, 0.0), axis=1, keepdims=True)
        a = jnp.exp(jnp.sum(jnp.where(lane1 == h + h_a, g_row, 0.0), axis=1, keepdims=True))
        k_col = jnp.sum(jnp.where(eye, k, 0.0), axis=1, keepdims=True)
        q_col = jnp.sum(jnp.where(eye, q, 0.0), axis=1, keepdims=True)
        s = a * s_ref[0, h]
        u = beta * (v - jnp.sum(s * k_col, axis=0, keepdims=True))
        s_new = s + k_col * u
        snew_ref[0, h] = s_new
        o = jnp.sum(s_new * q_col, axis=0, keepdims=True)
        ms = jnp.mean(o * o, axis=-1, keepdims=True)
        o_ref[0, :, h * dk:(h + 1) * dk] = (
            o * lax.rsqrt(ms + EPS) * gn_ref[...] * _silu(za[:, h * dk:(h + 1) * dk])
        ).astype(o_ref.dtype)


def _gdn_sample(zq, za, ba, buf, s0, conv_w, a_log, dt_bias, gnorm, h_a, dk):
    b = zq.shape[0]
    width = conv_w.shape[0]
    hd = h_a * dk
    alog_row, dtb_row = _gate_rows(a_log, dt_bias, h_a)
    per_seq = lambda n: pl.BlockSpec((1, 1, n), lambda i: (i, 0, 0))
    return pl.pallas_call(
        functools.partial(_gdn_sample_kernel, h_a=h_a, dk=dk, width=width),
        grid=(b,),
        in_specs=[per_seq(3 * hd), per_seq(hd), per_seq(LANES),
                  pl.BlockSpec((1, width - 1, 3 * hd), lambda i: (i, 0, 0)),
                  pl.BlockSpec((1, h_a, dk, LANES), lambda i: (i, 0, 0, 0)),
                  pl.BlockSpec((width, 3 * hd), lambda i: (0, 0)),
                  pl.BlockSpec((1, LANES), lambda i: (0, 0)),
                  pl.BlockSpec((1, LANES), lambda i: (0, 0)),
                  pl.BlockSpec((1, LANES), lambda i: (0, 0))],
        out_specs=[pl.BlockSpec((1, 1, hd), lambda i: (i, 0, 0)),
                   pl.BlockSpec((1, h_a, dk, LANES), lambda i: (i, 0, 0, 0)),
                   pl.BlockSpec((1, width - 1, 3 * hd), lambda i: (i, 0, 0))],
        out_shape=[jax.ShapeDtypeStruct((b, 1, hd), BF16),
                   jax.ShapeDtypeStruct((b, h_a, dk, LANES), F32),
                   jax.ShapeDtypeStruct((b, width - 1, 3 * hd), F32)],
        compiler_params=_params("parallel"),
        name="gdn_sample",
    )(zq[:, None, :], za[:, None, :], ba[:, None, :], buf, s0, conv_w, alog_row, dtb_row, gnorm)


def _diff_lambda(lamp_ref, lam_init):
    lp = lamp_ref[...]
    e1 = jnp.exp(jnp.sum(lp[0:1] * lp[1:2], axis=1, keepdims=True))
    e2 = jnp.exp(jnp.sum(lp[2:3] * lp[3:4], axis=1, keepdims=True))
    return e1 - e2 + lam_init


def _attn_prompt_kernel(qi_ref, kj_ref, q_ref, k_ref, v_ref, lamp_ref, g_ref, o_ref,
                        m_ref, l_ref, acc_ref, *, dq, lam_init):
    p = pl.program_id(1)
    qi = qi_ref[p]
    kj = kj_ref[p]

    @pl.when(kj == 0)
    def _():
        m_ref[...] = jnp.full(m_ref.shape, NEG_BIG, F32)
        l_ref[...] = jnp.zeros(l_ref.shape, F32)
        acc_ref[...] = jnp.zeros(acc_ref.shape, F32)

    def step(masked):
        v = v_ref[...]
        for c in range(2):
            s = lax.dot_general(q_ref[:, c * dq:(c + 1) * dq], k_ref[:, c * dq:(c + 1) * dq],
                                NT_DIMS, preferred_element_type=F32)
            if masked:
                row = lax.broadcasted_iota(jnp.int32, s.shape, 0)
                col = lax.broadcasted_iota(jnp.int32, s.shape, 1)
                s = jnp.where(row >= col, s, NEG_BIG)
            m_prev = m_ref[c]
            m_new = jnp.maximum(m_prev, jnp.max(s, axis=1, keepdims=True))
            alpha = jnp.exp2(m_prev - m_new)
            pm = jnp.exp2(s - m_new)
            l_ref[c] = alpha * l_ref[c] + jnp.sum(pm, axis=1, keepdims=True)
            acc_ref[c] = alpha * acc_ref[c] + jnp.dot(pm.astype(BF16), v,
                                                      preferred_element_type=F32)
            m_ref[c] = m_new

    @pl.when(kj < qi)
    def _():
        step(False)

    @pl.when(kj == qi)
    def _():
        step(True)
        _attn_finish(acc_ref, l_ref[0], l_ref[1], lamp_ref, g_ref, o_ref, lam_init)


def _attn_finish(acc_ref, l0, l1, lamp_ref, g_ref, o_ref, lam_init):
    lam = _diff_lambda(lamp_ref, lam_init)
    o = acc_ref[0] / l0 - lam * (acc_ref[1] / l1)
    ms = jnp.mean(o * o, axis=-1, keepdims=True)
    o_ref[...] = (o * lax.rsqrt(ms + SUBLN_EPS) * g_ref[...] * (1.0 - lam_init)).astype(o_ref.dtype)


def _attn_prompt_bounded_kernel(qi_ref, kj_ref, q_ref, k_ref, v_ref, lamp_ref, g_ref, o_ref,
                                l_ref, acc_ref, *, dq, lam_init):
    p = pl.program_id(1)
    qi = qi_ref[p]
    kj = kj_ref[p]

    @pl.when(kj == 0)
    def _():
        l_ref[...] = jnp.zeros(l_ref.shape, F32)
        acc_ref[...] = jnp.zeros(acc_ref.shape, F32)

    t = q_ref.shape[0]

    def accumulate(c, r0, nr, k0, nk, masked):
        s = lax.dot_general(q_ref[r0:r0 + nr, c * dq:(c + 1) * dq],
                            k_ref[k0:k0 + nk, c * dq:(c + 1) * dq], NT_DIMS,
                            preferred_element_type=F32)
        pm = jnp.exp2(s)
        if masked:
            row = lax.broadcasted_iota(jnp.int32, s.shape, 0) + r0
            col = lax.broadcasted_iota(jnp.int32, s.shape, 1) + k0
            pm = jnp.where(row >= col, pm, 0.0)
        part = pm[:, 0:LANES]
        for g in range(1, nk // LANES):
            part = part + pm[:, g * LANES:(g + 1) * LANES]
        l_ref[c, r0:r0 + nr, :] += part
        acc_ref[c, r0:r0 + nr, :] += jnp.dot(pm.astype(BF16), v_ref[k0:k0 + nk, :],
                                             preferred_element_type=F32)

    @pl.when(kj < qi)
    def _():
        for c in range(2):
            accumulate(c, 0, t, 0, t, False)

    @pl.when(kj == qi)
    def _():
        half = t // 2 if t % (2 * LANES) == 0 else t
        for c in range(2):
            accumulate(c, 0, half, 0, half, True)
            if half < t:
                accumulate(c, half, t - half, 0, t, True)
        _attn_finish(acc_ref, jnp.sum(l_ref[0], axis=1, keepdims=True),
                     jnp.sum(l_ref[1], axis=1, keepdims=True), lamp_ref, g_ref, o_ref, lam_init)


def _attn_prompt(q, k, v, score_bound, lam_params, subln_g, h_b, dq, lam_init):
    l = q.shape[0]
    dv = 2 * dq
    t = _tile(l, TILES["attn"], LANES)
    nb = l // t
    pairs = [(i, j) for i in range(nb) for j in range(i + 1)]
    qi_tab = jnp.asarray([a for a, _ in pairs], jnp.int32)
    kj_tab = jnp.asarray([b for _, b in pairs], jnp.int32)

    def call(body, scratch, name):
        grid_spec = pltpu.PrefetchScalarGridSpec(
            num_scalar_prefetch=2,
            grid=(h_b, len(pairs)),
            in_specs=[pl.BlockSpec((t, dv), lambda h, p, qi, kj: (qi[p], h)),
                      pl.BlockSpec((t, dv), lambda h, p, qi, kj: (kj[p], h)),
                      pl.BlockSpec((t, dv), lambda h, p, qi, kj: (kj[p], h)),
                      pl.BlockSpec((4, dq), lambda h, p, qi, kj: (0, 0)),
                      pl.BlockSpec((1, dv), lambda h, p, qi, kj: (0, 0))],
            out_specs=pl.BlockSpec((t, dv), lambda h, p, qi, kj: (qi[p], h)),
            scratch_shapes=scratch,
        )
        return pl.pallas_call(
            functools.partial(body, dq=dq, lam_init=lam_init),
            grid_spec=grid_spec,
            out_shape=jax.ShapeDtypeStruct((l, h_b * dv), BF16),
            compiler_params=_params("parallel", "arbitrary"),
            name=name,
        )(qi_tab, kj_tab, q, k, v, lam_params, subln_g)

    bounded = lambda: call(_attn_prompt_bounded_kernel,
                           [pltpu.VMEM((2, t, LANES), F32), pltpu.VMEM((2, t, dv), F32)],
                           "attn_prompt_bounded")
    online = lambda: call(_attn_prompt_kernel,
                          [pltpu.VMEM((2, t, 1), F32), pltpu.VMEM((2, t, 1), F32),
                           pltpu.VMEM((2, t, dv), F32)], "attn_prompt_online")
    return lax.cond(score_bound <= MAX_UNSHIFTED_SCORE, bounded, online)


def _attn_decode_kernel(pt_ref, q_ref, kn_ref, vn_ref, *rest, h_b, dq, lam_init, pp):
    del pt_ref
    kc_refs = rest[:pp]
    vc_refs = rest[pp:2 * pp]
    lamp_ref, g_ref, o_ref, m_ref, l_ref, acc_ref = rest[2 * pp:]
    p = pl.program_id(1)
    dv = 2 * dq
    page = kc_refs[0].shape[2]
    assert h_b & (h_b - 1) == 0
    qm = q_ref[0]

    @pl.when(p == 0)
    def _():
        m_ref[...] = jnp.sum(qm * kn_ref[0], axis=1, keepdims=True)
        l_ref[...] = jnp.ones(l_ref.shape, F32)
        acc_ref[...] = vn_ref[0]

    qb = qm.astype(BF16)
    scores = []
    for j in range(pp):
        k2 = kc_refs[j][0, 0].reshape(page * h_b, dv).astype(BF16)
        s = lax.dot_general(qb, k2, NT_DIMS, preferred_element_type=F32)
        sub = lax.broadcasted_iota(jnp.int32, s.shape, 0)
        lane = lax.broadcasted_iota(jnp.int32, s.shape, 1)
        scores.append(jnp.where((lane & (h_b - 1)) == (sub >> 1), s, NEG_BIG))
    m_prev = m_ref[...]
    m_new = m_prev
    for s in scores:
        m_new = jnp.maximum(m_new, jnp.max(s, axis=1, keepdims=True))
    alpha = jnp.exp(m_prev - m_new)
    l_new = alpha * l_ref[...]
    acc = alpha * acc_ref[...]
    for j in range(pp):
        pm = jnp.exp(scores[j] - m_new)
        l_new = l_new + jnp.sum(pm, axis=1, keepdims=True)
        v2 = vc_refs[j][0, 0].reshape(page * h_b, dv).astype(BF16)
        acc = acc + jnp.dot(pm.astype(BF16), v2, preferred_element_type=F32)
    l_ref[...] = l_new
    acc_ref[...] = acc
    m_ref[...] = m_new

    @pl.when(p == pl.num_programs(1) - 1)
    def _():
        lam = _diff_lambda(lamp_ref, lam_init)
        for h in range(h_b):
            o1 = acc_ref[2 * h:2 * h + 1, :] / l_ref[2 * h:2 * h + 1, :]
            o2 = acc_ref[2 * h + 1:2 * h + 2, :] / l_ref[2 * h + 1:2 * h + 2, :]
            o = o1 - lam * o2
            ms = jnp.mean(o * o, axis=-1, keepdims=True)
            o_ref[0, :, h * dv:(h + 1) * dv] = (o * lax.rsqrt(ms + SUBLN_EPS) * g_ref[...]
                                                * (1.0 - lam_init)).astype(o_ref.dtype)


def _attn_decode(qn, kn, vn, cache_k, cache_v, page_table, lam_params, subln_g, h_b, dq,
                 lam_init, lyr):
    b = qn.shape[0]
    n_pages = page_table.shape[1]
    page = cache_k.shape[2]
    dv = 2 * dq
    nsub = 2 * h_b
    pp = _tile(n_pages, TILES["decode_pages"], 1)
    zeros = jnp.zeros((b, h_b, dq), F32)

    def sub_rows(x):
        x4 = x.reshape(b, h_b, 2, dq)
        return jnp.stack([jnp.concatenate([x4[:, :, 0], zeros], axis=-1),
                          jnp.concatenate([zeros, x4[:, :, 1]], axis=-1)], axis=2).reshape(b, nsub, dv)

    qm = sub_rows(qn)
    km = sub_rows(kn)
    vm = jnp.repeat(vn.reshape(b, h_b, 1, dv), 2, axis=2).reshape(b, nsub, dv)
    rows = lambda: pl.BlockSpec((1, nsub, dv), lambda i, p, pt: (i, 0, 0))

    def pages(j):
        return pl.BlockSpec((1, 1, page, h_b, dv), lambda i, p, pt: (lyr, pt[i, p * pp + j], 0, 0, 0))

    grid_spec = pltpu.PrefetchScalarGridSpec(
        num_scalar_prefetch=1,
        grid=(b, n_pages // pp),
        in_specs=([rows(), rows(), rows()] + [pages(j) for j in range(pp)]
                  + [pages(j) for j in range(pp)]
                  + [pl.BlockSpec((4, dq), lambda i, p, pt: (0, 0)),
                     pl.BlockSpec((1, dv), lambda i, p, pt: (0, 0))]),
        out_specs=pl.BlockSpec((1, 1, h_b * dv), lambda i, p, pt: (i, 0, 0)),
        scratch_shapes=[pltpu.VMEM((nsub, 1), F32), pltpu.VMEM((nsub, 1), F32),
                        pltpu.VMEM((nsub, dv), F32)],
    )
    return pl.pallas_call(
        functools.partial(_attn_decode_kernel, h_b=h_b, dq=dq, lam_init=lam_init, pp=pp),
        grid_spec=grid_spec,
        out_shape=jax.ShapeDtypeStruct((b, 1, h_b * dv), BF16),
        compiler_params=_params("parallel", "arbitrary"),
        name="attn_decode",
    )(page_table, qm, km, vm, *([cache_k] * pp), *([cache_v] * pp), lam_params, subln_g)


def _ep_plain(accs, e, o, carry, first, last):
    o[0][...] = accs[0].astype(o[0].dtype)


def _ep_bias(accs, e, o, carry, first, last):
    o[0][...] = accs[0] + e[0][...]


def _ep_silu(accs, e, o, carry, first, last):
    o[0][...] = _silu(accs[0]).astype(o[0].dtype)


def _ep_sigmoid(accs, e, o, carry, first, last):
    o[0][...] = _sigmoid(accs[0]).astype(o[0].dtype)


def _group_norm_store(y, gain, eps, out_refs):
    tn = y.shape[1]
    for g in range(tn // LANES):
        sl = slice(g * LANES, (g + 1) * LANES)
        blk = y[:, sl]
        ms = jnp.sum(blk * blk, axis=-1, keepdims=True) / LANES
        val = blk * lax.rsqrt(ms + eps) * gain[:, sl]
        for r in out_refs:
            r[:, sl] = val.astype(r.dtype)


def _ep_qknorm(accs, e, o, carry, first, last):
    _group_norm_store(accs[0], e[0][...], EPS, o)


def _ep_copy2(accs, e, o, carry, first, last):
    for r in o:
        r[...] = accs[0].astype(r.dtype)


def _sample_post_kernel(q_ref, k_ref, ga_ref, gb_ref, gq_ref, gk_ref, qo_ref, ko_ref, sa_ref, sb_ref):
    _group_norm_store(q_ref[...], gq_ref[...], EPS, [qo_ref])
    _group_norm_store(k_ref[...], gk_ref[...], EPS, [ko_ref])
    sa_ref[...] = _sigmoid(ga_ref[...])
    sb_ref[...] = _sigmoid(gb_ref[...])


def _ep_gdn_conv(accs, e, o, carry, first, last, *, width):
    raw = accs[0]
    taps = [e[j][...] for j in range(width)]
    nflag = e[width][...]
    gain = e[width + 1][...]
    val = _silu(_conv_taps(raw, carry[0], taps, first))
    for g in range(val.shape[1] // LANES):
        sl = slice(g * LANES, (g + 1) * LANES)
        blk = val[:, sl]
        ss = jnp.sum(blk * blk, axis=-1, keepdims=True)
        scale = jnp.where(nflag[:, sl] > 0.5, lax.rsqrt(ss + EPS), 1.0) * gain[:, sl]
        o[0][:, sl] = (blk * scale).astype(o[0].dtype)
    if last:
        o[1][...] = carry[0][0:SUBLANES, :]


def _ep_branch(accs, e, o, carry, first, last):
    o[0][...] = (e[0][...].astype(F32) * accs[0] + e[1][...].astype(F32) * accs[1]).astype(o[0].dtype)


def _ep_residual(accs, e, o, carry, first, last):
    o[0][...] = e[0][...] + e[1][...] * accs[0]


def _ep_ffn_prompt(accs, e, o, carry, first, last, *, width):
    outs = []
    for d in range(2):
        taps = [e[d * (width + 1) + j][...] for j in range(width)]
        bias = e[d * (width + 1) + width][...]
        outs.append(_conv_taps(accs[d], carry[d], taps, first) + bias)
        if last:
            o[1 + d][...] = carry[d][0:SUBLANES, :]
    o[0][...] = (_silu(outs[0]) * outs[1]).astype(o[0].dtype)


def _sample_mix_kernel(ra_ref, rb_ref, ga_ref, gb_ref, o_ref):
    o_ref[...] = (ga_ref[...] * ra_ref[...] + gb_ref[...] * rb_ref[...]).astype(o_ref.dtype)


def _resid_normmod_kernel(x_ref, r_ref, gate_ref, g_ref, sc_ref, sh_ref, x2_ref, h_ref):
    x2 = x_ref[...] + gate_ref[...] * r_ref[...]
    x2_ref[...] = x2
    ms = jnp.mean(x2 * x2, axis=-1, keepdims=True)
    h = x2 * lax.rsqrt(ms + EPS) * g_ref[...]
    h_ref[...] = (h * (1.0 + sc_ref[...]) + sh_ref[...]).astype(h_ref.dtype)


def _ffn_sample_kernel(*refs, width):
    ns = width - 1
    ups = refs[0:2]
    states = (refs[2:2 + ns], refs[2 + ns:2 + 2 * ns])
    taps = refs[2 + 2 * ns:4 + 2 * ns]
    bias = refs[4 + 2 * ns:6 + 2 * ns]
    o_ref = refs[6 + 2 * ns]
    vals = []
    for d in range(2):
        y = ups[d][...] * taps[d][width - 1:width, :]
        for j in range(ns):
            y = y + states[d][j][...] * taps[d][j:j + 1, :]
        vals.append(y + bias[d][...])
    o_ref[...] = (_silu(vals[0]) * vals[1]).astype(o_ref.dtype)


def kernel(x_prompt, x_sample, c_prompt, c_sample, cache_k, cache_v, state_gdn, state_gdn_conv, state_ffn_conv, page_table, w_ada, b_ada, norm1_g, norm2_g, w_in, gdn_conv_w, gdn_a_log, gdn_dt_bias, gdn_norm_g, diff_q_norm_g, diff_k_norm_g, diff_lambda, diff_subln_g, w_branch_a, w_branch_b, w_o, w_up, ffn_conv_w, ffn_conv_b, w_down):
    depth = w_in.shape[0]
    assert depth == 1 and x_prompt.shape[0] == 1 and x_sample.shape[1] == 1
    lyr = 0
    lam_init = 0.8 - 0.6 * math.exp(-0.3 * lyr)
    d = x_prompt.shape[-1]
    l = x_prompt.shape[1]
    nb = x_sample.shape[0]
    h_a, dk, dva = state_gdn.shape[2:]
    h_b = cache_k.shape[3]
    dq = cache_k.shape[4] // 2
    dvb = cache_v.shape[4]
    assert dk == LANES and dva == LANES and dq == LANES and dvb == 2 * dq
    d_ff = w_down.shape[1]
    gw = gdn_conv_w.shape[1]
    fw = ffn_conv_w.shape[1]
    hd_a = h_a * dk
    hd_b = h_b * dvb

    w_t = jnp.swapaxes(w_in[lyr], 0, 1)
    c_za = 3 * hd_a
    c_ba = c_za + hd_a
    n_head = c_ba + LANES
    c_qb = c_ba + 2 * h_a
    c_kb, c_vb, c_ga, c_gb = c_qb + hd_b, c_qb + 2 * hd_b, c_qb + 3 * hd_b, c_qb + 3 * hd_b + d
    n_tail = 3 * hd_b + 2 * d
    assert c_qb + n_tail == w_t.shape[0] and 2 * h_a <= LANES
    t_qb, t_kb, t_vb, t_ga, t_gb = 0, hd_b, 2 * hd_b, 3 * hd_b, 3 * hd_b + d
    w_up0 = w_up[lyr]
    w_dn = w_down[lyr].astype(BF16)
    w_o_b = w_o[lyr]
    w_ba_b = w_branch_a[lyr]
    w_bb_b = w_branch_b[lyr]

    row = lambda v: v.reshape(1, -1).astype(F32)
    tile_rows = lambda v, reps: jnp.tile(row(v), (1, reps))

    c_all = jnp.concatenate([c_prompt, c_sample], axis=0).astype(F32)
    mp = -(-c_all.shape[0] // 16) * 16
    c_all = jnp.pad(c_all, ((0, mp - c_all.shape[0]), (0, 0)))
    (mod,) = _matmul("adaln", [c_all], [(w_ada[lyr], 0)], [(0, 0)], 6 * d, _ep_bias, [(F32, "tile")],
                     tm=mp, tn=_tile(6 * d, 512, LANES), extras=[(row(b_ada[lyr]), "row", 0)],
                     a_fn=_silu)
    mod_p = [mod[0:1, j * d:(j + 1) * d] for j in range(6)]
    mod_s = [mod[1:1 + nb, j * d:(j + 1) * d] for j in range(6)]

    gq = tile_rows(diff_q_norm_g[lyr], 2 * h_b) * (dq ** -0.5)
    gk = tile_rows(diff_k_norm_g[lyr], 2 * h_b)
    gq2 = gq * math.log2(math.e)
    score_bound = dq * jnp.max(jnp.abs(gq2)) * jnp.max(jnp.abs(gk))
    cw = gdn_conv_w[lyr].astype(F32)
    conv_nflag = jnp.concatenate([jnp.ones((1, 2 * hd_a), F32), jnp.zeros((1, hd_a), F32)], axis=1)
    conv_gain = jnp.concatenate([jnp.full((1, hd_a), dk ** -0.5, F32), jnp.ones((1, 2 * hd_a), F32)], axis=1)
    gnorm = row(gdn_norm_g[lyr])
    subln = row(diff_subln_g[lyr])
    lamp = diff_lambda[lyr].astype(F32)
    fcw = ffn_conv_w[lyr].astype(F32)
    fcb = row(ffn_conv_b[lyr])
    a_log = gdn_a_log[lyr].astype(F32)
    dt_bias = gdn_dt_bias[lyr].astype(F32)

    xp = x_prompt[0].astype(F32)
    tm = _tile(l, TILES["mm_m"], 16)
    tn = TILES["mm_n"]
    split = 4 if tm % 64 == 0 else 1
    split_ffn = 8 if tm % 128 == 0 else split
    h1 = _normmod(xp, row(norm1_g[lyr]), mod_p[1], mod_p[0])
    xs = x_sample[:, 0, :].astype(F32)
    h1s = _normmod(xs, row(norm1_g[lyr]), mod_s[1], mod_s[0])

    stationary = dict(tm=tm, order="nm", cast_once=True, row_split=split, wt=True, side=h1s)
    qkv_c, qkv_tail, zq_s = _matmul(
        "p_qkv_conv", [h1], [(w_t, 0)], [(0, 0)], 3 * hd_a,
        functools.partial(_ep_gdn_conv, width=gw), [(BF16, "tile"), (F32, "tail")],
        tn=_tile(hd_a, tn, LANES), carry_rows=1, **stationary,
        extras=[(cw[j:j + 1], "row", 0) for j in range(gw)] + [(conv_nflag, "row", 0), (conv_gain, "row", 0)])
    za_p, za_s = _matmul("p_za", [h1], [(w_t, c_za)], [(0, 0)], hd_a, _ep_silu, [(BF16, "tile")],
                         tn=_tile(hd_a, tn, LANES), **stationary)
    tn_b = _tile(hd_b, tn, LANES)
    qb_p, qb_s = _matmul("p_qb", [h1], [(w_t, c_qb)], [(0, 0)], hd_b, _ep_qknorm, [(BF16, "tile")],
                         tn=tn_b, extras=[(gq2, "row", 0)], **stationary)
    kb_p, kb_p16, kb_s = _matmul("p_kb", [h1], [(w_t, c_kb)], [(0, 0)], hd_b, _ep_qknorm,
                                 [(F32, "tile"), (BF16, "tile")], tn=tn_b, extras=[(gk, "row", 0)],
                                 **stationary)
    vb_p, vb_p16, vb_s = _matmul("p_vb", [h1], [(w_t, c_vb)], [(0, 0)], hd_b, _ep_copy2,
                                 [(F32, "tile"), (BF16, "tile")], tn=tn_b, **stationary)
    tn_d = _tile(d, tn, LANES)
    gates_p, gates_s = _matmul("p_gates", [h1], [(w_t, c_ga)], [(0, 0)], 2 * d, _ep_sigmoid,
                               [(BF16, "tile")], tn=tn_d, **stationary)

    oa_p, s_p, ba_s = _gdn_prompt(qkv_c, h1, h1s, w_t, c_ba, za_p, a_log, dt_bias, gnorm, h_a, dk)
    ob_p = _attn_prompt(qb_p, kb_p16, vb_p16, score_bound, lamp, subln, h_b, dq, lam_init)

    tns = TILES["small_n"]
    oa_s, s_s, gbuf_s = _gdn_sample(zq_s, za_s, ba_s, state_gdn_conv[lyr].astype(F32),
                                    state_gdn[lyr].astype(F32), cw, a_log, dt_bias, gnorm, h_a, dk)
    qn_s, kn_s, sga_s, sgb_s = pl.pallas_call(
        _sample_post_kernel,
        out_shape=[jax.ShapeDtypeStruct((nb, hd_b), F32), jax.ShapeDtypeStruct((nb, hd_b), F32),
                   jax.ShapeDtypeStruct((nb, d), F32), jax.ShapeDtypeStruct((nb, d), F32)],
        name="sample_post",
    )(qb_s, kb_s, gates_s[:, :d], gates_s[:, d:], gq, gk)
    ob_s = _attn_decode(qn_s, kn_s, vb_s, cache_k, cache_v, page_table.astype(jnp.int32), lamp, subln,
                        h_b, dq, lam_init, lyr)
    tn_ds = _tile(d, tns, LANES)

    mixed_p, ra_s, rb_s = _matmul(
        "p_branch", [oa_p, ob_p], [(w_ba_b, 0), (w_bb_b, 0)], [(0, 0), (1, 1)], d,
        _ep_branch, [(BF16, "tile")], tm=tm, tn=tn_d, row_split=split, order="nm", cast_once=True,
        extras=[(gates_p, "tile", 0), (gates_p, "tile", d)], side=[oa_s[:, 0, :], ob_s[:, 0, :]])
    mixed_s = pl.pallas_call(
        _sample_mix_kernel, out_shape=jax.ShapeDtypeStruct((nb, d), BF16), name="s_mix",
    )(ra_s, rb_s, sga_s, sgb_s)
    x2_p, ro_s = _matmul(
        "p_wo", [mixed_p], [(w_o_b, 0)], [(0, 0)], d, _ep_residual, [(F32, "tile")],
        tm=tm, tn=tn_d, row_split=split, order="nm", cast_once=True,
        extras=[(xp, "tile", 0), (mod_p[2], "row", 0)], side=mixed_s)
    x2_s, h2s = pl.pallas_call(
        _resid_normmod_kernel,
        out_shape=[jax.ShapeDtypeStruct((nb, d), F32), jax.ShapeDtypeStruct((nb, d), BF16)],
        name="s_resid_norm",
    )(xs, ro_s, mod_s[2], row(norm2_g[lyr]), mod_s[4], mod_s[3])
    h2 = _normmod(x2_p, row(norm2_g[lyr]), mod_p[4], mod_p[3])
    tn_u = _tile(d_ff, TILES["up_n"], LANES)
    assert d_ff % tn_u == 0
    ffn_rows = ([(fcw[j:j + 1], "row", 0) for j in range(fw)] + [(fcb, "row", 0)]
                + [(fcw[j:j + 1], "row", d_ff) for j in range(fw)] + [(fcb, "row", d_ff)])
    act_f, tail_g, tail_v, up_g, up_v = _matmul(
        "p_ffn_up", [h2], [(w_up0, 0), (w_up0, d_ff)], [(0, 0), (0, 1)], d_ff,
        functools.partial(_ep_ffn_prompt, width=fw),
        [(BF16, "tile"), (F32, "tail"), (F32, "tail")],
        tm=tm, tn=tn_u, order="nm", carry_rows=2, cast_once=True, row_split=split_ffn, extras=ffn_rows,
        side=h2s)
    (y_p,) = _matmul("p_down", [act_f], [(w_dn, 0)], [(0, 0)], d, _ep_residual, [(F32, "tile")],
                     tm=tm, tn=tn_d, tk=_tile(d_ff, TILES["down_k"], LANES),
                     extras=[(x2_p, "tile", 0), (mod_p[5], "row", 0)])

    fbuf = state_ffn_conv[lyr].astype(F32)
    tn_f = _tile(d_ff, 8192, LANES)
    nfb = d_ff // tn_f
    half = lambda rows, v: pl.BlockSpec((rows, tn_f), lambda j: (0, j + v * nfb))
    act_s = pl.pallas_call(
        functools.partial(_ffn_sample_kernel, width=fw),
        grid=(nfb,),
        in_specs=([half(nb, 0), half(nb, 0)]
                  + [half(nb, 0) for _ in range(fw - 1)] + [half(nb, 1) for _ in range(fw - 1)]
                  + [half(fw, 0), half(fw, 1), half(1, 0), half(1, 1)]),
        out_specs=half(nb, 0),
        out_shape=jax.ShapeDtypeStruct((nb, d_ff), BF16),
        compiler_params=_params("parallel"),
        name="s_ffn_conv",
    )(up_g, up_v, *[fbuf[:, j, :] for j in range(fw - 1)], *[fbuf[:, j, :] for j in range(fw - 1)],
      fcw, fcw, fcb, fcb)
    (y_s,) = _matmul("s_down", [act_s], [(w_dn, 0)], [(0, 0)], d, _ep_residual, [(F32, "tile")],
                     tm=nb, tn=tn_ds, tk=_tile(d_ff, TILES["down_k"], LANES),
                     extras=[(x2_s, "tile", 0), (mod_s[5], "tile", 0)])

    y_prompt = y_p[None].astype(x_prompt.dtype)
    y_sample = y_s[:, None, :].astype(x_sample.dtype)
    k_prompt = kb_p.reshape(1, 1, l, h_b, 2 * dq)
    v_prompt = vb_p.reshape(1, 1, l, h_b, dvb)
    gdn_state_prompt = s_p[None, None]
    gdn_conv_prompt = qkv_tail[SUBLANES - (gw - 1):][None, None]
    ffn_conv_prompt = jnp.concatenate([tail_g[SUBLANES - (fw - 1):], tail_v[SUBLANES - (fw - 1):]],
                                      axis=1)[None, None]
    k_sample = kn_s.reshape(1, nb, 1, h_b, 2 * dq)
    v_sample = vb_s.reshape(1, nb, 1, h_b, dvb)
    gdn_state_sample = s_s[None]
    gdn_conv_sample = gbuf_s[None]
    up_new = jnp.concatenate([up_g, up_v], axis=1)
    ffn_conv_sample = jnp.concatenate([fbuf[:, 1:, :], up_new[:, None, :]], axis=1)[None]
    return (y_prompt, y_sample, k_prompt, v_prompt, gdn_state_prompt, gdn_conv_prompt,
            ffn_conv_prompt, k_sample, v_sample, gdn_state_sample, gdn_conv_sample, ffn_conv_sample)
```

```python
import functools
import math

import jax
import jax.numpy as jnp
from jax import lax
from jax.experimental import pallas as pl
from jax.experimental.pallas import tpu as pltpu

F32 = jnp.float32
BF16 = jnp.bfloat16

LANES = 128
SUBLANES = 8
VMEM_LIMIT_BYTES = 56 * 1024 * 1024

EPS = 1e-6
SUBLN_EPS = 1e-5
GDN_CHUNK = 128
MAX_UNSHIFTED_SCORE = 40.0
NEG_BIG = -1e30

TILES = dict(
    norm_rows=512,
    mm_m=1024, mm_n=512,
    up_n=256,
    down_k=5504, down_m=512,
    attn=1024,
    gdn_heads=8,
    prep_rows=512,
    decode_pages=8,
    small_n=1024,
)

NT_DIMS = (((1,), (1,)), ((), ()))


def _params(*sem):
    return pltpu.CompilerParams(dimension_semantics=sem, vmem_limit_bytes=VMEM_LIMIT_BYTES)


def _tile(dim, pref, align):
    if dim <= pref:
        return dim
    t = (pref // align) * align
    while t >= align:
        if dim % t == 0:
            return t
        t -= align
    return dim


def _sigmoid(x):
    return 1.0 / (1.0 + jnp.exp(-x))


def _silu(x):
    return x * _sigmoid(x)


def _softplus(x):
    return jnp.maximum(x, 0.0) + jnp.log1p(jnp.exp(-jnp.abs(x)))


def _normmod_kernel(x_ref, g_ref, sc_ref, sh_ref, o_ref):
    x = x_ref[...]
    ms = jnp.mean(x * x, axis=-1, keepdims=True)
    h = x * lax.rsqrt(ms + EPS) * g_ref[...]
    o_ref[...] = (h * (1.0 + sc_ref[...]) + sh_ref[...]).astype(o_ref.dtype)


def _normmod(x, g, sc, sh):
    m, d = x.shape
    tm = _tile(m, TILES["norm_rows"], SUBLANES)
    per_row = sc.shape[0] != 1
    mod_spec = (pl.BlockSpec((tm, d), lambda i: (i, 0)) if per_row
                else pl.BlockSpec((1, d), lambda i: (0, 0)))
    return pl.pallas_call(
        _normmod_kernel,
        grid=(m // tm,),
        in_specs=[pl.BlockSpec((tm, d), lambda i: (i, 0)),
                  pl.BlockSpec((1, d), lambda i: (0, 0)), mod_spec, mod_spec],
        out_specs=pl.BlockSpec((tm, d), lambda i: (i, 0)),
        out_shape=jax.ShapeDtypeStruct((m, d), BF16),
        compiler_params=_params("parallel"),
        name="normmod",
    )(x, g, sc, sh)


def _matmul(name, as_, ws, dots, n, epilogue, outs, *, tm, tn, extras=(), order="mn",
            tk=None, a_fn=None, carry_rows=0, cast_once=False, row_split=1, wt=False, side=None):
    m = as_[0].shape[0]
    kdim = as_[0].shape[1]
    tk = kdim if tk is None else tk
    nk = kdim // tk
    assert kdim % tk == 0 and m % tm == 0 and n % tn == 0 and tm % row_split == 0
    if nk > 1:
        assert all(a.shape[1] == kdim for a in as_) and row_split == 1 and not cast_once
    assert not cast_once or order == "nm"
    ni, nj = m // tm, n // tn
    na, nw, ne, no, nd = len(as_), len(ws), len(extras), len(outs), len(dots)
    rows = tm // row_split

    if order == "mn":
        grid = (ni, nj, nk)
        ij = lambda g0, g1: (g0, g1)
    else:
        grid = (nj, ni, nk)
        ij = lambda g0, g1: (g1, g0)

    def a_map(g0, g1, k):
        return (ij(g0, g1)[0], k)

    def w_map(off):
        return lambda g0, g1, k: (k, ij(g0, g1)[1] + off // tn)

    def row_map(off):
        return lambda g0, g1, k: (0, ij(g0, g1)[1] + off // tn)

    def tile_map(off):
        return lambda g0, g1, k: (ij(g0, g1)[0], ij(g0, g1)[1] + off // tn)

    in_specs = []
    for a in as_:
        in_specs.append(pl.BlockSpec((tm, tk if nk > 1 else a.shape[1]), a_map))
    for w, off in ws:
        if wt:
            assert nk == 1 and off % SUBLANES == 0
            in_specs.append(pl.BlockSpec(
                (pl.Element(tn), pl.Element(w.shape[1])),
                functools.partial(
                    lambda g0, g1, k, off: (pl.multiple_of(off + ij(g0, g1)[1] * tn, SUBLANES), 0),
                    off=off)))
        else:
            assert off % tn == 0
            in_specs.append(pl.BlockSpec((tk if nk > 1 else w.shape[0], tn), w_map(off)))
    for arr, kind, off in extras:
        assert off % tn == 0
        if kind == "row":
            in_specs.append(pl.BlockSpec((1, tn), row_map(off)))
        else:
            in_specs.append(pl.BlockSpec((tm, tn), tile_map(off)))
    out_specs, out_shapes = [], []
    for dtype, kind in outs:
        if kind == "tile":
            out_specs.append(pl.BlockSpec((tm, tn), tile_map(0)))
            out_shapes.append(jax.ShapeDtypeStruct((m, n), dtype))
        else:
            out_specs.append(pl.BlockSpec((SUBLANES, tn), row_map(0)))
            out_shapes.append(jax.ShapeDtypeStruct((SUBLANES, n), dtype))
    nside = 0
    sides = []
    if side is not None:
        sides = list(side) if isinstance(side, (list, tuple)) else [side] * nw
        assert cast_once and len(sides) == nw
        nside = nw
        for sd in sides:
            in_specs.append(pl.BlockSpec(sd.shape, lambda g0, g1, k: (0, 0)))
        for sd in sides:
            out_specs.append(pl.BlockSpec((sd.shape[0], tn), row_map(0)))
            out_shapes.append(jax.ShapeDtypeStruct((sd.shape[0], n), F32))
    scratch = []
    if nk > 1:
        scratch += [pltpu.VMEM((tm, tn), F32) for _ in range(nd)]
    if cast_once:
        scratch += [pltpu.VMEM((tn, w.shape[1]) if wt else (w.shape[0], tn), BF16) for w, _ in ws]
    scratch += [pltpu.VMEM((SUBLANES + rows, tn), F32) for _ in range(carry_rows)]

    def body(*refs):
        a_refs = refs[:na]
        w_refs = refs[na:na + nw]
        e_refs = refs[na + nw:na + nw + ne]
        nin = na + nw + ne + nside
        side_refs = refs[nin - nside:nin]
        o_refs = refs[nin:nin + no]
        side_out = refs[nin + no:nin + no + nside]
        s_refs = list(refs[nin + no + nside:])
        acc_refs = [s_refs.pop(0) for _ in range(nd)] if nk > 1 else []
        wb_refs = [s_refs.pop(0) for _ in range(nw)] if cast_once else list(w_refs)
        carry_refs = s_refs
        i, _ = ij(pl.program_id(0), pl.program_id(1))
        k = pl.program_id(2)

        def partial(d, r):
            ai, wi = dots[d]
            a = a_refs[ai][r * rows:(r + 1) * rows, :] if row_split > 1 else a_refs[ai][...]
            if a_fn is not None:
                a = a_fn(a)
            w = wb_refs[wi][...].astype(BF16)
            if wt:
                return lax.dot_general(a.astype(BF16), w, NT_DIMS, preferred_element_type=F32)
            return jnp.dot(a.astype(BF16), w, preferred_element_type=F32)

        if cast_once:
            @pl.when(i == 0)
            def _():
                for wi in range(nw):
                    wb_refs[wi][...] = w_refs[wi][...].astype(BF16)
                for wi in range(nside):
                    wb = wb_refs[wi][...]
                    sd = side_refs[wi][...]
                    side_out[wi][...] = (
                        lax.dot_general(sd, wb, NT_DIMS, preferred_element_type=F32)
                        if wt else jnp.dot(sd, wb, preferred_element_type=F32))

        if nk == 1:
            for r in range(row_split):
                if row_split > 1:
                    piece = pl.ds(r * rows, rows)
                    ev = [e.at[piece] if extras[x][1] == "tile" else e for x, e in enumerate(e_refs)]
                    ov = [o.at[piece] if outs[x][1] == "tile" else o for x, o in enumerate(o_refs)]
                else:
                    ev, ov = e_refs, o_refs
                epilogue([partial(d, r) for d in range(nd)], ev, ov, carry_refs,
                         (i == 0) if r == 0 else False, r == row_split - 1)
        else:
            @pl.when(k == 0)
            def _():
                for d in range(nd):
                    acc_refs[d][...] = jnp.zeros((tm, tn), F32)

            for d in range(nd):
                acc_refs[d][...] += partial(d, 0)

            @pl.when(k == nk - 1)
            def _():
                epilogue([acc_refs[d][...] for d in range(nd)], e_refs, o_refs,
                         carry_refs, i == 0, True)

    return pl.pallas_call(
        body,
        grid=grid,
        in_specs=in_specs,
        out_specs=out_specs,
        out_shape=out_shapes,
        scratch_shapes=scratch,
        compiler_params=_params("arbitrary", "arbitrary", "arbitrary"),
        name=name,
    )(*as_, *[w for w, _ in ws], *[e for e, _, _ in extras], *sides)


def _conv_taps(raw, win_ref, taps, first):
    width = len(taps)
    rows = raw.shape[0]
    if first is not False:
        @pl.when(first)
        def _():
            win_ref[0:SUBLANES, :] = jnp.zeros((SUBLANES, raw.shape[1]), F32)

    win_ref[SUBLANES:SUBLANES + rows, :] = raw
    y = win_ref[SUBLANES:SUBLANES + rows, :] * taps[width - 1]
    for s in range(1, width):
        y = y + win_ref[SUBLANES - s:SUBLANES - s + rows, :] * taps[width - 1 - s]
    win_ref[0:SUBLANES, :] = win_ref[rows:rows + SUBLANES, :]
    return y


def _gdn_gates(ba, alog_row, dtb_row):
    return _sigmoid(ba), -jnp.exp(alog_row) * _softplus(ba + dtb_row)


def _gdn_prep_kernel(h_ref, hs_ref, wba_ref, alog_ref, dtb_ref, beta_ref, gc_ref, gct_ref, bas_ref,
                     *, heads_pad, h_a):
    c = GDN_CHUNK
    wba = wba_ref[...].astype(BF16)

    @pl.when(pl.program_id(0) == 0)
    def _():
        bas_ref[...] = lax.dot_general(hs_ref[...], wba, NT_DIMS, preferred_element_type=F32)

    ba = lax.dot_general(h_ref[...], wba, NT_DIMS, preferred_element_type=F32)
    beta, g = _gdn_gates(ba, alog_ref[...], dtb_ref[...])
    beta_ref[...] = beta
    row = lax.broadcasted_iota(jnp.int32, (c, LANES), 0)
    for q in range(g.shape[0] // c):
        gq = g[q * c:(q + 1) * c, :]
        s = 1
        while s < c:
            gq = gq + jnp.where(row >= s, pltpu.roll(gq, s, axis=0), 0.0)
            s *= 2
        gc_ref[q * c:(q + 1) * c, :] = gq
        gct_ref[:, q * c:(q + 1) * c] = gq.T[h_a:h_a + heads_pad, :]


def _gdn_intra_kernel(q_ref, k_ref, v_ref, beta_ref, gc_ref, gct_ref,
                      uv_ref, w_ref, aqk_ref, qdec_ref, kdect_ref, *, hb, h_a):
    c = GDN_CHUNK
    grp = pl.program_id(1)
    lane = lax.broadcasted_iota(jnp.int32, (c, LANES), 1)
    row = lax.broadcasted_iota(jnp.int32, (c, c), 0)
    col = lax.broadcasted_iota(jnp.int32, (c, c), 1)
    eye = jnp.where(row == col, 1.0, 0.0)
    levels = range(1, int(math.log2(c)))
    below = [((row >> (lvl + 1)) == (col >> (lvl + 1))) & ((row >> lvl) != (col >> lvl))
             for lvl in levels]
    pair = (row >> 1) == (col >> 1)
    beta_all = beta_ref[...]
    gc_all = gc_ref[...]
    heads = range(hb)
    sls = [slice(j * LANES, (j + 1) * LANES) for j in heads]
    beta_c, gcc, kb, mm, t = [], [], [], [], []
    for j in heads:
        h = grp * hb + j
        beta_c.append(jnp.sum(jnp.where(lane == h, beta_all, 0.0), axis=1, keepdims=True))
        gcc.append(jnp.sum(jnp.where(lane == h + h_a, gc_all, 0.0), axis=1, keepdims=True))
        gcr = gct_ref[pl.ds(h, 1), :]
        dec_incl = jnp.exp(jnp.where(row >= col, gcc[j] - gcr, NEG_BIG))
        k = k_ref[:, sls[j]]
        kb.append(k.astype(F32) * beta_c[j])
        mm.append(lax.dot_general(kb[j].astype(BF16), k, NT_DIMS, preferred_element_type=F32)
                  * jnp.where(row > col, dec_incl, 0.0))
        aqk = lax.dot_general(q_ref[:, sls[j]], k, NT_DIMS, preferred_element_type=F32) * dec_incl
        aqk_ref[:, sls[j]] = aqk.astype(BF16)
        t.append(eye - jnp.where(pair, mm[j], 0.0))
    for msk in below:
        tb = [t[j].astype(BF16) for j in heads]
        bt = [jnp.dot(jnp.where(msk, mm[j], 0.0).astype(BF16), tb[j], preferred_element_type=F32)
              for j in heads]
        t = [t[j] - jnp.dot(tb[j], bt[j].astype(BF16), preferred_element_type=F32) for j in heads]
    for j in heads:
        tb = t[j].astype(BF16)
        egc = jnp.exp(gcc[j])
        gl = gcc[j][c - 1:c, :]
        vb = v_ref[:, sls[j]].astype(F32) * beta_c[j]
        uv_ref[:, sls[j]] = jnp.dot(tb, vb.astype(BF16), preferred_element_type=F32)
        w_ref[:, sls[j]] = jnp.dot(tb, (kb[j] * egc).astype(BF16),
                                   preferred_element_type=F32).astype(BF16)
        qdec_ref[:, sls[j]] = (q_ref[:, sls[j]].astype(F32) * egc).astype(BF16)
        kdect_ref[sls[j], :] = (k_ref[:, sls[j]].astype(F32) * jnp.exp(gl - gcc[j])).T.astype(BF16)


def _gdn_rec_kernel(uv_ref, w_ref, aqk_ref, qdec_ref, kdect_ref, gct_ref, za_ref, gn_ref,
                    o_ref, sout_ref, s_ref, *, hb):
    c = GDN_CHUNK
    n = pl.program_id(1)
    g0 = pl.program_id(0)

    @pl.when(n == 0)
    def _():
        s_ref[...] = jnp.zeros(s_ref.shape, F32)

    heads = range(hb)
    sls = [slice(j * LANES, (j + 1) * LANES) for j in heads]
    s = [s_ref[j] for j in heads]
    sb = [s[j].astype(BF16) for j in heads]
    ub = [(uv_ref[:, sls[j]] - jnp.dot(w_ref[:, sls[j]], sb[j], preferred_element_type=F32)
           ).astype(BF16) for j in heads]
    oq = [jnp.dot(qdec_ref[:, sls[j]], sb[j], preferred_element_type=F32) for j in heads]
    for j in heads:
        gl = jnp.exp(gct_ref[pl.ds(g0 * hb + j, 1), c - 1:c])
        s_ref[j] = gl * s[j] + jnp.dot(kdect_ref[sls[j], :], ub[j], preferred_element_type=F32)
    for j in heads:
        o = oq[j] + jnp.dot(aqk_ref[:, sls[j]], ub[j], preferred_element_type=F32)
        ms = jnp.mean(o * o, axis=-1, keepdims=True)
        o_ref[:, sls[j]] = (o * lax.rsqrt(ms + EPS) * gn_ref[...]
                            * za_ref[:, sls[j]].astype(F32)).astype(o_ref.dtype)

    @pl.when(n == pl.num_programs(1) - 1)
    def _():
        sout_ref[...] = s_ref[...]


def _gate_rows(a_log, dt_bias, h_a):
    alog_row = jnp.zeros((1, LANES), F32).at[0, h_a:2 * h_a].set(a_log)
    dtb_row = jnp.zeros((1, LANES), F32).at[0, h_a:2 * h_a].set(dt_bias)
    return alog_row, dtb_row


def _gdn_prompt(qkv, h1, h1s, w_t, c_ba, za_silu, a_log, dt_bias, gnorm, h_a, dk):
    l = qkv.shape[0]
    kdim = h1.shape[1]
    c = GDN_CHUNK
    assert l % c == 0 and dk == LANES and 2 * h_a <= LANES and c_ba % LANES == 0
    nchunk = l // c
    hp = max(SUBLANES, -(-h_a // SUBLANES) * SUBLANES)
    alog_row, dtb_row = _gate_rows(a_log, dt_bias, h_a)

    ms = h1s.shape[0]
    rp = _tile(l, TILES["prep_rows"], c)
    beta, gc, gct, ba_s = pl.pallas_call(
        functools.partial(_gdn_prep_kernel, heads_pad=hp, h_a=h_a),
        grid=(l // rp,),
        in_specs=[pl.BlockSpec((rp, kdim), lambda n: (n, 0)),
                  pl.BlockSpec((ms, kdim), lambda n: (0, 0)),
                  pl.BlockSpec((LANES, kdim), lambda n: (c_ba // LANES, 0)),
                  pl.BlockSpec((1, LANES), lambda n: (0, 0)),
                  pl.BlockSpec((1, LANES), lambda n: (0, 0))],
        out_specs=[pl.BlockSpec((rp, LANES), lambda n: (n, 0)),
                   pl.BlockSpec((rp, LANES), lambda n: (n, 0)),
                   pl.BlockSpec((hp, rp), lambda n: (0, n)),
                   pl.BlockSpec((ms, LANES), lambda n: (0, 0))],
        out_shape=[jax.ShapeDtypeStruct((l, LANES), F32),
                   jax.ShapeDtypeStruct((l, LANES), F32),
                   jax.ShapeDtypeStruct((hp, l), F32),
                   jax.ShapeDtypeStruct((ms, LANES), F32)],
        compiler_params=_params("arbitrary"),
        name="gdn_prep",
    )(h1, h1s, w_t, alog_row, dtb_row)

    hd = h_a * dk
    hb = _tile(h_a, TILES["gdn_heads"], 1)
    ng = h_a // hb
    blk = lambda off: pl.BlockSpec((c, hb * LANES), lambda n, g: (n, g + off))
    uv, w, aqk, qdec, kdect = pl.pallas_call(
        functools.partial(_gdn_intra_kernel, hb=hb, h_a=h_a),
        grid=(nchunk, ng),
        in_specs=[blk(0), blk(ng), blk(2 * ng),
                  pl.BlockSpec((c, LANES), lambda n, g: (n, 0)),
                  pl.BlockSpec((c, LANES), lambda n, g: (n, 0)),
                  pl.BlockSpec((hp, c), lambda n, g: (0, n))],
        out_specs=[blk(0), blk(0), blk(0), blk(0),
                   pl.BlockSpec((hb * LANES, c), lambda n, g: (g, n))],
        out_shape=[jax.ShapeDtypeStruct((l, hd), F32),
                   jax.ShapeDtypeStruct((l, hd), BF16),
                   jax.ShapeDtypeStruct((l, hd), BF16),
                   jax.ShapeDtypeStruct((l, hd), BF16),
                   jax.ShapeDtypeStruct((hd, l), BF16)],
        compiler_params=_params("parallel", "arbitrary"),
        name="gdn_intra",
    )(qkv, qkv, qkv, beta, gc, gct)

    hr = hb
    wide = lambda: pl.BlockSpec((c, hr * LANES), lambda g, n: (n, g))
    oa, s_fin = pl.pallas_call(
        functools.partial(_gdn_rec_kernel, hb=hr),
        grid=(h_a // hr, nchunk),
        in_specs=[wide(), wide(), wide(), wide(),
                  pl.BlockSpec((hr * LANES, c), lambda g, n: (g, n)),
                  pl.BlockSpec((hp, c), lambda g, n: (0, n)),
                  wide(),
                  pl.BlockSpec((1, LANES), lambda g, n: (0, 0))],
        out_specs=[wide(), pl.BlockSpec((hr, dk, LANES), lambda g, n: (g, 0, 0))],
        out_shape=[jax.ShapeDtypeStruct((l, hd), BF16),
                   jax.ShapeDtypeStruct((h_a, dk, LANES), F32)],
        scratch_shapes=[pltpu.VMEM((hr, dk, LANES), F32)],
        compiler_params=_params("parallel", "arbitrary"),
        name="gdn_rec",
    )(uv, w, aqk, qdec, kdect, gct, za_silu, gnorm)
    return oa, s_fin, ba_s


def _gdn_sample_kernel(zq_ref, za_ref, ba_ref, buf_ref, s_ref, cw_ref, alog_ref, dtb_ref, gn_ref,
                       o_ref, snew_ref, bufnew_ref, *, h_a, dk, width):
    hd = h_a * dk
    raw = zq_ref[0]
    buf = buf_ref[0]
    y = raw * cw_ref[width - 1:width, :]
    for j in range(width - 1):
        y = y + buf[j:j + 1, :] * cw_ref[j:j + 1, :]
    y = _silu(y)
    bufnew_ref[0, 0:width - 2, :] = buf[1:width - 1, :]
    bufnew_ref[0, width - 2:width - 1, :] = raw
    za = za_ref[0]
    beta_row, g_row = _gdn_gates(ba_ref[0], alog_ref[...], dtb_ref[...])
    lane1 = lax.broadcasted_iota(jnp.int32, (1, LANES), 1)
    row = lax.broadcasted_iota(jnp.int32, (dk, LANES), 0)
    col = lax.broadcasted_iota(jnp.int32, (dk, LANES), 1)
    eye = row == col
    for h in range(h_a):
        q = y[:, h * dk:(h + 1) * dk]
        k = y[:, hd + h * dk:hd + (h + 1) * dk]
        v = y[:, 2 * hd + h * dk:2 * hd + (h + 1) * dk]
        q = q * lax.rsqrt(jnp.sum(q * q, axis=-1, keepdims=True) + EPS) * (dk ** -0.5)
        k = k * lax.rsqrt(jnp.sum(k * k, axis=-1, keepdims=True) + EPS)
        beta = jnp.sum(jnp.where(lane1 == h, beta_row, 0.0), axis=1, keepdims=True)
        a = jnp.exp(jnp.sum(jnp.where(lane1 == h + h_a, g_row, 0.0), axis=1, keepdims=True))
        k_col = jnp.sum(jnp.where(eye, k, 0.0), axis=1, keepdims=True)
        q_col = jnp.sum(jnp.where(eye, q, 0.0), axis=1, keepdims=True)
        s = a * s_ref[0, h]
        u = beta * (v - jnp.sum(s * k_col, axis=0, keepdims=True))
        s_new = s + k_col * u
        snew_ref[0, h] = s_new
        o = jnp.sum(s_new * q_col, axis=0, keepdims=True)
        ms = jnp.mean(o * o, axis=-1, keepdims=True)
        o_ref[0, :, h * dk:(h + 1) * dk] = (
            o * lax.rsqrt(ms + EPS) * gn_ref[...] * _silu(za[:, h * dk:(h + 1) * dk])
        ).astype(o_ref.dtype)


def _gdn_sample(zq, za, ba, buf, s0, conv_w, a_log, dt_bias, gnorm, h_a, dk):
    b = zq.shape[0]
    width = conv_w.shape[0]
    hd = h_a * dk
    alog_row, dtb_row = _gate_rows(a_log, dt_bias, h_a)
    per_seq = lambda n: pl.BlockSpec((1, 1, n), lambda i: (i, 0, 0))
    return pl.pallas_call(
        functools.partial(_gdn_sample_kernel, h_a=h_a, dk=dk, width=width),
        grid=(b,),
        in_specs=[per_seq(3 * hd), per_seq(hd), per_seq(LANES),
                  pl.BlockSpec((1, width - 1, 3 * hd), lambda i: (i, 0, 0)),
                  pl.BlockSpec((1, h_a, dk, LANES), lambda i: (i, 0, 0, 0)),
                  pl.BlockSpec((width, 3 * hd), lambda i: (0, 0)),
                  pl.BlockSpec((1, LANES), lambda i: (0, 0)),
                  pl.BlockSpec((1, LANES), lambda i: (0, 0)),
                  pl.BlockSpec((1, LANES), lambda i: (0, 0))],
        out_specs=[pl.BlockSpec((1, 1, hd), lambda i: (i, 0, 0)),
                   pl.BlockSpec((1, h_a, dk, LANES), lambda i: (i, 0, 0, 0)),
                   pl.BlockSpec((1, width - 1, 3 * hd), lambda i: (i, 0, 0))],
        out_shape=[jax.ShapeDtypeStruct((b, 1, hd), BF16),
                   jax.ShapeDtypeStruct((b, h_a, dk, LANES), F32),
                   jax.ShapeDtypeStruct((b, width - 1, 3 * hd), F32)],
        compiler_params=_params("parallel"),
        name="gdn_sample",
    )(zq[:, None, :], za[:, None, :], ba[:, None, :], buf, s0, conv_w, alog_row, dtb_row, gnorm)


def _diff_lambda(lamp_ref, lam_init):
    lp = lamp_ref[...]
    e1 = jnp.exp(jnp.sum(lp[0:1] * lp[1:2], axis=1, keepdims=True))
    e2 = jnp.exp(jnp.sum(lp[2:3] * lp[3:4], axis=1, keepdims=True))
    return e1 - e2 + lam_init


def _attn_prompt_kernel(qi_ref, kj_ref, q_ref, k_ref, v_ref, lamp_ref, g_ref, o_ref,
                        m_ref, l_ref, acc_ref, *, dq, lam_init):
    p = pl.program_id(1)
    qi = qi_ref[p]
    kj = kj_ref[p]

    @pl.when(kj == 0)
    def _():
        m_ref[...] = jnp.full(m_ref.shape, NEG_BIG, F32)
        l_ref[...] = jnp.zeros(l_ref.shape, F32)
        acc_ref[...] = jnp.zeros(acc_ref.shape, F32)

    def step(masked):
        v = v_ref[...]
        for c in range(2):
            s = lax.dot_general(q_ref[:, c * dq:(c + 1) * dq], k_ref[:, c * dq:(c + 1) * dq],
                                NT_DIMS, preferred_element_type=F32)
            if masked:
                row = lax.broadcasted_iota(jnp.int32, s.shape, 0)
                col = lax.broadcasted_iota(jnp.int32, s.shape, 1)
                s = jnp.where(row >= col, s, NEG_BIG)
            m_prev = m_ref[c]
            m_new = jnp.maximum(m_prev, jnp.max(s, axis=1, keepdims=True))
            alpha = jnp.exp2(m_prev - m_new)
            pm = jnp.exp2(s - m_new)
            l_ref[c] = alpha * l_ref[c] + jnp.sum(pm, axis=1, keepdims=True)
            acc_ref[c] = alpha * acc_ref[c] + jnp.dot(pm.astype(BF16), v,
                                                      preferred_element_type=F32)
            m_ref[c] = m_new

    @pl.when(kj < qi)
    def _():
        step(False)

    @pl.when(kj == qi)
    def _():
        step(True)
        _attn_finish(acc_ref, l_ref[0], l_ref[1], lamp_ref, g_ref, o_ref, lam_init)


def _attn_finish(acc_ref, l0, l1, lamp_ref, g_ref, o_ref, lam_init):
    lam = _diff_lambda(lamp_ref, lam_init)
    o = acc_ref[0] / l0 - lam * (acc_ref[1] / l1)
    ms = jnp.mean(o * o, axis=-1, keepdims=True)
    o_ref[...] = (o * lax.rsqrt(ms + SUBLN_EPS) * g_ref[...] * (1.0 - lam_init)).astype(o_ref.dtype)


def _attn_prompt_bounded_kernel(qi_ref, kj_ref, q_ref, k_ref, v_ref, lamp_ref, g_ref, o_ref,
                                l_ref, acc_ref, *, dq, lam_init):
    p = pl.program_id(1)
    qi = qi_ref[p]
    kj = kj_ref[p]

    @pl.when(kj == 0)
    def _():
        l_ref[...] = jnp.zeros(l_ref.shape, F32)
        acc_ref[...] = jnp.zeros(acc_ref.shape, F32)

    t = q_ref.shape[0]

    def accumulate(c, r0, nr, k0, nk, masked):
        s = lax.dot_general(q_ref[r0:r0 + nr, c * dq:(c + 1) * dq],
                            k_ref[k0:k0 + nk, c * dq:(c + 1) * dq], NT_DIMS,
                            preferred_element_type=F32)
        pm = jnp.exp2(s)
        if masked:
            row = lax.broadcasted_iota(jnp.int32, s.shape, 0) + r0
            col = lax.broadcasted_iota(jnp.int32, s.shape, 1) + k0
            pm = jnp.where(row >= col, pm, 0.0)
        part = pm[:, 0:LANES]
        for g in range(1, nk // LANES):
            part = part + pm[:, g * LANES:(g + 1) * LANES]
        l_ref[c, r0:r0 + nr, :] += part
        acc_ref[c, r0:r0 + nr, :] += jnp.dot(pm.astype(BF16), v_ref[k0:k0 + nk, :],
                                             preferred_element_type=F32)

    @pl.when(kj < qi)
    def _():
        for c in range(2):
            accumulate(c, 0, t, 0, t, False)

    @pl.when(kj == qi)
    def _():
        half = t // 2 if t % (2 * LANES) == 0 else t
        for c in range(2):
            accumulate(c, 0, half, 0, half, True)
            if half < t:
                accumulate(c, half, t - half, 0, t, True)
        _attn_finish(acc_ref, jnp.sum(l_ref[0], axis=1, keepdims=True),
                     jnp.sum(l_ref[1], axis=1, keepdims=True), lamp_ref, g_ref, o_ref, lam_init)


def _attn_prompt(q, k, v, score_bound, lam_params, subln_g, h_b, dq, lam_init):
    l = q.shape[0]
    dv = 2 * dq
    t = _tile(l, TILES["attn"], LANES)
    nb = l // t
    pairs = [(i, j) for i in range(nb) for j in range(i + 1)]
    qi_tab = jnp.asarray([a for a, _ in pairs], jnp.int32)
    kj_tab = jnp.asarray([b for _, b in pairs], jnp.int32)

    def call(body, scratch, name):
        grid_spec = pltpu.PrefetchScalarGridSpec(
            num_scalar_prefetch=2,
            grid=(h_b, len(pairs)),
            in_specs=[pl.BlockSpec((t, dv), lambda h, p, qi, kj: (qi[p], h)),
                      pl.BlockSpec((t, dv), lambda h, p, qi, kj: (kj[p], h)),
                      pl.BlockSpec((t, dv), lambda h, p, qi, kj: (kj[p], h)),
                      pl.BlockSpec((4, dq), lambda h, p, qi, kj: (0, 0)),
                      pl.BlockSpec((1, dv), lambda h, p, qi, kj: (0, 0))],
            out_specs=pl.BlockSpec((t, dv), lambda h, p, qi, kj: (qi[p], h)),
            scratch_shapes=scratch,
        )
        return pl.pallas_call(
            functools.partial(body, dq=dq, lam_init=lam_init),
            grid_spec=grid_spec,
            out_shape=jax.ShapeDtypeStruct((l, h_b * dv), BF16),
            compiler_params=_params("parallel", "arbitrary"),
            name=name,
        )(qi_tab, kj_tab, q, k, v, lam_params, subln_g)

    bounded = lambda: call(_attn_prompt_bounded_kernel,
                           [pltpu.VMEM((2, t, LANES), F32), pltpu.VMEM((2, t, dv), F32)],
                           "attn_prompt_bounded")
    online = lambda: call(_attn_prompt_kernel,
                          [pltpu.VMEM((2, t, 1), F32), pltpu.VMEM((2, t, 1), F32),
                           pltpu.VMEM((2, t, dv), F32)], "attn_prompt_online")
    return lax.cond(score_bound <= MAX_UNSHIFTED_SCORE, bounded, online)


def _attn_decode_kernel(pt_ref, q_ref, kn_ref, vn_ref, *rest, h_b, dq, lam_init, pp):
    del pt_ref
    kc_refs = rest[:pp]
    vc_refs = rest[pp:2 * pp]
    lamp_ref, g_ref, o_ref, m_ref, l_ref, acc_ref = rest[2 * pp:]
    p = pl.program_id(1)
    dv = 2 * dq
    page = kc_refs[0].shape[2]
    assert h_b & (h_b - 1) == 0
    qm = q_ref[0]

    @pl.when(p == 0)
    def _():
        m_ref[...] = jnp.sum(qm * kn_ref[0], axis=1, keepdims=True)
        l_ref[...] = jnp.ones(l_ref.shape, F32)
        acc_ref[...] = vn_ref[0]

    qb = qm.astype(BF16)
    scores = []
    for j in range(pp):
        k2 = kc_refs[j][0, 0].reshape(page * h_b, dv).astype(BF16)
        s = lax.dot_general(qb, k2, NT_DIMS, preferred_element_type=F32)
        sub = lax.broadcasted_iota(jnp.int32, s.shape, 0)
        lane = lax.broadcasted_iota(jnp.int32, s.shape, 1)
        scores.append(jnp.where((lane & (h_b - 1)) == (sub >> 1), s, NEG_BIG))
    m_prev = m_ref[...]
    m_new = m_prev
    for s in scores:
        m_new = jnp.maximum(m_new, jnp.max(s, axis=1, keepdims=True))
    alpha = jnp.exp(m_prev - m_new)
    l_new = alpha * l_ref[...]
    acc = alpha * acc_ref[...]
    for j in range(pp):
        pm = jnp.exp(scores[j] - m_new)
        l_new = l_new + jnp.sum(pm, axis=1, keepdims=True)
        v2 = vc_refs[j][0, 0].reshape(page * h_b, dv).astype(BF16)
        acc = acc + jnp.dot(pm.astype(BF16), v2, preferred_element_type=F32)
    l_ref[...] = l_new
    acc_ref[...] = acc
    m_ref[...] = m_new

    @pl.when(p == pl.num_programs(1) - 1)
    def _():
        lam = _diff_lambda(lamp_ref, lam_init)
        for h in range(h_b):
            o1 = acc_ref[2 * h:2 * h + 1, :] / l_ref[2 * h:2 * h + 1, :]
            o2 = acc_ref[2 * h + 1:2 * h + 2, :] / l_ref[2 * h + 1:2 * h + 2, :]
            o = o1 - lam * o2
            ms = jnp.mean(o * o, axis=-1, keepdims=True)
            o_ref[0, :, h * dv:(h + 1) * dv] = (o * lax.rsqrt(ms + SUBLN_EPS) * g_ref[...]
                                                * (1.0 - lam_init)).astype(o_ref.dtype)


def _attn_decode(qn, kn, vn, cache_k, cache_v, page_table, lam_params, subln_g, h_b, dq,
                 lam_init, lyr):
    b = qn.shape[0]
    n_pages = page_table.shape[1]
    page = cache_k.shape[2]
    dv = 2 * dq
    nsub = 2 * h_b
    pp = _tile(n_pages, TILES["decode_pages"], 1)
    zeros = jnp.zeros((b, h_b, dq), F32)

    def sub_rows(x):
        x4 = x.reshape(b, h_b, 2, dq)
        return jnp.stack([jnp.concatenate([x4[:, :, 0], zeros], axis=-1),
                          jnp.concatenate([zeros, x4[:, :, 1]], axis=-1)], axis=2).reshape(b, nsub, dv)

    qm = sub_rows(qn)
    km = sub_rows(kn)
    vm = jnp.repeat(vn.reshape(b, h_b, 1, dv), 2, axis=2).reshape(b, nsub, dv)
    rows = lambda: pl.BlockSpec((1, nsub, dv), lambda i, p, pt: (i, 0, 0))

    def pages(j):
        return pl.BlockSpec((1, 1, page, h_b, dv), lambda i, p, pt: (lyr, pt[i, p * pp + j], 0, 0, 0))

    grid_spec = pltpu.PrefetchScalarGridSpec(
        num_scalar_prefetch=1,
        grid=(b, n_pages // pp),
        in_specs=([rows(), rows(), rows()] + [pages(j) for j in range(pp)]
                  + [pages(j) for j in range(pp)]
                  + [pl.BlockSpec((4, dq), lambda i, p, pt: (0, 0)),
                     pl.BlockSpec((1, dv), lambda i, p, pt: (0, 0))]),
        out_specs=pl.BlockSpec((1, 1, h_b * dv), lambda i, p, pt: (i, 0, 0)),
        scratch_shapes=[pltpu.VMEM((nsub, 1), F32), pltpu.VMEM((nsub, 1), F32),
                        pltpu.VMEM((nsub, dv), F32)],
    )
    return pl.pallas_call(
        functools.partial(_attn_decode_kernel, h_b=h_b, dq=dq, lam_init=lam_init, pp=pp),
        grid_spec=grid_spec,
        out_shape=jax.ShapeDtypeStruct((b, 1, h_b * dv), BF16),
        compiler_params=_params("parallel", "arbitrary"),
        name="attn_decode",
    )(page_table, qm, km, vm, *([cache_k] * pp), *([cache_v] * pp), lam_params, subln_g)


def _ep_plain(accs, e, o, carry, first, last):
    o[0][...] = accs[0].astype(o[0].dtype)


def _ep_bias(accs, e, o, carry, first, last):
    o[0][...] = accs[0] + e[0][...]


def _ep_silu(accs, e, o, carry, first, last):
    o[0][...] = _silu(accs[0]).astype(o[0].dtype)


def _ep_sigmoid(accs, e, o, carry, first, last):
    o[0][...] = _sigmoid(accs[0]).astype(o[0].dtype)


def _group_norm_store(y, gain, eps, out_refs):
    tn = y.shape[1]
    for g in range(tn // LANES):
        sl = slice(g * LANES, (g + 1) * LANES)
        blk = y[:, sl]
        ms = jnp.sum(blk * blk, axis=-1, keepdims=True) / LANES
        val = blk * lax.rsqrt(ms + eps) * gain[:, sl]
        for r in out_refs:
            r[:, sl] = val.astype(r.dtype)


def _ep_qknorm(accs, e, o, carry, first, last):
    _group_norm_store(accs[0], e[0][...], EPS, o)


def _ep_copy2(accs, e, o, carry, first, last):
    for r in o:
        r[...] = accs[0].astype(r.dtype)


def _sample_post_kernel(q_ref, k_ref, ga_ref, gb_ref, gq_ref, gk_ref, qo_ref, ko_ref, sa_ref, sb_ref):
    _group_norm_store(q_ref[...], gq_ref[...], EPS, [qo_ref])
    _group_norm_store(k_ref[...], gk_ref[...], EPS, [ko_ref])
    sa_ref[...] = _sigmoid(ga_ref[...])
    sb_ref[...] = _sigmoid(gb_ref[...])


def _ep_gdn_conv(accs, e, o, carry, first, last, *, width):
    raw = accs[0]
    taps = [e[j][...] for j in range(width)]
    nflag = e[width][...]
    gain = e[width + 1][...]
    val = _silu(_conv_taps(raw, carry[0], taps, first))
    for g in range(val.shape[1] // LANES):
        sl = slice(g * LANES, (g + 1) * LANES)
        blk = val[:, sl]
        ss = jnp.sum(blk * blk, axis=-1, keepdims=True)
        scale = jnp.where(nflag[:, sl] > 0.5, lax.rsqrt(ss + EPS), 1.0) * gain[:, sl]
        o[0][:, sl] = (blk * scale).astype(o[0].dtype)
    if last:
        o[1][...] = carry[0][0:SUBLANES, :]


def _ep_branch(accs, e, o, carry, first, last):
    o[0][...] = (e[0][...].astype(F32) * accs[0] + e[1][...].astype(F32) * accs[1]).astype(o[0].dtype)


def _ep_residual(accs, e, o, carry, first, last):
    o[0][...] = e[0][...] + e[1][...] * accs[0]


def _ep_ffn_prompt(accs, e, o, carry, first, last, *, width):
    outs = []
    for d in range(2):
        taps = [e[d * (width + 1) + j][...] for j in range(width)]
        bias = e[d * (width + 1) + width][...]
        outs.append(_conv_taps(accs[d], carry[d], taps, first) + bias)
        if last:
            o[1 + d][...] = carry[d][0:SUBLANES, :]
    o[0][...] = (_silu(outs[0]) * outs[1]).astype(o[0].dtype)


def _sample_mix_kernel(ra_ref, rb_ref, ga_ref, gb_ref, o_ref):
    o_ref[...] = (ga_ref[...] * ra_ref[...] + gb_ref[...] * rb_ref[...]).astype(o_ref.dtype)


def _resid_normmod_kernel(x_ref, r_ref, gate_ref, g_ref, sc_ref, sh_ref, x2_ref, h_ref):
    x2 = x_ref[...] + gate_ref[...] * r_ref[...]
    x2_ref[...] = x2
    ms = jnp.mean(x2 * x2, axis=-1, keepdims=True)
    h = x2 * lax.rsqrt(ms + EPS) * g_ref[...]
    h_ref[...] = (h * (1.0 + sc_ref[...]) + sh_ref[...]).astype(h_ref.dtype)


def _ffn_sample_kernel(*refs, width):
    ns = width - 1
    ups = refs[0:2]
    states = (refs[2:2 + ns], refs[2 + ns:2 + 2 * ns])
    taps = refs[2 + 2 * ns:4 + 2 * ns]
    bias = refs[4 + 2 * ns:6 + 2 * ns]
    o_ref = refs[6 + 2 * ns]
    vals = []
    for d in range(2):
        y = ups[d][...] * taps[d][width - 1:width, :]
        for j in range(ns):
            y = y + states[d][j][...] * taps[d][j:j + 1, :]
        vals.append(y + bias[d][...])
    o_ref[...] = (_silu(vals[0]) * vals[1]).astype(o_ref.dtype)


def kernel(x_prompt, x_sample, c_prompt, c_sample, cache_k, cache_v, state_gdn, state_gdn_conv, state_ffn_conv, page_table, w_ada, b_ada, norm1_g, norm2_g, w_in, gdn_conv_w, gdn_a_log, gdn_dt_bias, gdn_norm_g, diff_q_norm_g, diff_k_norm_g, diff_lambda, diff_subln_g, w_branch_a, w_branch_b, w_o, w_up, ffn_conv_w, ffn_conv_b, w_down):
    depth = w_in.shape[0]
    assert depth == 1 and x_prompt.shape[0] == 1 and x_sample.shape[1] == 1
    lyr = 0
    lam_init = 0.8 - 0.6 * math.exp(-0.3 * lyr)
    d = x_prompt.shape[-1]
    l = x_prompt.shape[1]
    nb = x_sample.shape[0]
    h_a, dk, dva = state_gdn.shape[2:]
    h_b = cache_k.shape[3]
    dq = cache_k.shape[4] // 2
    dvb = cache_v.shape[4]
    assert dk == LANES and dva == LANES and dq == LANES and dvb == 2 * dq
    d_ff = w_down.shape[1]
    gw = gdn_conv_w.shape[1]
    fw = ffn_conv_w.shape[1]
    hd_a = h_a * dk
    hd_b = h_b * dvb

    w_t = jnp.swapaxes(w_in[lyr], 0, 1)
    c_za = 3 * hd_a
    c_ba = c_za + hd_a
    n_head = c_ba + LANES
    c_qb = c_ba + 2 * h_a
    c_kb, c_vb, c_ga, c_gb = c_qb + hd_b, c_qb + 2 * hd_b, c_qb + 3 * hd_b, c_qb + 3 * hd_b + d
    n_tail = 3 * hd_b + 2 * d
    assert c_qb + n_tail == w_t.shape[0] and 2 * h_a <= LANES
    t_qb, t_kb, t_vb, t_ga, t_gb = 0, hd_b, 2 * hd_b, 3 * hd_b, 3 * hd_b + d
    w_up0 = w_up[lyr]
    w_dn = w_down[lyr].astype(BF16)
    w_o_b = w_o[lyr]
    w_ba_b = w_branch_a[lyr]
    w_bb_b = w_branch_b[lyr]

    row = lambda v: v.reshape(1, -1).astype(F32)
    tile_rows = lambda v, reps: jnp.tile(row(v), (1, reps))

    c_all = jnp.concatenate([c_prompt, c_sample], axis=0).astype(F32)
    mp = -(-c_all.shape[0] // 16) * 16
    c_all = jnp.pad(c_all, ((0, mp - c_all.shape[0]), (0, 0)))
    (mod,) = _matmul("adaln", [c_all], [(w_ada[lyr], 0)], [(0, 0)], 6 * d, _ep_bias, [(F32, "tile")],
                     tm=mp, tn=_tile(6 * d, 512, LANES), extras=[(row(b_ada[lyr]), "row", 0)],
                     a_fn=_silu)
    mod_p = [mod[0:1, j * d:(j + 1) * d] for j in range(6)]
    mod_s = [mod[1:1 + nb, j * d:(j + 1) * d] for j in range(6)]

    gq = tile_rows(diff_q_norm_g[lyr], 2 * h_b) * (dq ** -0.5)
    gk = tile_rows(diff_k_norm_g[lyr], 2 * h_b)
    gq2 = gq * math.log2(math.e)
    score_bound = dq * jnp.max(jnp.abs(gq2)) * jnp.max(jnp.abs(gk))
    cw = gdn_conv_w[lyr].astype(F32)
    conv_nflag = jnp.concatenate([jnp.ones((1, 2 * hd_a), F32), jnp.zeros((1, hd_a), F32)], axis=1)
    conv_gain = jnp.concatenate([jnp.full((1, hd_a), dk ** -0.5, F32), jnp.ones((1, 2 * hd_a), F32)], axis=1)
    gnorm = row(gdn_norm_g[lyr])
    subln = row(diff_subln_g[lyr])
    lamp = diff_lambda[lyr].astype(F32)
    fcw = ffn_conv_w[lyr].astype(F32)
    fcb = row(ffn_conv_b[lyr])
    a_log = gdn_a_log[lyr].astype(F32)
    dt_bias = gdn_dt_bias[lyr].astype(F32)

    xp = x_prompt[0].astype(F32)
    tm = _tile(l, TILES["mm_m"], 16)
    tn = TILES["mm_n"]
    split = 4 if tm % 64 == 0 else 1
    split_ffn = split
    h1 = _normmod(xp, row(norm1_g[lyr]), mod_p[1], mod_p[0])
    xs = x_sample[:, 0, :].astype(F32)
    h1s = _normmod(xs, row(norm1_g[lyr]), mod_s[1], mod_s[0])

    stationary = dict(tm=tm, order="nm", cast_once=True, row_split=split, wt=True, side=h1s)
    qkv_c, qkv_tail, zq_s = _matmul(
        "p_qkv_conv", [h1], [(w_t, 0)], [(0, 0)], 3 * hd_a,
        functools.partial(_ep_gdn_conv, width=gw), [(BF16, "tile"), (F32, "tail")],
        tn=_tile(hd_a, tn, LANES), carry_rows=1, **stationary,
        extras=[(cw[j:j + 1], "row", 0) for j in range(gw)] + [(conv_nflag, "row", 0), (conv_gain, "row", 0)])
    za_p, za_s = _matmul("p_za", [h1], [(w_t, c_za)], [(0, 0)], hd_a, _ep_silu, [(BF16, "tile")],
                         tn=_tile(hd_a, tn, LANES), **stationary)
    tn_b = _tile(hd_b, tn, LANES)
    qb_p, qb_s = _matmul("p_qb", [h1], [(w_t, c_qb)], [(0, 0)], hd_b, _ep_qknorm, [(BF16, "tile")],
                         tn=tn_b, extras=[(gq2, "row", 0)], **stationary)
    kb_p, kb_p16, kb_s = _matmul("p_kb", [h1], [(w_t, c_kb)], [(0, 0)], hd_b, _ep_qknorm,
                                 [(F32, "tile"), (BF16, "tile")], tn=tn_b, extras=[(gk, "row", 0)],
                                 **stationary)
    vb_p, vb_p16, vb_s = _matmul("p_vb", [h1], [(w_t, c_vb)], [(0, 0)], hd_b, _ep_copy2,
                                 [(F32, "tile"), (BF16, "tile")], tn=tn_b, **stationary)
    tn_d = _tile(d, tn, LANES)
    gates_p, gates_s = _matmul("p_gates", [h1], [(w_t, c_ga)], [(0, 0)], 2 * d, _ep_sigmoid,
                               [(BF16, "tile")], tn=tn_d, **stationary)

    oa_p, s_p, ba_s = _gdn_prompt(qkv_c, h1, h1s, w_t, c_ba, za_p, a_log, dt_bias, gnorm, h_a, dk)
    ob_p = _attn_prompt(qb_p, kb_p16, vb_p16, score_bound, lamp, subln, h_b, dq, lam_init)

    tns = TILES["small_n"]
    oa_s, s_s, gbuf_s = _gdn_sample(zq_s, za_s, ba_s, state_gdn_conv[lyr].astype(F32),
                                    state_gdn[lyr].astype(F32), cw, a_log, dt_bias, gnorm, h_a, dk)
    qn_s, kn_s, sga_s, sgb_s = pl.pallas_call(
        _sample_post_kernel,
        out_shape=[jax.ShapeDtypeStruct((nb, hd_b), F32), jax.ShapeDtypeStruct((nb, hd_b), F32),
                   jax.ShapeDtypeStruct((nb, d), F32), jax.ShapeDtypeStruct((nb, d), F32)],
        name="sample_post",
    )(qb_s, kb_s, gates_s[:, :d], gates_s[:, d:], gq, gk)
    ob_s = _attn_decode(qn_s, kn_s, vb_s, cache_k, cache_v, page_table.astype(jnp.int32), lamp, subln,
                        h_b, dq, lam_init, lyr)
    tn_ds = _tile(d, tns, LANES)

    mixed_p, ra_s, rb_s = _matmul(
        "p_branch", [oa_p, ob_p], [(w_ba_b, 0), (w_bb_b, 0)], [(0, 0), (1, 1)], d,
        _ep_branch, [(BF16, "tile")], tm=tm, tn=tn_d, row_split=split, order="nm", cast_once=True,
        extras=[(gates_p, "tile", 0), (gates_p, "tile", d)], side=[oa_s[:, 0, :], ob_s[:, 0, :]])
    mixed_s = pl.pallas_call(
        _sample_mix_kernel, out_shape=jax.ShapeDtypeStruct((nb, d), BF16), name="s_mix",
    )(ra_s, rb_s, sga_s, sgb_s)
    x2_p, ro_s = _matmul(
        "p_wo", [mixed_p], [(w_o_b, 0)], [(0, 0)], d, _ep_residual, [(F32, "tile")],
        tm=tm, tn=tn_d, row_split=split, order="nm", cast_once=True,
        extras=[(xp, "tile", 0), (mod_p[2], "row", 0)], side=mixed_s)
    x2_s, h2s = pl.pallas_call(
        _resid_normmod_kernel,
        out_shape=[jax.ShapeDtypeStruct((nb, d), F32), jax.ShapeDtypeStruct((nb, d), BF16)],
        name="s_resid_norm",
    )(xs, ro_s, mod_s[2], row(norm2_g[lyr]), mod_s[4], mod_s[3])
    h2 = _normmod(x2_p, row(norm2_g[lyr]), mod_p[4], mod_p[3])
    tn_u = _tile(d_ff, TILES["up_n"], LANES)
    assert d_ff % tn_u == 0
    ffn_rows = ([(fcw[j:j + 1], "row", 0) for j in range(fw)] + [(fcb, "row", 0)]
                + [(fcw[j:j + 1], "row", d_ff) for j in range(fw)] + [(fcb, "row", d_ff)])
    act_f, tail_g, tail_v, up_g, up_v = _matmul(
        "p_ffn_up", [h2], [(w_up0, 0), (w_up0, d_ff)], [(0, 0), (0, 1)], d_ff,
        functools.partial(_ep_ffn_prompt, width=fw),
        [(BF16, "tile"), (F32, "tail"), (F32, "tail")],
        tm=tm, tn=tn_u, order="nm", carry_rows=2, cast_once=True, row_split=split_ffn, extras=ffn_rows,
        side=h2s)
    tm_dn = _tile(l, TILES["down_m"], 16)
    (y_p,) = _matmul("p_down", [act_f], [(w_dn, 0)], [(0, 0)], d, _ep_residual, [(F32, "tile")],
                     tm=tm_dn, tn=tn_d, row_split=2 if tm_dn % 32 == 0 else 1,
                     extras=[(x2_p, "tile", 0), (mod_p[5], "row", 0)])

    fbuf = state_ffn_conv[lyr].astype(F32)
    tn_f = _tile(d_ff, 8192, LANES)
    nfb = d_ff // tn_f
    half = lambda rows, v: pl.BlockSpec((rows, tn_f), lambda j: (0, j + v * nfb))
    act_s = pl.pallas_call(
        functools.partial(_ffn_sample_kernel, width=fw),
        grid=(nfb,),
        in_specs=([half(nb, 0), half(nb, 0)]
                  + [half(nb, 0) for _ in range(fw - 1)] + [half(nb, 1) for _ in range(fw - 1)]
                  + [half(fw, 0), half(fw, 1), half(1, 0), half(1, 1)]),
        out_specs=half(nb, 0),
        out_shape=jax.ShapeDtypeStruct((nb, d_ff), BF16),
        compiler_params=_params("parallel"),
        name="s_ffn_conv",
    )(up_g, up_v, *[fbuf[:, j, :] for j in range(fw - 1)], *[fbuf[:, j, :] for j in range(fw - 1)],
      fcw, fcw, fcb, fcb)
    (y_s,) = _matmul("s_down", [act_s], [(w_dn, 0)], [(0, 0)], d, _ep_residual, [(F32, "tile")],
                     tm=nb, tn=tn_ds, tk=_tile(d_ff, TILES["down_k"], LANES),
                     extras=[(x2_s, "tile", 0), (mod_s[5], "tile", 0)])

    y_prompt = y_p[None].astype(x_prompt.dtype)
    y_sample = y_s[:, None, :].astype(x_sample.dtype)
    k_prompt = kb_p.reshape(1, 1, l, h_b, 2 * dq)
    v_prompt = vb_p.reshape(1, 1, l, h_b, dvb)
    gdn_state_prompt = s_p[None, None]
    gdn_conv_prompt = qkv_tail[SUBLANES - (gw - 1):][None, None]
    ffn_conv_prompt = jnp.concatenate([tail_g[SUBLANES - (fw - 1):], tail_v[SUBLANES - (fw - 1):]],
                                      axis=1)[None, None]
    k_sample = kn_s.reshape(1, nb, 1, h_b, 2 * dq)
    v_sample = vb_s.reshape(1, nb, 1, h_b, dvb)
    gdn_state_sample = s_s[None]
    gdn_conv_sample = gbuf_s[None]
    up_new = jnp.concatenate([up_g, up_v], axis=1)
    ffn_conv_sample = jnp.concatenate([fbuf[:, 1:, :], up_new[:, None, :]], axis=1)[None]
    return (y_prompt, y_sample, k_prompt, v_prompt, gdn_state_prompt, gdn_conv_prompt,
            ffn_conv_prompt, k_sample, v_sample, gdn_state_sample, gdn_conv_sample, ffn_conv_sample)
```

```python
import functools
import math

import jax
import jax.numpy as jnp
from jax import lax
from jax.experimental import pallas as pl
from jax.experimental.pallas import tpu as pltpu

F32 = jnp.float32
BF16 = jnp.bfloat16

LANES = 128
SUBLANES = 8
VMEM_LIMIT_BYTES = 56 * 1024 * 1024

EPS = 1e-6
SUBLN_EPS = 1e-5
GDN_CHUNK = 128
MAX_UNSHIFTED_SCORE = 40.0
NEG_BIG = -1e30

TILES = dict(
    norm_rows=512,
    mm_m=1024, mm_n=512,
    up_n=256,
    down_k=5504, down_m=512,
    attn=1024,
    gdn_heads=8,
    prep_rows=512,
    decode_pages=8,
    small_n=1024,
)

NT_DIMS = (((1,), (1,)), ((), ()))


def _params(*sem):
    return pltpu.CompilerParams(dimension_semantics=sem, vmem_limit_bytes=VMEM_LIMIT_BYTES)


def _tile(dim, pref, align):
    if dim <= pref:
        return dim
    t = (pref // align) * align
    while t >= align:
        if dim % t == 0:
            return t
        t -= align
    return dim


def _sigmoid(x):
    return 1.0 / (1.0 + jnp.exp(-x))


def _silu(x):
    return x * _sigmoid(x)


def _softplus(x):
    return jnp.maximum(x, 0.0) + jnp.log1p(jnp.exp(-jnp.abs(x)))


def _normmod_kernel(x_ref, g_ref, sc_ref, sh_ref, o_ref):
    x = x_ref[...]
    ms = jnp.mean(x * x, axis=-1, keepdims=True)
    h = x * lax.rsqrt(ms + EPS) * g_ref[...]
    o_ref[...] = (h * (1.0 + sc_ref[...]) + sh_ref[...]).astype(o_ref.dtype)


def _normmod(x, g, sc, sh):
    m, d = x.shape
    tm = _tile(m, TILES["norm_rows"], SUBLANES)
    per_row = sc.shape[0] != 1
    mod_spec = (pl.BlockSpec((tm, d), lambda i: (i, 0)) if per_row
                else pl.BlockSpec((1, d), lambda i: (0, 0)))
    return pl.pallas_call(
        _normmod_kernel,
        grid=(m // tm,),
        in_specs=[pl.BlockSpec((tm, d), lambda i: (i, 0)),
                  pl.BlockSpec((1, d), lambda i: (0, 0)), mod_spec, mod_spec],
        out_specs=pl.BlockSpec((tm, d), lambda i: (i, 0)),
        out_shape=jax.ShapeDtypeStruct((m, d), BF16),
        compiler_params=_params("parallel"),
        name="normmod",
    )(x, g, sc, sh)


def _matmul(name, as_, ws, dots, n, epilogue, outs, *, tm, tn, extras=(), order="mn",
            tk=None, a_fn=None, carry_rows=0, cast_once=False, row_split=1, wt=False, side=None):
    m = as_[0].shape[0]
    kdim = as_[0].shape[1]
    tk = kdim if tk is None else tk
    nk = kdim // tk
    assert kdim % tk == 0 and m % tm == 0 and n % tn == 0 and tm % row_split == 0
    if nk > 1:
        assert all(a.shape[1] == kdim for a in as_) and row_split == 1 and not cast_once
    assert not cast_once or order == "nm"
    ni, nj = m // tm, n // tn
    na, nw, ne, no, nd = len(as_), len(ws), len(extras), len(outs), len(dots)
    rows = tm // row_split

    if order == "mn":
        grid = (ni, nj, nk)
        ij = lambda g0, g1: (g0, g1)
    else:
        grid = (nj, ni, nk)
        ij = lambda g0, g1: (g1, g0)

    def a_map(g0, g1, k):
        return (ij(g0, g1)[0], k)

    def w_map(off):
        return lambda g0, g1, k: (k, ij(g0, g1)[1] + off // tn)

    def row_map(off):
        return lambda g0, g1, k: (0, ij(g0, g1)[1] + off // tn)

    def tile_map(off):
        return lambda g0, g1, k: (ij(g0, g1)[0], ij(g0, g1)[1] + off // tn)

    in_specs = []
    for a in as_:
        in_specs.append(pl.BlockSpec((tm, tk if nk > 1 else a.shape[1]), a_map))
    for w, off in ws:
        if wt:
            assert nk == 1 and off % SUBLANES == 0
            in_specs.append(pl.BlockSpec(
                (pl.Element(tn), pl.Element(w.shape[1])),
                functools.partial(
                    lambda g0, g1, k, off: (pl.multiple_of(off + ij(g0, g1)[1] * tn, SUBLANES), 0),
                    off=off)))
        else:
            assert off % tn == 0
            in_specs.append(pl.BlockSpec((tk if nk > 1 else w.shape[0], tn), w_map(off)))
    for arr, kind, off in extras:
        assert off % tn == 0
        if kind == "row":
            in_specs.append(pl.BlockSpec((1, tn), row_map(off)))
        else:
            in_specs.append(pl.BlockSpec((tm, tn), tile_map(off)))
    out_specs, out_shapes = [], []
    for dtype, kind in outs:
        if kind == "tile":
            out_specs.append(pl.BlockSpec((tm, tn), tile_map(0)))
            out_shapes.append(jax.ShapeDtypeStruct((m, n), dtype))
        else:
            out_specs.append(pl.BlockSpec((SUBLANES, tn), row_map(0)))
            out_shapes.append(jax.ShapeDtypeStruct((SUBLANES, n), dtype))
    nside = 0
    sides = []
    if side is not None:
        sides = list(side) if isinstance(side, (list, tuple)) else [side] * nw
        assert cast_once and len(sides) == nw
        nside = nw
        for sd in sides:
            in_specs.append(pl.BlockSpec(sd.shape, lambda g0, g1, k: (0, 0)))
        for sd in sides:
            out_specs.append(pl.BlockSpec((sd.shape[0], tn), row_map(0)))
            out_shapes.append(jax.ShapeDtypeStruct((sd.shape[0], n), F32))
    scratch = []
    if nk > 1:
        scratch += [pltpu.VMEM((tm, tn), F32) for _ in range(nd)]
    if cast_once:
        scratch += [pltpu.VMEM((tn, w.shape[1]) if wt else (w.shape[0], tn), BF16) for w, _ in ws]
    scratch += [pltpu.VMEM((SUBLANES + rows, tn), F32) for _ in range(carry_rows)]

    def body(*refs):
        a_refs = refs[:na]
        w_refs = refs[na:na + nw]
        e_refs = refs[na + nw:na + nw + ne]
        nin = na + nw + ne + nside
        side_refs = refs[nin - nside:nin]
        o_refs = refs[nin:nin + no]
        side_out = refs[nin + no:nin + no + nside]
        s_refs = list(refs[nin + no + nside:])
        acc_refs = [s_refs.pop(0) for _ in range(nd)] if nk > 1 else []
        wb_refs = [s_refs.pop(0) for _ in range(nw)] if cast_once else list(w_refs)
        carry_refs = s_refs
        i, _ = ij(pl.program_id(0), pl.program_id(1))
        k = pl.program_id(2)

        def partial(d, r):
            ai, wi = dots[d]
            a = a_refs[ai][r * rows:(r + 1) * rows, :] if row_split > 1 else a_refs[ai][...]
            if a_fn is not None:
                a = a_fn(a)
            w = wb_refs[wi][...].astype(BF16)
            if wt:
                return lax.dot_general(a.astype(BF16), w, NT_DIMS, preferred_element_type=F32)
            return jnp.dot(a.astype(BF16), w, preferred_element_type=F32)

        if cast_once:
            @pl.when(i == 0)
            def _():
                for wi in range(nw):
                    wb_refs[wi][...] = w_refs[wi][...].astype(BF16)
                for wi in range(nside):
                    wb = wb_refs[wi][...]
                    sd = side_refs[wi][...]
                    side_out[wi][...] = (
                        lax.dot_general(sd, wb, NT_DIMS, preferred_element_type=F32)
                        if wt else jnp.dot(sd, wb, preferred_element_type=F32))

        if nk == 1:
            for r in range(row_split):
                if row_split > 1:
                    piece = pl.ds(r * rows, rows)
                    ev = [e.at[piece] if extras[x][1] == "tile" else e for x, e in enumerate(e_refs)]
                    ov = [o.at[piece] if outs[x][1] == "tile" else o for x, o in enumerate(o_refs)]
                else:
                    ev, ov = e_refs, o_refs
                epilogue([partial(d, r) for d in range(nd)], ev, ov, carry_refs,
                         (i == 0) if r == 0 else False, r == row_split - 1)
        else:
            @pl.when(k == 0)
            def _():
                for d in range(nd):
                    acc_refs[d][...] = jnp.zeros((tm, tn), F32)

            for d in range(nd):
                acc_refs[d][...] += partial(d, 0)

            @pl.when(k == nk - 1)
            def _():
                epilogue([acc_refs[d][...] for d in range(nd)], e_refs, o_refs,
                         carry_refs, i == 0, True)

    return pl.pallas_call(
        body,
        grid=grid,
        in_specs=in_specs,
        out_specs=out_specs,
        out_shape=out_shapes,
        scratch_shapes=scratch,
        compiler_params=_params("arbitrary", "arbitrary", "arbitrary"),
        name=name,
    )(*as_, *[w for w, _ in ws], *[e for e, _, _ in extras], *sides)


def _conv_taps(raw, win_ref, taps, first):
    width = len(taps)
    rows = raw.shape[0]
    if first is not False:
        @pl.when(first)
        def _():
            win_ref[0:SUBLANES, :] = jnp.zeros((SUBLANES, raw.shape[1]), F32)

    win_ref[SUBLANES:SUBLANES + rows, :] = raw
    y = win_ref[SUBLANES:SUBLANES + rows, :] * taps[width - 1]
    for s in range(1, width):
        y = y + win_ref[SUBLANES - s:SUBLANES - s + rows, :] * taps[width - 1 - s]
    win_ref[0:SUBLANES, :] = win_ref[rows:rows + SUBLANES, :]
    return y


def _gdn_gates(ba, alog_row, dtb_row):
    return _sigmoid(ba), -jnp.exp(alog_row) * _softplus(ba + dtb_row)


def _gdn_prep_kernel(h_ref, hs_ref, wba_ref, alog_ref, dtb_ref, beta_ref, gc_ref, gct_ref, bas_ref,
                     *, heads_pad, h_a):
    c = GDN_CHUNK
    wba = wba_ref[...].astype(BF16)

    @pl.when(pl.program_id(0) == 0)
    def _():
        bas_ref[...] = lax.dot_general(hs_ref[...], wba, NT_DIMS, preferred_element_type=F32)

    ba = lax.dot_general(h_ref[...], wba, NT_DIMS, preferred_element_type=F32)
    beta, g = _gdn_gates(ba, alog_ref[...], dtb_ref[...])
    beta_ref[...] = beta
    row = lax.broadcasted_iota(jnp.int32, (c, LANES), 0)
    for q in range(g.shape[0] // c):
        gq = g[q * c:(q + 1) * c, :]
        s = 1
        while s < c:
            gq = gq + jnp.where(row >= s, pltpu.roll(gq, s, axis=0), 0.0)
            s *= 2
        gc_ref[q * c:(q + 1) * c, :] = gq
        gct_ref[:, q * c:(q + 1) * c] = gq.T[h_a:h_a + heads_pad, :]


def _gdn_intra_kernel(q_ref, k_ref, v_ref, beta_ref, gc_ref, gct_ref,
                      uv_ref, w_ref, aqk_ref, qdec_ref, kdect_ref, *, hb, h_a):
    c = GDN_CHUNK
    grp = pl.program_id(1)
    lane = lax.broadcasted_iota(jnp.int32, (c, LANES), 1)
    row = lax.broadcasted_iota(jnp.int32, (c, c), 0)
    col = lax.broadcasted_iota(jnp.int32, (c, c), 1)
    eye = jnp.where(row == col, 1.0, 0.0)
    levels = range(1, int(math.log2(c)))
    below = [((row >> (lvl + 1)) == (col >> (lvl + 1))) & ((row >> lvl) != (col >> lvl))
             for lvl in levels]
    pair = (row >> 1) == (col >> 1)
    beta_all = beta_ref[...]
    gc_all = gc_ref[...]
    heads = range(hb)
    sls = [slice(j * LANES, (j + 1) * LANES) for j in heads]
    beta_c, gcc, kb, mm, t = [], [], [], [], []
    for j in heads:
        h = grp * hb + j
        beta_c.append(jnp.sum(jnp.where(lane == h, beta_all, 0.0), axis=1, keepdims=True))
        gcc.append(jnp.sum(jnp.where(lane == h + h_a, gc_all, 0.0), axis=1, keepdims=True))
        gcr = gct_ref[pl.ds(h, 1), :]
        dec_incl = jnp.exp(jnp.where(row >= col, gcc[j] - gcr, NEG_BIG))
        k = k_ref[:, sls[j]]
        kb.append(k.astype(F32) * beta_c[j])
        mm.append(lax.dot_general(kb[j].astype(BF16), k, NT_DIMS, preferred_element_type=F32)
                  * jnp.where(row > col, dec_incl, 0.0))
        aqk = lax.dot_general(q_ref[:, sls[j]], k, NT_DIMS, preferred_element_type=F32) * dec_incl
        aqk_ref[:, sls[j]] = aqk.astype(BF16)
        t.append(eye - jnp.where(pair, mm[j], 0.0))
    for msk in below:
        tb = [t[j].astype(BF16) for j in heads]
        bt = [jnp.dot(jnp.where(msk, mm[j], 0.0).astype(BF16), tb[j], preferred_element_type=F32)
              for j in heads]
        t = [t[j] - jnp.dot(tb[j], bt[j].astype(BF16), preferred_element_type=F32) for j in heads]
    for j in heads:
        tb = t[j].astype(BF16)
        egc = jnp.exp(gcc[j])
        gl = gcc[j][c - 1:c, :]
        vb = v_ref[:, sls[j]].astype(F32) * beta_c[j]
        uv_ref[:, sls[j]] = jnp.dot(tb, vb.astype(BF16), preferred_element_type=F32)
        w_ref[:, sls[j]] = jnp.dot(tb, (kb[j] * egc).astype(BF16),
                                   preferred_element_type=F32).astype(BF16)
        qdec_ref[:, sls[j]] = (q_ref[:, sls[j]].astype(F32) * egc).astype(BF16)
        kdect_ref[sls[j], :] = (k_ref[:, sls[j]].astype(F32) * jnp.exp(gl - gcc[j])).T.astype(BF16)


def _gdn_rec_kernel(uv_ref, w_ref, aqk_ref, qdec_ref, kdect_ref, gct_ref, za_ref, gn_ref,
                    o_ref, sout_ref, s_ref, *, hb):
    c = GDN_CHUNK
    n = pl.program_id(1)
    g0 = pl.program_id(0)

    @pl.when(n == 0)
    def _():
        s_ref[...] = jnp.zeros(s_ref.shape, F32)

    heads = range(hb)
    sls = [slice(j * LANES, (j + 1) * LANES) for j in heads]
    s = [s_ref[j] for j in heads]
    sb = [s[j].astype(BF16) for j in heads]
    ub = [(uv_ref[:, sls[j]] - jnp.dot(w_ref[:, sls[j]], sb[j], preferred_element_type=F32)
           ).astype(BF16) for j in heads]
    oq = [jnp.dot(qdec_ref[:, sls[j]], sb[j], preferred_element_type=F32) for j in heads]
    for j in heads:
        gl = jnp.exp(gct_ref[pl.ds(g0 * hb + j, 1), c - 1:c])
        s_ref[j] = gl * s[j] + jnp.dot(kdect_ref[sls[j], :], ub[j], preferred_element_type=F32)
    for j in heads:
        o = oq[j] + jnp.dot(aqk_ref[:, sls[j]], ub[j], preferred_element_type=F32)
        ms = jnp.mean(o * o, axis=-1, keepdims=True)
        o_ref[:, sls[j]] = (o * lax.rsqrt(ms + EPS) * gn_ref[...]
                            * za_ref[:, sls[j]].astype(F32)).astype(o_ref.dtype)

    @pl.when(n == pl.num_programs(1) - 1)
    def _():
        sout_ref[...] = s_ref[...]


def _gate_rows(a_log, dt_bias, h_a):
    alog_row = jnp.zeros((1, LANES), F32).at[0, h_a:2 * h_a].set(a_log)
    dtb_row = jnp.zeros((1, LANES), F32).at[0, h_a:2 * h_a].set(dt_bias)
    return alog_row, dtb_row


def _gdn_prompt(qkv, h1, h1s, w_t, c_ba, za_silu, a_log, dt_bias, gnorm, h_a, dk):
    l = qkv.shape[0]
    kdim = h1.shape[1]
    c = GDN_CHUNK
    assert l % c == 0 and dk == LANES and 2 * h_a <= LANES and c_ba % LANES == 0
    nchunk = l // c
    hp = max(SUBLANES, -(-h_a // SUBLANES) * SUBLANES)
    alog_row, dtb_row = _gate_rows(a_log, dt_bias, h_a)

    ms = h1s.shape[0]
    rp = _tile(l, TILES["prep_rows"], c)
    beta, gc, gct, ba_s = pl.pallas_call(
        functools.partial(_gdn_prep_kernel, heads_pad=hp, h_a=h_a),
        grid=(l // rp,),
        in_specs=[pl.BlockSpec((rp, kdim), lambda n: (n, 0)),
                  pl.BlockSpec((ms, kdim), lambda n: (0, 0)),
                  pl.BlockSpec((LANES, kdim), lambda n: (c_ba // LANES, 0)),
                  pl.BlockSpec((1, LANES), lambda n: (0, 0)),
                  pl.BlockSpec((1, LANES), lambda n: (0, 0))],
        out_specs=[pl.BlockSpec((rp, LANES), lambda n: (n, 0)),
                   pl.BlockSpec((rp, LANES), lambda n: (n, 0)),
                   pl.BlockSpec((hp, rp), lambda n: (0, n)),
                   pl.BlockSpec((ms, LANES), lambda n: (0, 0))],
        out_shape=[jax.ShapeDtypeStruct((l, LANES), F32),
                   jax.ShapeDtypeStruct((l, LANES), F32),
                   jax.ShapeDtypeStruct((hp, l), F32),
                   jax.ShapeDtypeStruct((ms, LANES), F32)],
        compiler_params=_params("arbitrary"),
        name="gdn_prep",
    )(h1, h1s, w_t, alog_row, dtb_row)

    hd = h_a * dk
    hb = _tile(h_a, TILES["gdn_heads"], 1)
    ng = h_a // hb
    blk = lambda off: pl.BlockSpec((c, hb * LANES), lambda n, g: (n, g + off))
    uv, w, aqk, qdec, kdect = pl.pallas_call(
        functools.partial(_gdn_intra_kernel, hb=hb, h_a=h_a),
        grid=(nchunk, ng),
        in_specs=[blk(0), blk(ng), blk(2 * ng),
                  pl.BlockSpec((c, LANES), lambda n, g: (n, 0)),
                  pl.BlockSpec((c, LANES), lambda n, g: (n, 0)),
                  pl.BlockSpec((hp, c), lambda n, g: (0, n))],
        out_specs=[blk(0), blk(0), blk(0), blk(0),
                   pl.BlockSpec((hb * LANES, c), lambda n, g: (g, n))],
        out_shape=[jax.ShapeDtypeStruct((l, hd), F32),
                   jax.ShapeDtypeStruct((l, hd), BF16),
                   jax.ShapeDtypeStruct((l, hd), BF16),
                   jax.ShapeDtypeStruct((l, hd), BF16),
                   jax.ShapeDtypeStruct((hd, l), BF16)],
        compiler_params=_params("parallel", "arbitrary"),
        name="gdn_intra",
    )(qkv, qkv, qkv, beta, gc, gct)

    hr = hb
    wide = lambda: pl.BlockSpec((c, hr * LANES), lambda g, n: (n, g))
    oa, s_fin = pl.pallas_call(
        functools.partial(_gdn_rec_kernel, hb=hr),
        grid=(h_a // hr, nchunk),
        in_specs=[wide(), wide(), wide(), wide(),
                  pl.BlockSpec((hr * LANES, c), lambda g, n: (g, n)),
                  pl.BlockSpec((hp, c), lambda g, n: (0, n)),
                  wide(),
                  pl.BlockSpec((1, LANES), lambda g, n: (0, 0))],
        out_specs=[wide(), pl.BlockSpec((hr, dk, LANES), lambda g, n: (g, 0, 0))],
        out_shape=[jax.ShapeDtypeStruct((l, hd), BF16),
                   jax.ShapeDtypeStruct((h_a, dk, LANES), F32)],
        scratch_shapes=[pltpu.VMEM((hr, dk, LANES), F32)],
        compiler_params=_params("parallel", "arbitrary"),
        name="gdn_rec",
    )(uv, w, aqk, qdec, kdect, gct, za_silu, gnorm)
    return oa, s_fin, ba_s


def _gdn_sample_kernel(zq_ref, za_ref, ba_ref, buf_ref, s_ref, cw_ref, alog_ref, dtb_ref, gn_ref,
                       o_ref, snew_ref, bufnew_ref, *, h_a, dk, width):
    hd = h_a * dk
    raw = zq_ref[0]
    buf = buf_ref[0]
    y = raw * cw_ref[width - 1:width, :]
    for j in range(width - 1):
        y = y + buf[j:j + 1, :] * cw_ref[j:j + 1, :]
    y = _silu(y)
    bufnew_ref[0, 0:width - 2, :] = buf[1:width - 1, :]
    bufnew_ref[0, width - 2:width - 1, :] = raw
    za = za_ref[0]
    beta_row, g_row = _gdn_gates(ba_ref[0], alog_ref[...], dtb_ref[...])
    lane1 = lax.broadcasted_iota(jnp.int32, (1, LANES), 1)
    row = lax.broadcasted_iota(jnp.int32, (dk, LANES), 0)
    col = lax.broadcasted_iota(jnp.int32, (dk, LANES), 1)
    eye = row == col
    for h in range(h_a):
        q = y[:, h * dk:(h + 1) * dk]
        k = y[:, hd + h * dk:hd + (h + 1) * dk]
        v = y[:, 2 * hd + h * dk:2 * hd + (h + 1) * dk]
        q = q * lax.rsqrt(jnp.sum(q * q, axis=-1, keepdims=True) + EPS) * (dk ** -0.5)
        k = k * lax.rsqrt(jnp.sum(k * k, axis=-1, keepdims=True) + EPS)
        beta = jnp.sum(jnp.where(lane1 == h, beta_row, 0.0), axis=1, keepdims=True)
        a = jnp.exp(jnp.sum(jnp.where(lane1 == h + h_a, g_row, 0.0), axis=1, keepdims=True))
        k_col = jnp.sum(jnp.where(eye, k, 0.0), axis=1, keepdims=True)
        q_col = jnp.sum(jnp.where(eye, q, 0.0), axis=1, keepdims=True)
        s = a * s_ref[0, h]
        u = beta * (v - jnp.sum(s * k_col, axis=0, keepdims=True))
        s_new = s + k_col * u
        snew_ref[0, h] = s_new
        o = jnp.sum(s_new * q_col, axis=0, keepdims=True)
        ms = jnp.mean(o * o, axis=-1, keepdims=True)
        o_ref[0, :, h * dk:(h + 1) * dk] = (
            o * lax.rsqrt(ms + EPS) * gn_ref[...] * _silu(za[:, h * dk:(h + 1) * dk])
        ).astype(o_ref.dtype)


def _gdn_sample(zq, za, ba, buf, s0, conv_w, a_log, dt_bias, gnorm, h_a, dk):
    b = zq.shape[0]
    width = conv_w.shape[0]
    hd = h_a * dk
    alog_row, dtb_row = _gate_rows(a_log, dt_bias, h_a)
    per_seq = lambda n: pl.BlockSpec((1, 1, n), lambda i: (i, 0, 0))
    return pl.pallas_call(
        functools.partial(_gdn_sample_kernel, h_a=h_a, dk=dk, width=width),
        grid=(b,),
        in_specs=[per_seq(3 * hd), per_seq(hd), per_seq(LANES),
                  pl.BlockSpec((1, width - 1, 3 * hd), lambda i: (i, 0, 0)),
                  pl.BlockSpec((1, h_a, dk, LANES), lambda i: (i, 0, 0, 0)),
                  pl.BlockSpec((width, 3 * hd), lambda i: (0, 0)),
                  pl.BlockSpec((1, LANES), lambda i: (0, 0)),
                  pl.BlockSpec((1, LANES), lambda i: (0, 0)),
                  pl.BlockSpec((1, LANES), lambda i: (0, 0))],
        out_specs=[pl.BlockSpec((1, 1, hd), lambda i: (i, 0, 0)),
                   pl.BlockSpec((1, h_a, dk, LANES), lambda i: (i, 0, 0, 0)),
                   pl.BlockSpec((1, width - 1, 3 * hd), lambda i: (i, 0, 0))],
        out_shape=[jax.ShapeDtypeStruct((b, 1, hd), BF16),
                   jax.ShapeDtypeStruct((b, h_a, dk, LANES), F32),
                   jax.ShapeDtypeStruct((b, width - 1, 3 * hd), F32)],
        compiler_params=_params("parallel"),
        name="gdn_sample",
    )(zq[:, None, :], za[:, None, :], ba[:, None, :], buf, s0, conv_w, alog_row, dtb_row, gnorm)


def _diff_lambda(lamp_ref, lam_init):
    lp = lamp_ref[...]
    e1 = jnp.exp(jnp.sum(lp[0:1] * lp[1:2], axis=1, keepdims=True))
    e2 = jnp.exp(jnp.sum(lp[2:3] * lp[3:4], axis=1, keepdims=True))
    return e1 - e2 + lam_init


def _attn_prompt_kernel(qi_ref, kj_ref, q_ref, k_ref, v_ref, lamp_ref, g_ref, o_ref,
                        m_ref, l_ref, acc_ref, *, dq, lam_init):
    p = pl.program_id(1)
    qi = qi_ref[p]
    kj = kj_ref[p]

    @pl.when(kj == 0)
    def _():
        m_ref[...] = jnp.full(m_ref.shape, NEG_BIG, F32)
        l_ref[...] = jnp.zeros(l_ref.shape, F32)
        acc_ref[...] = jnp.zeros(acc_ref.shape, F32)

    def step(masked):
        v = v_ref[...]
        for c in range(2):
            s = lax.dot_general(q_ref[:, c * dq:(c + 1) * dq], k_ref[:, c * dq:(c + 1) * dq],
                                NT_DIMS, preferred_element_type=F32)
            if masked:
                row = lax.broadcasted_iota(jnp.int32, s.shape, 0)
                col = lax.broadcasted_iota(jnp.int32, s.shape, 1)
                s = jnp.where(row >= col, s, NEG_BIG)
            m_prev = m_ref[c]
            m_new = jnp.maximum(m_prev, jnp.max(s, axis=1, keepdims=True))
            alpha = jnp.exp2(m_prev - m_new)
            pm = jnp.exp2(s - m_new)
            l_ref[c] = alpha * l_ref[c] + jnp.sum(pm, axis=1, keepdims=True)
            acc_ref[c] = alpha * acc_ref[c] + jnp.dot(pm.astype(BF16), v,
                                                      preferred_element_type=F32)
            m_ref[c] = m_new

    @pl.when(kj < qi)
    def _():
        step(False)

    @pl.when(kj == qi)
    def _():
        step(True)
        _attn_finish(acc_ref, l_ref[0], l_ref[1], lamp_ref, g_ref, o_ref, lam_init)


def _attn_finish(acc_ref, l0, l1, lamp_ref, g_ref, o_ref, lam_init):
    lam = _diff_lambda(lamp_ref, lam_init)
    o = acc_ref[0] / l0 - lam * (acc_ref[1] / l1)
    ms = jnp.mean(o * o, axis=-1, keepdims=True)
    o_ref[...] = (o * lax.rsqrt(ms + SUBLN_EPS) * g_ref[...] * (1.0 - lam_init)).astype(o_ref.dtype)


def _attn_prompt_bounded_kernel(qi_ref, kj_ref, q_ref, k_ref, v_ref, lamp_ref, g_ref, o_ref,
                                l_ref, acc_ref, *, dq, lam_init):
    p = pl.program_id(1)
    qi = qi_ref[p]
    kj = kj_ref[p]

    @pl.when(kj == 0)
    def _():
        l_ref[...] = jnp.zeros(l_ref.shape, F32)
        acc_ref[...] = jnp.zeros(acc_ref.shape, F32)

    t = q_ref.shape[0]

    def accumulate(c, r0, nr, k0, nk, masked):
        s = lax.dot_general(q_ref[r0:r0 + nr, c * dq:(c + 1) * dq],
                            k_ref[k0:k0 + nk, c * dq:(c + 1) * dq], NT_DIMS,
                            preferred_element_type=F32)
        pm = jnp.exp2(s)
        if masked:
            row = lax.broadcasted_iota(jnp.int32, s.shape, 0) + r0
            col = lax.broadcasted_iota(jnp.int32, s.shape, 1) + k0
            pm = jnp.where(row >= col, pm, 0.0)
        part = pm[:, 0:LANES]
        for g in range(1, nk // LANES):
            part = part + pm[:, g * LANES:(g + 1) * LANES]
        l_ref[c, r0:r0 + nr, :] += part
        acc_ref[c, r0:r0 + nr, :] += jnp.dot(pm.astype(BF16), v_ref[k0:k0 + nk, :],
                                             preferred_element_type=F32)

    @pl.when(kj < qi)
    def _():
        for c in range(2):
            accumulate(c, 0, t, 0, t, False)

    @pl.when(kj == qi)
    def _():
        half = t // 2 if t % (2 * LANES) == 0 else t
        for c in range(2):
            accumulate(c, 0, half, 0, half, True)
            if half < t:
                accumulate(c, half, t - half, 0, t, True)
        _attn_finish(acc_ref, jnp.sum(l_ref[0], axis=1, keepdims=True),
                     jnp.sum(l_ref[1], axis=1, keepdims=True), lamp_ref, g_ref, o_ref, lam_init)


def _attn_prompt(qkv, score_bound, lam_params, subln_g, h_b, dq, lam_init):
    l = qkv.shape[0]
    dv = 2 * dq
    q = k = v = qkv
    t = _tile(l, TILES["attn"], LANES)
    nb = l // t
    pairs = [(i, j) for i in range(nb) for j in range(i + 1)]
    qi_tab = jnp.asarray([a for a, _ in pairs], jnp.int32)
    kj_tab = jnp.asarray([b for _, b in pairs], jnp.int32)

    def call(body, scratch, name):
        grid_spec = pltpu.PrefetchScalarGridSpec(
            num_scalar_prefetch=2,
            grid=(h_b, len(pairs)),
            in_specs=[pl.BlockSpec((t, dv), lambda h, p, qi, kj: (qi[p], h)),
                      pl.BlockSpec((t, dv), lambda h, p, qi, kj: (kj[p], h + h_b)),
                      pl.BlockSpec((t, dv), lambda h, p, qi, kj: (kj[p], h + 2 * h_b)),
                      pl.BlockSpec((4, dq), lambda h, p, qi, kj: (0, 0)),
                      pl.BlockSpec((1, dv), lambda h, p, qi, kj: (0, 0))],
            out_specs=pl.BlockSpec((t, dv), lambda h, p, qi, kj: (qi[p], h)),
            scratch_shapes=scratch,
        )
        return pl.pallas_call(
            functools.partial(body, dq=dq, lam_init=lam_init),
            grid_spec=grid_spec,
            out_shape=jax.ShapeDtypeStruct((l, h_b * dv), BF16),
            compiler_params=_params("parallel", "arbitrary"),
            name=name,
        )(qi_tab, kj_tab, q, k, v, lam_params, subln_g)

    bounded = lambda: call(_attn_prompt_bounded_kernel,
                           [pltpu.VMEM((2, t, LANES), F32), pltpu.VMEM((2, t, dv), F32)],
                           "attn_prompt_bounded")
    online = lambda: call(_attn_prompt_kernel,
                          [pltpu.VMEM((2, t, 1), F32), pltpu.VMEM((2, t, 1), F32),
                           pltpu.VMEM((2, t, dv), F32)], "attn_prompt_online")
    return lax.cond(score_bound <= MAX_UNSHIFTED_SCORE, bounded, online)


def _attn_decode_kernel(pt_ref, q_ref, kn_ref, vn_ref, *rest, h_b, dq, lam_init, pp):
    del pt_ref
    kc_refs = rest[:pp]
    vc_refs = rest[pp:2 * pp]
    lamp_ref, g_ref, o_ref, m_ref, l_ref, acc_ref = rest[2 * pp:]
    p = pl.program_id(1)
    dv = 2 * dq
    page = kc_refs[0].shape[2]
    assert h_b & (h_b - 1) == 0
    qm = q_ref[0]

    @pl.when(p == 0)
    def _():
        m_ref[...] = jnp.sum(qm * kn_ref[0], axis=1, keepdims=True)
        l_ref[...] = jnp.ones(l_ref.shape, F32)
        acc_ref[...] = vn_ref[0]

    qb = qm.astype(BF16)
    scores = []
    for j in range(pp):
        k2 = kc_refs[j][0, 0].reshape(page * h_b, dv).astype(BF16)
        s = lax.dot_general(qb, k2, NT_DIMS, preferred_element_type=F32)
        sub = lax.broadcasted_iota(jnp.int32, s.shape, 0)
        lane = lax.broadcasted_iota(jnp.int32, s.shape, 1)
        scores.append(jnp.where((lane & (h_b - 1)) == (sub >> 1), s, NEG_BIG))
    m_prev = m_ref[...]
    m_new = m_prev
    for s in scores:
        m_new = jnp.maximum(m_new, jnp.max(s, axis=1, keepdims=True))
    alpha = jnp.exp(m_prev - m_new)
    l_new = alpha * l_ref[...]
    acc = alpha * acc_ref[...]
    for j in range(pp):
        pm = jnp.exp(scores[j] - m_new)
        l_new = l_new + jnp.sum(pm, axis=1, keepdims=True)
        v2 = vc_refs[j][0, 0].reshape(page * h_b, dv).astype(BF16)
        acc = acc + jnp.dot(pm.astype(BF16), v2, preferred_element_type=F32)
    l_ref[...] = l_new
    acc_ref[...] = acc
    m_ref[...] = m_new

    @pl.when(p == pl.num_programs(1) - 1)
    def _():
        lam = _diff_lambda(lamp_ref, lam_init)
        for h in range(h_b):
            o1 = acc_ref[2 * h:2 * h + 1, :] / l_ref[2 * h:2 * h + 1, :]
            o2 = acc_ref[2 * h + 1:2 * h + 2, :] / l_ref[2 * h + 1:2 * h + 2, :]
            o = o1 - lam * o2
            ms = jnp.mean(o * o, axis=-1, keepdims=True)
            o_ref[0, :, h * dv:(h + 1) * dv] = (o * lax.rsqrt(ms + SUBLN_EPS) * g_ref[...]
                                                * (1.0 - lam_init)).astype(o_ref.dtype)


def _attn_decode(qn, kn, vn, cache_k, cache_v, page_table, lam_params, subln_g, h_b, dq,
                 lam_init, lyr):
    b = qn.shape[0]
    n_pages = page_table.shape[1]
    page = cache_k.shape[2]
    dv = 2 * dq
    nsub = 2 * h_b
    pp = _tile(n_pages, TILES["decode_pages"], 1)
    zeros = jnp.zeros((b, h_b, dq), F32)

    def sub_rows(x):
        x4 = x.reshape(b, h_b, 2, dq)
        return jnp.stack([jnp.concatenate([x4[:, :, 0], zeros], axis=-1),
                          jnp.concatenate([zeros, x4[:, :, 1]], axis=-1)], axis=2).reshape(b, nsub, dv)

    qm = sub_rows(qn)
    km = sub_rows(kn)
    vm = jnp.repeat(vn.reshape(b, h_b, 1, dv), 2, axis=2).reshape(b, nsub, dv)
    rows = lambda: pl.BlockSpec((1, nsub, dv), lambda i, p, pt: (i, 0, 0))

    def pages(j):
        return pl.BlockSpec((1, 1, page, h_b, dv), lambda i, p, pt: (lyr, pt[i, p * pp + j], 0, 0, 0))

    grid_spec = pltpu.PrefetchScalarGridSpec(
        num_scalar_prefetch=1,
        grid=(b, n_pages // pp),
        in_specs=([rows(), rows(), rows()] + [pages(j) for j in range(pp)]
                  + [pages(j) for j in range(pp)]
                  + [pl.BlockSpec((4, dq), lambda i, p, pt: (0, 0)),
                     pl.BlockSpec((1, dv), lambda i, p, pt: (0, 0))]),
        out_specs=pl.BlockSpec((1, 1, h_b * dv), lambda i, p, pt: (i, 0, 0)),
        scratch_shapes=[pltpu.VMEM((nsub, 1), F32), pltpu.VMEM((nsub, 1), F32),
                        pltpu.VMEM((nsub, dv), F32)],
    )
    return pl.pallas_call(
        functools.partial(_attn_decode_kernel, h_b=h_b, dq=dq, lam_init=lam_init, pp=pp),
        grid_spec=grid_spec,
        out_shape=jax.ShapeDtypeStruct((b, 1, h_b * dv), BF16),
        compiler_params=_params("parallel", "arbitrary"),
        name="attn_decode",
    )(page_table, qm, km, vm, *([cache_k] * pp), *([cache_v] * pp), lam_params, subln_g)


def _ep_bias(accs, e, o, carry, first, last):
    o[0][...] = accs[0] + e[0][...]


def _ep_silu(accs, e, o, carry, first, last):
    o[0][...] = _silu(accs[0]).astype(o[0].dtype)


def _ep_sigmoid(accs, e, o, carry, first, last):
    o[0][...] = _sigmoid(accs[0]).astype(o[0].dtype)


def _group_norm_store(y, gain, eps, out_refs):
    tn = y.shape[1]
    for g in range(tn // LANES):
        sl = slice(g * LANES, (g + 1) * LANES)
        blk = y[:, sl]
        ms = jnp.sum(blk * blk, axis=-1, keepdims=True) / LANES
        val = blk * lax.rsqrt(ms + eps) * gain[:, sl]
        for r in out_refs:
            r[:, sl] = val.astype(r.dtype)


def _ep_qkv_b(accs, e, o, carry, first, last):
    y = accs[0]
    gain = e[0][...]
    nflag = e[1][...]
    for g in range(y.shape[1] // LANES):
        sl = slice(g * LANES, (g + 1) * LANES)
        blk = y[:, sl]
        ms = jnp.sum(blk * blk, axis=-1, keepdims=True) / LANES
        val = blk * (jnp.where(nflag[:, sl] > 0.5, lax.rsqrt(ms + EPS), 1.0) * gain[:, sl])
        for r in o:
            r[:, sl] = val.astype(r.dtype)


def _sample_post_kernel(q_ref, k_ref, ga_ref, gb_ref, gq_ref, gk_ref, qo_ref, ko_ref, sa_ref, sb_ref):
    _group_norm_store(q_ref[...], gq_ref[...], EPS, [qo_ref])
    _group_norm_store(k_ref[...], gk_ref[...], EPS, [ko_ref])
    sa_ref[...] = _sigmoid(ga_ref[...])
    sb_ref[...] = _sigmoid(gb_ref[...])


def _ep_gdn_conv(accs, e, o, carry, first, last, *, width):
    raw = accs[0]
    taps = [e[j][...] for j in range(width)]
    nflag = e[width][...]
    gain = e[width + 1][...]
    val = _silu(_conv_taps(raw, carry[0], taps, first))
    for g in range(val.shape[1] // LANES):
        sl = slice(g * LANES, (g + 1) * LANES)
        blk = val[:, sl]
        ss = jnp.sum(blk * blk, axis=-1, keepdims=True)
        scale = jnp.where(nflag[:, sl] > 0.5, lax.rsqrt(ss + EPS), 1.0) * gain[:, sl]
        o[0][:, sl] = (blk * scale).astype(o[0].dtype)
    if last:
        o[1][...] = carry[0][0:SUBLANES, :]


def _ep_branch(accs, e, o, carry, first, last):
    o[0][...] = (e[0][...].astype(F32) * accs[0] + e[1][...].astype(F32) * accs[1]).astype(o[0].dtype)


def _ep_residual(accs, e, o, carry, first, last):
    o[0][...] = e[0][...] + e[1][...] * accs[0]


def _ep_ffn_prompt(accs, e, o, carry, first, last, *, width):
    outs = []
    for d in range(2):
        taps = [e[d * (width + 1) + j][...] for j in range(width)]
        bias = e[d * (width + 1) + width][...]
        outs.append(_conv_taps(accs[d], carry[d], taps, first) + bias)
        if last:
            o[1 + d][...] = carry[d][0:SUBLANES, :]
    o[0][...] = (_silu(outs[0]) * outs[1]).astype(o[0].dtype)


def _sample_mix_kernel(ra_ref, rb_ref, ga_ref, gb_ref, o_ref):
    o_ref[...] = (ga_ref[...] * ra_ref[...] + gb_ref[...] * rb_ref[...]).astype(o_ref.dtype)


def _resid_normmod_kernel(x_ref, r_ref, gate_ref, g_ref, sc_ref, sh_ref, x2_ref, h_ref):
    x2 = x_ref[...] + gate_ref[...] * r_ref[...]
    x2_ref[...] = x2
    ms = jnp.mean(x2 * x2, axis=-1, keepdims=True)
    h = x2 * lax.rsqrt(ms + EPS) * g_ref[...]
    h_ref[...] = (h * (1.0 + sc_ref[...]) + sh_ref[...]).astype(h_ref.dtype)


def _ffn_sample_kernel(*refs, width):
    ns = width - 1
    ups = refs[0:2]
    states = (refs[2:2 + ns], refs[2 + ns:2 + 2 * ns])
    taps = refs[2 + 2 * ns:4 + 2 * ns]
    bias = refs[4 + 2 * ns:6 + 2 * ns]
    o_ref = refs[6 + 2 * ns]
    vals = []
    for d in range(2):
        y = ups[d][...] * taps[d][width - 1:width, :]
        for j in range(ns):
            y = y + states[d][j][...] * taps[d][j:j + 1, :]
        vals.append(y + bias[d][...])
    o_ref[...] = (_silu(vals[0]) * vals[1]).astype(o_ref.dtype)


def kernel(x_prompt, x_sample, c_prompt, c_sample, cache_k, cache_v, state_gdn, state_gdn_conv, state_ffn_conv, page_table, w_ada, b_ada, norm1_g, norm2_g, w_in, gdn_conv_w, gdn_a_log, gdn_dt_bias, gdn_norm_g, diff_q_norm_g, diff_k_norm_g, diff_lambda, diff_subln_g, w_branch_a, w_branch_b, w_o, w_up, ffn_conv_w, ffn_conv_b, w_down):
    depth = w_in.shape[0]
    assert depth == 1 and x_prompt.shape[0] == 1 and x_sample.shape[1] == 1
    lyr = 0
    lam_init = 0.8 - 0.6 * math.exp(-0.3 * lyr)
    d = x_prompt.shape[-1]
    l = x_prompt.shape[1]
    nb = x_sample.shape[0]
    h_a, dk, dva = state_gdn.shape[2:]
    h_b = cache_k.shape[3]
    dq = cache_k.shape[4] // 2
    dvb = cache_v.shape[4]
    assert dk == LANES and dva == LANES and dq == LANES and dvb == 2 * dq
    d_ff = w_down.shape[1]
    gw = gdn_conv_w.shape[1]
    fw = ffn_conv_w.shape[1]
    hd_a = h_a * dk
    hd_b = h_b * dvb

    w_t = jnp.swapaxes(w_in[lyr], 0, 1)
    c_za = 3 * hd_a
    c_ba = c_za + hd_a
    c_qb = c_ba + 2 * h_a
    c_ga = c_qb + 3 * hd_b
    assert c_ga + 2 * d == w_t.shape[0] and 2 * h_a <= LANES
    w_up0 = w_up[lyr]
    w_dn = w_down[lyr].astype(BF16)
    w_o_b = w_o[lyr]
    w_ba_b = w_branch_a[lyr]
    w_bb_b = w_branch_b[lyr]

    row = lambda v: v.reshape(1, -1).astype(F32)
    tile_rows = lambda v, reps: jnp.tile(row(v), (1, reps))

    c_all = jnp.concatenate([c_prompt, c_sample], axis=0).astype(F32)
    mp = -(-c_all.shape[0] // 16) * 16
    c_all = jnp.pad(c_all, ((0, mp - c_all.shape[0]), (0, 0)))
    (mod,) = _matmul("adaln", [c_all], [(w_ada[lyr], 0)], [(0, 0)], 6 * d, _ep_bias, [(F32, "tile")],
                     tm=mp, tn=_tile(6 * d, 512, LANES), extras=[(row(b_ada[lyr]), "row", 0)],
                     a_fn=_silu)
    mod_p = [mod[0:1, j * d:(j + 1) * d] for j in range(6)]
    mod_s = [mod[1:1 + nb, j * d:(j + 1) * d] for j in range(6)]

    gq = tile_rows(diff_q_norm_g[lyr], 2 * h_b) * (dq ** -0.5)
    gk = tile_rows(diff_k_norm_g[lyr], 2 * h_b)
    gq2 = gq * math.log2(math.e)
    score_bound = dq * jnp.max(jnp.abs(gq2)) * jnp.max(jnp.abs(gk))
    cw = gdn_conv_w[lyr].astype(F32)
    conv_nflag = jnp.concatenate([jnp.ones((1, 2 * hd_a), F32), jnp.zeros((1, hd_a), F32)], axis=1)
    conv_gain = jnp.concatenate([jnp.full((1, hd_a), dk ** -0.5, F32), jnp.ones((1, 2 * hd_a), F32)], axis=1)
    gnorm = row(gdn_norm_g[lyr])
    subln = row(diff_subln_g[lyr])
    lamp = diff_lambda[lyr].astype(F32)
    fcw = ffn_conv_w[lyr].astype(F32)
    fcb = row(ffn_conv_b[lyr])
    a_log = gdn_a_log[lyr].astype(F32)
    dt_bias = gdn_dt_bias[lyr].astype(F32)

    xp = x_prompt[0].astype(F32)
    tm = _tile(l, TILES["mm_m"], 16)
    tn = TILES["mm_n"]
    split = 4 if tm % 64 == 0 else 1
    split_ffn = split
    h1 = _normmod(xp, row(norm1_g[lyr]), mod_p[1], mod_p[0])
    xs = x_sample[:, 0, :].astype(F32)
    h1s = _normmod(xs, row(norm1_g[lyr]), mod_s[1], mod_s[0])

    stationary = dict(tm=tm, order="nm", cast_once=True, row_split=split, wt=True, side=h1s)
    qkv_c, qkv_tail, zq_s = _matmul(
        "p_qkv_conv", [h1], [(w_t, 0)], [(0, 0)], 3 * hd_a,
        functools.partial(_ep_gdn_conv, width=gw), [(BF16, "tile"), (F32, "tail")],
        tn=_tile(hd_a, tn, LANES), carry_rows=1, **stationary,
        extras=[(cw[j:j + 1], "row", 0) for j in range(gw)] + [(conv_nflag, "row", 0), (conv_gain, "row", 0)])
    za_p, za_s = _matmul("p_za", [h1], [(w_t, c_za)], [(0, 0)], hd_a, _ep_silu, [(BF16, "tile")],
                         tn=_tile(hd_a, tn, LANES), **stationary)
    tn_b = _tile(hd_b, tn, LANES)
    gain_b = jnp.concatenate([gq2, gk, jnp.ones((1, hd_b), F32)], axis=1)
    nflag_b = jnp.concatenate([jnp.ones((1, 2 * hd_b), F32), jnp.zeros((1, hd_b), F32)], axis=1)
    qkvb_p, qkvb_p16, qkvb_s = _matmul(
        "p_qkv_b", [h1], [(w_t, c_qb)], [(0, 0)], 3 * hd_b, _ep_qkv_b,
        [(F32, "tile"), (BF16, "tile")], tn=tn_b,
        extras=[(gain_b, "row", 0), (nflag_b, "row", 0)], **stationary)
    kb_p, vb_p = qkvb_p[:, hd_b:2 * hd_b], qkvb_p[:, 2 * hd_b:]
    qb_s, kb_s, vb_s = qkvb_s[:, :hd_b], qkvb_s[:, hd_b:2 * hd_b], qkvb_s[:, 2 * hd_b:]
    tn_d = _tile(d, tn, LANES)
    gates_p, gates_s = _matmul("p_gates", [h1], [(w_t, c_ga)], [(0, 0)], 2 * d, _ep_sigmoid,
                               [(BF16, "tile")], tn=tn_d, **stationary)

    oa_p, s_p, ba_s = _gdn_prompt(qkv_c, h1, h1s, w_t, c_ba, za_p, a_log, dt_bias, gnorm, h_a, dk)
    ob_p = _attn_prompt(qkvb_p16, score_bound, lamp, subln, h_b, dq, lam_init)

    tns = TILES["small_n"]
    oa_s, s_s, gbuf_s = _gdn_sample(zq_s, za_s, ba_s, state_gdn_conv[lyr].astype(F32),
                                    state_gdn[lyr].astype(F32), cw, a_log, dt_bias, gnorm, h_a, dk)
    qn_s, kn_s, sga_s, sgb_s = pl.pallas_call(
        _sample_post_kernel,
        out_shape=[jax.ShapeDtypeStruct((nb, hd_b), F32), jax.ShapeDtypeStruct((nb, hd_b), F32),
                   jax.ShapeDtypeStruct((nb, d), F32), jax.ShapeDtypeStruct((nb, d), F32)],
        name="sample_post",
    )(qb_s, kb_s, gates_s[:, :d], gates_s[:, d:], gq, gk)
    ob_s = _attn_decode(qn_s, kn_s, vb_s, cache_k, cache_v, page_table.astype(jnp.int32), lamp, subln,
                        h_b, dq, lam_init, lyr)
    tn_ds = _tile(d, tns, LANES)

    mixed_p, ra_s, rb_s = _matmul(
        "p_branch", [oa_p, ob_p], [(w_ba_b, 0), (w_bb_b, 0)], [(0, 0), (1, 1)], d,
        _ep_branch, [(BF16, "tile")], tm=tm, tn=tn_d, row_split=split, order="nm", cast_once=True,
        extras=[(gates_p, "tile", 0), (gates_p, "tile", d)], side=[oa_s[:, 0, :], ob_s[:, 0, :]])
    mixed_s = pl.pallas_call(
        _sample_mix_kernel, out_shape=jax.ShapeDtypeStruct((nb, d), BF16), name="s_mix",
    )(ra_s, rb_s, sga_s, sgb_s)
    x2_p, ro_s = _matmul(
        "p_wo", [mixed_p], [(w_o_b, 0)], [(0, 0)], d, _ep_residual, [(F32, "tile")],
        tm=tm, tn=tn_d, row_split=split, order="nm", cast_once=True,
        extras=[(xp, "tile", 0), (mod_p[2], "row", 0)], side=mixed_s)
    x2_s, h2s = pl.pallas_call(
        _resid_normmod_kernel,
        out_shape=[jax.ShapeDtypeStruct((nb, d), F32), jax.ShapeDtypeStruct((nb, d), BF16)],
        name="s_resid_norm",
    )(xs, ro_s, mod_s[2], row(norm2_g[lyr]), mod_s[4], mod_s[3])
    h2 = _normmod(x2_p, row(norm2_g[lyr]), mod_p[4], mod_p[3])
    tn_u = _tile(d_ff, TILES["up_n"], LANES)
    assert d_ff % tn_u == 0
    ffn_rows = ([(fcw[j:j + 1], "row", 0) for j in range(fw)] + [(fcb, "row", 0)]
                + [(fcw[j:j + 1], "row", d_ff) for j in range(fw)] + [(fcb, "row", d_ff)])
    act_f, tail_g, tail_v, up_g, up_v = _matmul(
        "p_ffn_up", [h2], [(w_up0, 0), (w_up0, d_ff)], [(0, 0), (0, 1)], d_ff,
        functools.partial(_ep_ffn_prompt, width=fw),
        [(BF16, "tile"), (F32, "tail"), (F32, "tail")],
        tm=tm, tn=tn_u, order="nm", carry_rows=2, cast_once=True, row_split=split_ffn, extras=ffn_rows,
        side=h2s)
    tm_dn = _tile(l, TILES["down_m"], 16)
    (y_p,) = _matmul("p_down", [act_f], [(w_dn, 0)], [(0, 0)], d, _ep_residual, [(F32, "tile")],
                     tm=tm_dn, tn=tn_d,
                     extras=[(x2_p, "tile", 0), (mod_p[5], "row", 0)])

    fbuf = state_ffn_conv[lyr].astype(F32)
    tn_f = _tile(d_ff, 8192, LANES)
    nfb = d_ff // tn_f
    half = lambda rows, v: pl.BlockSpec((rows, tn_f), lambda j: (0, j + v * nfb))
    act_s = pl.pallas_call(
        functools.partial(_ffn_sample_kernel, width=fw),
        grid=(nfb,),
        in_specs=([half(nb, 0), half(nb, 0)]
                  + [half(nb, 0) for _ in range(fw - 1)] + [half(nb, 1) for _ in range(fw - 1)]
                  + [half(fw, 0), half(fw, 1), half(1, 0), half(1, 1)]),
        out_specs=half(nb, 0),
        out_shape=jax.ShapeDtypeStruct((nb, d_ff), BF16),
        compiler_params=_params("parallel"),
        name="s_ffn_conv",
    )(up_g, up_v, *[fbuf[:, j, :] for j in range(fw - 1)], *[fbuf[:, j, :] for j in range(fw - 1)],
      fcw, fcw, fcb, fcb)
    (y_s,) = _matmul("s_down", [act_s], [(w_dn, 0)], [(0, 0)], d, _ep_residual, [(F32, "tile")],
                     tm=nb, tn=tn_ds, tk=_tile(d_ff, TILES["down_k"], LANES),
                     extras=[(x2_s, "tile", 0), (mod_s[5], "tile", 0)])

    y_prompt = y_p[None].astype(x_prompt.dtype)
    y_sample = y_s[:, None, :].astype(x_sample.dtype)
    k_prompt = kb_p.reshape(1, 1, l, h_b, 2 * dq)
    v_prompt = vb_p.reshape(1, 1, l, h_b, dvb)
    gdn_state_prompt = s_p[None, None]
    gdn_conv_prompt = qkv_tail[SUBLANES - (gw - 1):][None, None]
    ffn_conv_prompt = jnp.concatenate([tail_g[SUBLANES - (fw - 1):], tail_v[SUBLANES - (fw - 1):]],
                                      axis=1)[None, None]
    k_sample = kn_s.reshape(1, nb, 1, h_b, 2 * dq)
    v_sample = vb_s.reshape(1, nb, 1, h_b, dvb)
    gdn_state_sample = s_s[None]
    gdn_conv_sample = gbuf_s[None]
    up_new = jnp.concatenate([up_g, up_v], axis=1)
    ffn_conv_sample = jnp.concatenate([fbuf[:, 1:, :], up_new[:, None, :]], axis=1)[None]
    return (y_prompt, y_sample, k_prompt, v_prompt, gdn_state_prompt, gdn_conv_prompt,
            ffn_conv_prompt, k_sample, v_sample, gdn_state_sample, gdn_conv_sample, ffn_conv_sample)
```

```python
import functools
import math

import jax
import jax.numpy as jnp
from jax import lax
from jax.experimental import pallas as pl
from jax.experimental.pallas import tpu as pltpu

F32 = jnp.float32
BF16 = jnp.bfloat16

LANES = 128
SUBLANES = 8
VMEM_LIMIT_BYTES = 56 * 1024 * 1024

EPS = 1e-6
SUBLN_EPS = 1e-5
GDN_CHUNK = 128
MAX_UNSHIFTED_SCORE = 40.0
NEG_BIG = -1e30

TILES = dict(
    norm_rows=512,
    mm_m=1024, mm_n=512,
    up_n=256,
    down_k=5504, down_m=512,
    attn=1024,
    gdn_heads=16,
    prep_rows=512,
    decode_pages=8,
    small_n=1024,
)

NT_DIMS = (((1,), (1,)), ((), ()))


def _params(*sem):
    return pltpu.CompilerParams(dimension_semantics=sem, vmem_limit_bytes=VMEM_LIMIT_BYTES)


def _tile(dim, pref, align):
    if dim <= pref:
        return dim
    t = (pref // align) * align
    while t >= align:
        if dim % t == 0:
            return t
        t -= align
    return dim


def _sigmoid(x):
    return 1.0 / (1.0 + jnp.exp(-x))


def _silu(x):
    return x * _sigmoid(x)


def _softplus(x):
    return jnp.maximum(x, 0.0) + jnp.log1p(jnp.exp(-jnp.abs(x)))


def _normmod_kernel(x_ref, g_ref, sc_ref, sh_ref, o_ref):
    x = x_ref[...]
    ms = jnp.mean(x * x, axis=-1, keepdims=True)
    h = x * lax.rsqrt(ms + EPS) * g_ref[...]
    o_ref[...] = (h * (1.0 + sc_ref[...]) + sh_ref[...]).astype(o_ref.dtype)


def _normmod(x, g, sc, sh):
    m, d = x.shape
    tm = _tile(m, TILES["norm_rows"], SUBLANES)
    per_row = sc.shape[0] != 1
    mod_spec = (pl.BlockSpec((tm, d), lambda i: (i, 0)) if per_row
                else pl.BlockSpec((1, d), lambda i: (0, 0)))
    return pl.pallas_call(
        _normmod_kernel,
        grid=(m // tm,),
        in_specs=[pl.BlockSpec((tm, d), lambda i: (i, 0)),
                  pl.BlockSpec((1, d), lambda i: (0, 0)), mod_spec, mod_spec],
        out_specs=pl.BlockSpec((tm, d), lambda i: (i, 0)),
        out_shape=jax.ShapeDtypeStruct((m, d), BF16),
        compiler_params=_params("parallel"),
        name="normmod",
    )(x, g, sc, sh)


def _matmul(name, as_, ws, dots, n, epilogue, outs, *, tm, tn, extras=(), order="mn",
            tk=None, a_fn=None, carry_rows=0, cast_once=False, row_split=1, wt=False, side=None):
    m = as_[0].shape[0]
    kdim = as_[0].shape[1]
    tk = kdim if tk is None else tk
    nk = kdim // tk
    assert kdim % tk == 0 and m % tm == 0 and n % tn == 0 and tm % row_split == 0
    if nk > 1:
        assert all(a.shape[1] == kdim for a in as_) and row_split == 1 and not cast_once
    assert not cast_once or order == "nm"
    ni, nj = m // tm, n // tn
    na, nw, ne, no, nd = len(as_), len(ws), len(extras), len(outs), len(dots)
    rows = tm // row_split

    if order == "mn":
        grid = (ni, nj, nk)
        ij = lambda g0, g1: (g0, g1)
    else:
        grid = (nj, ni, nk)
        ij = lambda g0, g1: (g1, g0)

    def a_map(g0, g1, k):
        return (ij(g0, g1)[0], k)

    def w_map(off):
        return lambda g0, g1, k: (k, ij(g0, g1)[1] + off // tn)

    def row_map(off):
        return lambda g0, g1, k: (0, ij(g0, g1)[1] + off // tn)

    def tile_map(off):
        return lambda g0, g1, k: (ij(g0, g1)[0], ij(g0, g1)[1] + off // tn)

    in_specs = []
    for a in as_:
        in_specs.append(pl.BlockSpec((tm, tk if nk > 1 else a.shape[1]), a_map))
    for w, off in ws:
        if wt:
            assert nk == 1 and off % SUBLANES == 0
            in_specs.append(pl.BlockSpec(
                (pl.Element(tn), pl.Element(w.shape[1])),
                functools.partial(
                    lambda g0, g1, k, off: (pl.multiple_of(off + ij(g0, g1)[1] * tn, SUBLANES), 0),
                    off=off)))
        else:
            assert off % tn == 0
            in_specs.append(pl.BlockSpec((tk if nk > 1 else w.shape[0], tn), w_map(off)))
    for arr, kind, off in extras:
        assert off % tn == 0
        if kind == "row":
            in_specs.append(pl.BlockSpec((1, tn), row_map(off)))
        else:
            in_specs.append(pl.BlockSpec((tm, tn), tile_map(off)))
    out_specs, out_shapes = [], []
    for dtype, kind in outs:
        if kind == "tile":
            out_specs.append(pl.BlockSpec((tm, tn), tile_map(0)))
            out_shapes.append(jax.ShapeDtypeStruct((m, n), dtype))
        else:
            out_specs.append(pl.BlockSpec((SUBLANES, tn), row_map(0)))
            out_shapes.append(jax.ShapeDtypeStruct((SUBLANES, n), dtype))
    nside = 0
    sides = []
    if side is not None:
        sides = list(side) if isinstance(side, (list, tuple)) else [side] * nw
        assert cast_once and len(sides) == nw
        nside = nw
        for sd in sides:
            in_specs.append(pl.BlockSpec(sd.shape, lambda g0, g1, k: (0, 0)))
        for sd in sides:
            out_specs.append(pl.BlockSpec((sd.shape[0], tn), row_map(0)))
            out_shapes.append(jax.ShapeDtypeStruct((sd.shape[0], n), F32))
    scratch = []
    if nk > 1:
        scratch += [pltpu.VMEM((tm, tn), F32) for _ in range(nd)]
    if cast_once:
        scratch += [pltpu.VMEM((tn, w.shape[1]) if wt else (w.shape[0], tn), BF16) for w, _ in ws]
    scratch += [pltpu.VMEM((SUBLANES + rows, tn), F32) for _ in range(carry_rows)]

    def body(*refs):
        a_refs = refs[:na]
        w_refs = refs[na:na + nw]
        e_refs = refs[na + nw:na + nw + ne]
        nin = na + nw + ne + nside
        side_refs = refs[nin - nside:nin]
        o_refs = refs[nin:nin + no]
        side_out = refs[nin + no:nin + no + nside]
        s_refs = list(refs[nin + no + nside:])
        acc_refs = [s_refs.pop(0) for _ in range(nd)] if nk > 1 else []
        wb_refs = [s_refs.pop(0) for _ in range(nw)] if cast_once else list(w_refs)
        carry_refs = s_refs
        i, _ = ij(pl.program_id(0), pl.program_id(1))
        k = pl.program_id(2)

        def partial(d, r):
            ai, wi = dots[d]
            a = a_refs[ai][r * rows:(r + 1) * rows, :] if row_split > 1 else a_refs[ai][...]
            if a_fn is not None:
                a = a_fn(a)
            w = wb_refs[wi][...].astype(BF16)
            if wt:
                return lax.dot_general(a.astype(BF16), w, NT_DIMS, preferred_element_type=F32)
            return jnp.dot(a.astype(BF16), w, preferred_element_type=F32)

        if cast_once:
            @pl.when(i == 0)
            def _():
                for wi in range(nw):
                    wb_refs[wi][...] = w_refs[wi][...].astype(BF16)
                for wi in range(nside):
                    wb = wb_refs[wi][...]
                    sd = side_refs[wi][...]
                    side_out[wi][...] = (
                        lax.dot_general(sd, wb, NT_DIMS, preferred_element_type=F32)
                        if wt else jnp.dot(sd, wb, preferred_element_type=F32))

        if nk == 1:
            for r in range(row_split):
                if row_split > 1:
                    piece = pl.ds(r * rows, rows)
                    ev = [e.at[piece] if extras[x][1] == "tile" else e for x, e in enumerate(e_refs)]
                    ov = [o.at[piece] if outs[x][1] == "tile" else o for x, o in enumerate(o_refs)]
                else:
                    ev, ov = e_refs, o_refs
                epilogue([partial(d, r) for d in range(nd)], ev, ov, carry_refs,
                         (i == 0) if r == 0 else False, r == row_split - 1)
        else:
            @pl.when(k == 0)
            def _():
                for d in range(nd):
                    acc_refs[d][...] = jnp.zeros((tm, tn), F32)

            for d in range(nd):
                acc_refs[d][...] += partial(d, 0)

            @pl.when(k == nk - 1)
            def _():
                epilogue([acc_refs[d][...] for d in range(nd)], e_refs, o_refs,
                         carry_refs, i == 0, True)

    return pl.pallas_call(
        body,
        grid=grid,
        in_specs=in_specs,
        out_specs=out_specs,
        out_shape=out_shapes,
        scratch_shapes=scratch,
        compiler_params=_params("arbitrary", "arbitrary", "arbitrary"),
        name=name,
    )(*as_, *[w for w, _ in ws], *[e for e, _, _ in extras], *sides)


def _conv_taps(raw, win_ref, taps, first):
    width = len(taps)
    rows = raw.shape[0]
    if first is not False:
        @pl.when(first)
        def _():
            win_ref[0:SUBLANES, :] = jnp.zeros((SUBLANES, raw.shape[1]), F32)

    win_ref[SUBLANES:SUBLANES + rows, :] = raw
    y = win_ref[SUBLANES:SUBLANES + rows, :] * taps[width - 1]
    for s in range(1, width):
        y = y + win_ref[SUBLANES - s:SUBLANES - s + rows, :] * taps[width - 1 - s]
    win_ref[0:SUBLANES, :] = win_ref[rows:rows + SUBLANES, :]
    return y


def _gdn_gates(ba, alog_row, dtb_row):
    return _sigmoid(ba), -jnp.exp(alog_row) * _softplus(ba + dtb_row)


def _gdn_prep_kernel(h_ref, hs_ref, wba_ref, alog_ref, dtb_ref, beta_ref, gc_ref, gct_ref, bas_ref,
                     *, heads_pad, h_a):
    c = GDN_CHUNK
    wba = wba_ref[...].astype(BF16)

    @pl.when(pl.program_id(0) == 0)
    def _():
        bas_ref[...] = lax.dot_general(hs_ref[...], wba, NT_DIMS, preferred_element_type=F32)

    ba = lax.dot_general(h_ref[...], wba, NT_DIMS, preferred_element_type=F32)
    beta, g = _gdn_gates(ba, alog_ref[...], dtb_ref[...])
    beta_ref[...] = beta
    row = lax.broadcasted_iota(jnp.int32, (c, LANES), 0)
    for q in range(g.shape[0] // c):
        gq = g[q * c:(q + 1) * c, :]
        s = 1
        while s < c:
            gq = gq + jnp.where(row >= s, pltpu.roll(gq, s, axis=0), 0.0)
            s *= 2
        gc_ref[q * c:(q + 1) * c, :] = gq
        gct_ref[:, q * c:(q + 1) * c] = gq.T[h_a:h_a + heads_pad, :]


def _gdn_intra_kernel(q_ref, k_ref, v_ref, beta_ref, gc_ref, gct_ref,
                      uv_ref, w_ref, aqk_ref, qdec_ref, kdect_ref, *, hb, h_a):
    c = GDN_CHUNK
    grp = pl.program_id(1)
    lane = lax.broadcasted_iota(jnp.int32, (c, LANES), 1)
    row = lax.broadcasted_iota(jnp.int32, (c, c), 0)
    col = lax.broadcasted_iota(jnp.int32, (c, c), 1)
    eye = jnp.where(row == col, 1.0, 0.0)
    levels = range(1, int(math.log2(c)))
    below = [((row >> (lvl + 1)) == (col >> (lvl + 1))) & ((row >> lvl) != (col >> lvl))
             for lvl in levels]
    pair = (row >> 1) == (col >> 1)
    beta_all = beta_ref[...]
    gc_all = gc_ref[...]
    heads = range(hb)
    sls = [slice(j * LANES, (j + 1) * LANES) for j in heads]
    beta_c, gcc, kb, mm, t = [], [], [], [], []
    for j in heads:
        h = grp * hb + j
        beta_c.append(jnp.sum(jnp.where(lane == h, beta_all, 0.0), axis=1, keepdims=True))
        gcc.append(jnp.sum(jnp.where(lane == h + h_a, gc_all, 0.0), axis=1, keepdims=True))
        gcr = gct_ref[pl.ds(h, 1), :]
        dec_incl = jnp.exp(jnp.where(row >= col, gcc[j] - gcr, NEG_BIG))
        k = k_ref[:, sls[j]]
        kb.append(k.astype(F32) * beta_c[j])
        mm.append(lax.dot_general(kb[j].astype(BF16), k, NT_DIMS, preferred_element_type=F32)
                  * jnp.where(row > col, dec_incl, 0.0))
        aqk = lax.dot_general(q_ref[:, sls[j]], k, NT_DIMS, preferred_element_type=F32) * dec_incl
        aqk_ref[:, sls[j]] = aqk.astype(BF16)
        t.append(eye - jnp.where(pair, mm[j], 0.0))
    for msk in below:
        tb = [t[j].astype(BF16) for j in heads]
        bt = [jnp.dot(jnp.where(msk, mm[j], 0.0).astype(BF16), tb[j], preferred_element_type=F32)
              for j in heads]
        t = [t[j] - jnp.dot(tb[j], bt[j].astype(BF16), preferred_element_type=F32) for j in heads]
    for j in heads:
        tb = t[j].astype(BF16)
        egc = jnp.exp(gcc[j])
        gl = gcc[j][c - 1:c, :]
        vb = v_ref[:, sls[j]].astype(F32) * beta_c[j]
        uv_ref[:, sls[j]] = jnp.dot(tb, vb.astype(BF16), preferred_element_type=F32)
        w_ref[:, sls[j]] = jnp.dot(tb, (kb[j] * egc).astype(BF16),
                                   preferred_element_type=F32).astype(BF16)
        qdec_ref[:, sls[j]] = (q_ref[:, sls[j]].astype(F32) * egc).astype(BF16)
        kdect_ref[sls[j], :] = (k_ref[:, sls[j]].astype(F32) * jnp.exp(gl - gcc[j])).T.astype(BF16)


def _gdn_rec_kernel(uv_ref, w_ref, aqk_ref, qdec_ref, kdect_ref, gct_ref, za_ref, gn_ref,
                    o_ref, sout_ref, s_ref, *, hb):
    c = GDN_CHUNK
    n = pl.program_id(1)
    g0 = pl.program_id(0)

    @pl.when(n == 0)
    def _():
        s_ref[...] = jnp.zeros(s_ref.shape, F32)

    heads = range(hb)
    sls = [slice(j * LANES, (j + 1) * LANES) for j in heads]
    s = [s_ref[j] for j in heads]
    sb = [s[j].astype(BF16) for j in heads]
    ub = [(uv_ref[:, sls[j]] - jnp.dot(w_ref[:, sls[j]], sb[j], preferred_element_type=F32)
           ).astype(BF16) for j in heads]
    oq = [jnp.dot(qdec_ref[:, sls[j]], sb[j], preferred_element_type=F32) for j in heads]
    for j in heads:
        gl = jnp.exp(gct_ref[pl.ds(g0 * hb + j, 1), c - 1:c])
        s_ref[j] = gl * s[j] + jnp.dot(kdect_ref[sls[j], :], ub[j], preferred_element_type=F32)
    for j in heads:
        o = oq[j] + jnp.dot(aqk_ref[:, sls[j]], ub[j], preferred_element_type=F32)
        ms = jnp.mean(o * o, axis=-1, keepdims=True)
        o_ref[:, sls[j]] = (o * lax.rsqrt(ms + EPS) * gn_ref[...]
                            * za_ref[:, sls[j]].astype(F32)).astype(o_ref.dtype)

    @pl.when(n == pl.num_programs(1) - 1)
    def _():
        sout_ref[...] = s_ref[...]


def _gate_rows(a_log, dt_bias, h_a):
    alog_row = jnp.zeros((1, LANES), F32).at[0, h_a:2 * h_a].set(a_log)
    dtb_row = jnp.zeros((1, LANES), F32).at[0, h_a:2 * h_a].set(dt_bias)
    return alog_row, dtb_row


def _gdn_prompt(qkv, h1, h1s, w_t, c_ba, za_silu, a_log, dt_bias, gnorm, h_a, dk):
    l = qkv.shape[0]
    kdim = h1.shape[1]
    c = GDN_CHUNK
    assert l % c == 0 and dk == LANES and 2 * h_a <= LANES and c_ba % LANES == 0
    nchunk = l // c
    hp = max(SUBLANES, -(-h_a // SUBLANES) * SUBLANES)
    alog_row, dtb_row = _gate_rows(a_log, dt_bias, h_a)

    ms = h1s.shape[0]
    rp = _tile(l, TILES["prep_rows"], c)
    beta, gc, gct, ba_s = pl.pallas_call(
        functools.partial(_gdn_prep_kernel, heads_pad=hp, h_a=h_a),
        grid=(l // rp,),
        in_specs=[pl.BlockSpec((rp, kdim), lambda n: (n, 0)),
                  pl.BlockSpec((ms, kdim), lambda n: (0, 0)),
                  pl.BlockSpec((LANES, kdim), lambda n: (c_ba // LANES, 0)),
                  pl.BlockSpec((1, LANES), lambda n: (0, 0)),
                  pl.BlockSpec((1, LANES), lambda n: (0, 0))],
        out_specs=[pl.BlockSpec((rp, LANES), lambda n: (n, 0)),
                   pl.BlockSpec((rp, LANES), lambda n: (n, 0)),
                   pl.BlockSpec((hp, rp), lambda n: (0, n)),
                   pl.BlockSpec((ms, LANES), lambda n: (0, 0))],
        out_shape=[jax.ShapeDtypeStruct((l, LANES), F32),
                   jax.ShapeDtypeStruct((l, LANES), F32),
                   jax.ShapeDtypeStruct((hp, l), F32),
                   jax.ShapeDtypeStruct((ms, LANES), F32)],
        compiler_params=_params("arbitrary"),
        name="gdn_prep",
    )(h1, h1s, w_t, alog_row, dtb_row)

    hd = h_a * dk
    hb = _tile(h_a, TILES["gdn_heads"], 1)
    ng = h_a // hb
    blk = lambda off: pl.BlockSpec((c, hb * LANES), lambda n, g: (n, g + off))
    uv, w, aqk, qdec, kdect = pl.pallas_call(
        functools.partial(_gdn_intra_kernel, hb=hb, h_a=h_a),
        grid=(nchunk, ng),
        in_specs=[blk(0), blk(ng), blk(2 * ng),
                  pl.BlockSpec((c, LANES), lambda n, g: (n, 0)),
                  pl.BlockSpec((c, LANES), lambda n, g: (n, 0)),
                  pl.BlockSpec((hp, c), lambda n, g: (0, n))],
        out_specs=[blk(0), blk(0), blk(0), blk(0),
                   pl.BlockSpec((hb * LANES, c), lambda n, g: (g, n))],
        out_shape=[jax.ShapeDtypeStruct((l, hd), F32),
                   jax.ShapeDtypeStruct((l, hd), BF16),
                   jax.ShapeDtypeStruct((l, hd), BF16),
                   jax.ShapeDtypeStruct((l, hd), BF16),
                   jax.ShapeDtypeStruct((hd, l), BF16)],
        compiler_params=_params("parallel", "arbitrary"),
        name="gdn_intra",
    )(qkv, qkv, qkv, beta, gc, gct)

    hr = hb
    wide = lambda: pl.BlockSpec((c, hr * LANES), lambda g, n: (n, g))
    oa, s_fin = pl.pallas_call(
        functools.partial(_gdn_rec_kernel, hb=hr),
        grid=(h_a // hr, nchunk),
        in_specs=[wide(), wide(), wide(), wide(),
                  pl.BlockSpec((hr * LANES, c), lambda g, n: (g, n)),
                  pl.BlockSpec((hp, c), lambda g, n: (0, n)),
                  wide(),
                  pl.BlockSpec((1, LANES), lambda g, n: (0, 0))],
        out_specs=[wide(), pl.BlockSpec((hr, dk, LANES), lambda g, n: (g, 0, 0))],
        out_shape=[jax.ShapeDtypeStruct((l, hd), BF16),
                   jax.ShapeDtypeStruct((h_a, dk, LANES), F32)],
        scratch_shapes=[pltpu.VMEM((hr, dk, LANES), F32)],
        compiler_params=_params("parallel", "arbitrary"),
        name="gdn_rec",
    )(uv, w, aqk, qdec, kdect, gct, za_silu, gnorm)
    return oa, s_fin, ba_s


def _gdn_sample_kernel(zq_ref, za_ref, ba_ref, buf_ref, s_ref, cw_ref, alog_ref, dtb_ref, gn_ref,
                       o_ref, snew_ref, bufnew_ref, *, h_a, dk, width):
    hd = h_a * dk
    raw = zq_ref[0]
    buf = buf_ref[0]
    y = raw * cw_ref[width - 1:width, :]
    for j in range(width - 1):
        y = y + buf[j:j + 1, :] * cw_ref[j:j + 1, :]
    y = _silu(y)
    bufnew_ref[0, 0:width - 2, :] = buf[1:width - 1, :]
    bufnew_ref[0, width - 2:width - 1, :] = raw
    za = za_ref[0]
    beta_row, g_row = _gdn_gates(ba_ref[0], alog_ref[...], dtb_ref[...])
    lane1 = lax.broadcasted_iota(jnp.int32, (1, LANES), 1)
    row = lax.broadcasted_iota(jnp.int32, (dk, LANES), 0)
    col = lax.broadcasted_iota(jnp.int32, (dk, LANES), 1)
    eye = row == col
    for h in range(h_a):
        q = y[:, h * dk:(h + 1) * dk]
        k = y[:, hd + h * dk:hd + (h + 1) * dk]
        v = y[:, 2 * hd + h * dk:2 * hd + (h + 1) * dk]
        q = q * lax.rsqrt(jnp.sum(q * q, axis=-1, keepdims=True) + EPS) * (dk ** -0.5)
        k = k * lax.rsqrt(jnp.sum(k * k, axis=-1, keepdims=True) + EPS)
        beta = jnp.sum(jnp.where(lane1 == h, beta_row, 0.0), axis=1, keepdims=True)
        a = jnp.exp(jnp.sum(jnp.where(lane1 == h + h_a, g_row, 0.0), axis=1, keepdims=True))
        k_col = jnp.sum(jnp.where(eye, k, 0.0), axis=1, keepdims=True)
        q_col = jnp.sum(jnp.where(eye, q, 0.0), axis=1, keepdims=True)
        s = a * s_ref[0, h]
        u = beta * (v - jnp.sum(s * k_col, axis=0, keepdims=True))
        s_new = s + k_col * u
        snew_ref[0, h] = s_new
        o = jnp.sum(s_new * q_col, axis=0, keepdims=True)
        ms = jnp.mean(o * o, axis=-1, keepdims=True)
        o_ref[0, :, h * dk:(h + 1) * dk] = (
            o * lax.rsqrt(ms + EPS) * gn_ref[...] * _silu(za[:, h * dk:(h + 1) * dk])
        ).astype(o_ref.dtype)


def _gdn_sample(zq, za, ba, buf, s0, conv_w, a_log, dt_bias, gnorm, h_a, dk):
    b = zq.shape[0]
    width = conv_w.shape[0]
    hd = h_a * dk
    alog_row, dtb_row = _gate_rows(a_log, dt_bias, h_a)
    per_seq = lambda n: pl.BlockSpec((1, 1, n), lambda i: (i, 0, 0))
    return pl.pallas_call(
        functools.partial(_gdn_sample_kernel, h_a=h_a, dk=dk, width=width),
        grid=(b,),
        in_specs=[per_seq(3 * hd), per_seq(hd), per_seq(LANES),
                  pl.BlockSpec((1, width - 1, 3 * hd), lambda i: (i, 0, 0)),
                  pl.BlockSpec((1, h_a, dk, LANES), lambda i: (i, 0, 0, 0)),
                  pl.BlockSpec((width, 3 * hd), lambda i: (0, 0)),
                  pl.BlockSpec((1, LANES), lambda i: (0, 0)),
                  pl.BlockSpec((1, LANES), lambda i: (0, 0)),
                  pl.BlockSpec((1, LANES), lambda i: (0, 0))],
        out_specs=[pl.BlockSpec((1, 1, hd), lambda i: (i, 0, 0)),
                   pl.BlockSpec((1, h_a, dk, LANES), lambda i: (i, 0, 0, 0)),
                   pl.BlockSpec((1, width - 1, 3 * hd), lambda i: (i, 0, 0))],
        out_shape=[jax.ShapeDtypeStruct((b, 1, hd), BF16),
                   jax.ShapeDtypeStruct((b, h_a, dk, LANES), F32),
                   jax.ShapeDtypeStruct((b, width - 1, 3 * hd), F32)],
        compiler_params=_params("parallel"),
        name="gdn_sample",
    )(zq[:, None, :], za[:, None, :], ba[:, None, :], buf, s0, conv_w, alog_row, dtb_row, gnorm)


def _diff_lambda(lamp_ref, lam_init):
    lp = lamp_ref[...]
    e1 = jnp.exp(jnp.sum(lp[0:1] * lp[1:2], axis=1, keepdims=True))
    e2 = jnp.exp(jnp.sum(lp[2:3] * lp[3:4], axis=1, keepdims=True))
    return e1 - e2 + lam_init


def _attn_prompt_kernel(qi_ref, kj_ref, q_ref, k_ref, v_ref, lamp_ref, g_ref, o_ref,
                        m_ref, l_ref, acc_ref, *, dq, lam_init):
    p = pl.program_id(1)
    qi = qi_ref[p]
    kj = kj_ref[p]

    @pl.when(kj == 0)
    def _():
        m_ref[...] = jnp.full(m_ref.shape, NEG_BIG, F32)
        l_ref[...] = jnp.zeros(l_ref.shape, F32)
        acc_ref[...] = jnp.zeros(acc_ref.shape, F32)

    def step(masked):
        v = v_ref[...]
        for c in range(2):
            s = lax.dot_general(q_ref[:, c * dq:(c + 1) * dq], k_ref[:, c * dq:(c + 1) * dq],
                                NT_DIMS, preferred_element_type=F32)
            if masked:
                row = lax.broadcasted_iota(jnp.int32, s.shape, 0)
                col = lax.broadcasted_iota(jnp.int32, s.shape, 1)
                s = jnp.where(row >= col, s, NEG_BIG)
            m_prev = m_ref[c]
            m_new = jnp.maximum(m_prev, jnp.max(s, axis=1, keepdims=True))
            alpha = jnp.exp2(m_prev - m_new)
            pm = jnp.exp2(s - m_new)
            l_ref[c] = alpha * l_ref[c] + jnp.sum(pm, axis=1, keepdims=True)
            acc_ref[c] = alpha * acc_ref[c] + jnp.dot(pm.astype(BF16), v,
                                                      preferred_element_type=F32)
            m_ref[c] = m_new

    @pl.when(kj < qi)
    def _():
        step(False)

    @pl.when(kj == qi)
    def _():
        step(True)
        _attn_finish(acc_ref, l_ref[0], l_ref[1], lamp_ref, g_ref, o_ref, lam_init)


def _attn_finish(acc_ref, l0, l1, lamp_ref, g_ref, o_ref, lam_init):
    lam = _diff_lambda(lamp_ref, lam_init)
    o = acc_ref[0] / l0 - lam * (acc_ref[1] / l1)
    ms = jnp.mean(o * o, axis=-1, keepdims=True)
    o_ref[...] = (o * lax.rsqrt(ms + SUBLN_EPS) * g_ref[...] * (1.0 - lam_init)).astype(o_ref.dtype)


def _attn_prompt_bounded_kernel(qi_ref, kj_ref, q_ref, k_ref, v_ref, lamp_ref, g_ref, o_ref,
                                l_ref, acc_ref, *, dq, lam_init):
    p = pl.program_id(1)
    qi = qi_ref[p]
    kj = kj_ref[p]

    @pl.when(kj == 0)
    def _():
        l_ref[...] = jnp.zeros(l_ref.shape, F32)
        acc_ref[...] = jnp.zeros(acc_ref.shape, F32)

    t = q_ref.shape[0]

    def accumulate(c, r0, nr, k0, nk, masked):
        s = lax.dot_general(q_ref[r0:r0 + nr, c * dq:(c + 1) * dq],
                            k_ref[k0:k0 + nk, c * dq:(c + 1) * dq], NT_DIMS,
                            preferred_element_type=F32)
        pm = jnp.exp2(s)
        if masked:
            row = lax.broadcasted_iota(jnp.int32, s.shape, 0) + r0
            col = lax.broadcasted_iota(jnp.int32, s.shape, 1) + k0
            pm = jnp.where(row >= col, pm, 0.0)
        part = pm[:, 0:LANES]
        for g in range(1, nk // LANES):
            part = part + pm[:, g * LANES:(g + 1) * LANES]
        l_ref[c, r0:r0 + nr, :] += part
        acc_ref[c, r0:r0 + nr, :] += jnp.dot(pm.astype(BF16), v_ref[k0:k0 + nk, :],
                                             preferred_element_type=F32)

    @pl.when(kj < qi)
    def _():
        for c in range(2):
            accumulate(c, 0, t, 0, t, False)

    @pl.when(kj == qi)
    def _():
        half = t // 2 if t % (2 * LANES) == 0 else t
        for c in range(2):
            accumulate(c, 0, half, 0, half, True)
            if half < t:
                accumulate(c, half, t - half, 0, t, True)
        _attn_finish(acc_ref, jnp.sum(l_ref[0], axis=1, keepdims=True),
                     jnp.sum(l_ref[1], axis=1, keepdims=True), lamp_ref, g_ref, o_ref, lam_init)


def _attn_prompt(q, k, v, score_bound, lam_params, subln_g, h_b, dq, lam_init):
    l = q.shape[0]
    dv = 2 * dq
    t = _tile(l, TILES["attn"], LANES)
    nb = l // t
    pairs = [(i, j) for i in range(nb) for j in range(i + 1)]
    qi_tab = jnp.asarray([a for a, _ in pairs], jnp.int32)
    kj_tab = jnp.asarray([b for _, b in pairs], jnp.int32)

    def call(body, scratch, name):
        grid_spec = pltpu.PrefetchScalarGridSpec(
            num_scalar_prefetch=2,
            grid=(h_b, len(pairs)),
            in_specs=[pl.BlockSpec((t, dv), lambda h, p, qi, kj: (qi[p], h)),
                      pl.BlockSpec((t, dv), lambda h, p, qi, kj: (kj[p], h)),
                      pl.BlockSpec((t, dv), lambda h, p, qi, kj: (kj[p], h)),
                      pl.BlockSpec((4, dq), lambda h, p, qi, kj: (0, 0)),
                      pl.BlockSpec((1, dv), lambda h, p, qi, kj: (0, 0))],
            out_specs=pl.BlockSpec((t, dv), lambda h, p, qi, kj: (qi[p], h)),
            scratch_shapes=scratch,
        )
        return pl.pallas_call(
            functools.partial(body, dq=dq, lam_init=lam_init),
            grid_spec=grid_spec,
            out_shape=jax.ShapeDtypeStruct((l, h_b * dv), BF16),
            compiler_params=_params("parallel", "arbitrary"),
            name=name,
        )(qi_tab, kj_tab, q, k, v, lam_params, subln_g)

    bounded = lambda: call(_attn_prompt_bounded_kernel,
                           [pltpu.VMEM((2, t, LANES), F32), pltpu.VMEM((2, t, dv), F32)],
                           "attn_prompt_bounded")
    online = lambda: call(_attn_prompt_kernel,
                          [pltpu.VMEM((2, t, 1), F32), pltpu.VMEM((2, t, 1), F32),
                           pltpu.VMEM((2, t, dv), F32)], "attn_prompt_online")
    return lax.cond(score_bound <= MAX_UNSHIFTED_SCORE, bounded, online)


def _attn_decode_kernel(pt_ref, q_ref, kn_ref, vn_ref, *rest, h_b, dq, lam_init, pp):
    del pt_ref
    kc_refs = rest[:pp]
    vc_refs = rest[pp:2 * pp]
    lamp_ref, g_ref, o_ref, m_ref, l_ref, acc_ref = rest[2 * pp:]
    p = pl.program_id(1)
    dv = 2 * dq
    page = kc_refs[0].shape[2]
    assert h_b & (h_b - 1) == 0
    qm = q_ref[0]

    @pl.when(p == 0)
    def _():
        m_ref[...] = jnp.sum(qm * kn_ref[0], axis=1, keepdims=True)
        l_ref[...] = jnp.ones(l_ref.shape, F32)
        acc_ref[...] = vn_ref[0]

    qb = qm.astype(BF16)
    scores = []
    for j in range(pp):
        k2 = kc_refs[j][0, 0].reshape(page * h_b, dv).astype(BF16)
        s = lax.dot_general(qb, k2, NT_DIMS, preferred_element_type=F32)
        sub = lax.broadcasted_iota(jnp.int32, s.shape, 0)
        lane = lax.broadcasted_iota(jnp.int32, s.shape, 1)
        scores.append(jnp.where((lane & (h_b - 1)) == (sub >> 1), s, NEG_BIG))
    m_prev = m_ref[...]
    m_new = m_prev
    for s in scores:
        m_new = jnp.maximum(m_new, jnp.max(s, axis=1, keepdims=True))
    alpha = jnp.exp(m_prev - m_new)
    l_new = alpha * l_ref[...]
    acc = alpha * acc_ref[...]
    for j in range(pp):
        pm = jnp.exp(scores[j] - m_new)
        l_new = l_new + jnp.sum(pm, axis=1, keepdims=True)
        v2 = vc_refs[j][0, 0].reshape(page * h_b, dv).astype(BF16)
        acc = acc + jnp.dot(pm.astype(BF16), v2, preferred_element_type=F32)
    l_ref[...] = l_new
    acc_ref[...] = acc
    m_ref[...] = m_new

    @pl.when(p == pl.num_programs(1) - 1)
    def _():
        lam = _diff_lambda(lamp_ref, lam_init)
        for h in range(h_b):
            o1 = acc_ref[2 * h:2 * h + 1, :] / l_ref[2 * h:2 * h + 1, :]
            o2 = acc_ref[2 * h + 1:2 * h + 2, :] / l_ref[2 * h + 1:2 * h + 2, :]
            o = o1 - lam * o2
            ms = jnp.mean(o * o, axis=-1, keepdims=True)
            o_ref[0, :, h * dv:(h + 1) * dv] = (o * lax.rsqrt(ms + SUBLN_EPS) * g_ref[...]
                                                * (1.0 - lam_init)).astype(o_ref.dtype)


def _attn_decode(qn, kn, vn, cache_k, cache_v, page_table, lam_params, subln_g, h_b, dq,
                 lam_init, lyr):
    b = qn.shape[0]
    n_pages = page_table.shape[1]
    page = cache_k.shape[2]
    dv = 2 * dq
    nsub = 2 * h_b
    pp = _tile(n_pages, TILES["decode_pages"], 1)
    zeros = jnp.zeros((b, h_b, dq), F32)

    def sub_rows(x):
        x4 = x.reshape(b, h_b, 2, dq)
        return jnp.stack([jnp.concatenate([x4[:, :, 0], zeros], axis=-1),
                          jnp.concatenate([zeros, x4[:, :, 1]], axis=-1)], axis=2).reshape(b, nsub, dv)

    qm = sub_rows(qn)
    km = sub_rows(kn)
    vm = jnp.repeat(vn.reshape(b, h_b, 1, dv), 2, axis=2).reshape(b, nsub, dv)
    rows = lambda: pl.BlockSpec((1, nsub, dv), lambda i, p, pt: (i, 0, 0))

    def pages(j):
        return pl.BlockSpec((1, 1, page, h_b, dv), lambda i, p, pt: (lyr, pt[i, p * pp + j], 0, 0, 0))

    grid_spec = pltpu.PrefetchScalarGridSpec(
        num_scalar_prefetch=1,
        grid=(b, n_pages // pp),
        in_specs=([rows(), rows(), rows()] + [pages(j) for j in range(pp)]
                  + [pages(j) for j in range(pp)]
                  + [pl.BlockSpec((4, dq), lambda i, p, pt: (0, 0)),
                     pl.BlockSpec((1, dv), lambda i, p, pt: (0, 0))]),
        out_specs=pl.BlockSpec((1, 1, h_b * dv), lambda i, p, pt: (i, 0, 0)),
        scratch_shapes=[pltpu.VMEM((nsub, 1), F32), pltpu.VMEM((nsub, 1), F32),
                        pltpu.VMEM((nsub, dv), F32)],
    )
    return pl.pallas_call(
        functools.partial(_attn_decode_kernel, h_b=h_b, dq=dq, lam_init=lam_init, pp=pp),
        grid_spec=grid_spec,
        out_shape=jax.ShapeDtypeStruct((b, 1, h_b * dv), BF16),
        compiler_params=_params("parallel", "arbitrary"),
        name="attn_decode",
    )(page_table, qm, km, vm, *([cache_k] * pp), *([cache_v] * pp), lam_params, subln_g)


def _ep_plain(accs, e, o, carry, first, last):
    o[0][...] = accs[0].astype(o[0].dtype)


def _ep_bias(accs, e, o, carry, first, last):
    o[0][...] = accs[0] + e[0][...]


def _ep_silu(accs, e, o, carry, first, last):
    o[0][...] = _silu(accs[0]).astype(o[0].dtype)


def _ep_sigmoid(accs, e, o, carry, first, last):
    o[0][...] = _sigmoid(accs[0]).astype(o[0].dtype)


def _group_norm_store(y, gain, eps, out_refs):
    tn = y.shape[1]
    for g in range(tn // LANES):
        sl = slice(g * LANES, (g + 1) * LANES)
        blk = y[:, sl]
        ms = jnp.sum(blk * blk, axis=-1, keepdims=True) / LANES
        val = blk * lax.rsqrt(ms + eps) * gain[:, sl]
        for r in out_refs:
            r[:, sl] = val.astype(r.dtype)


def _ep_qknorm(accs, e, o, carry, first, last):
    _group_norm_store(accs[0], e[0][...], EPS, o)


def _ep_copy2(accs, e, o, carry, first, last):
    for r in o:
        r[...] = accs[0].astype(r.dtype)


def _sample_post_kernel(q_ref, k_ref, ga_ref, gb_ref, gq_ref, gk_ref, qo_ref, ko_ref, sa_ref, sb_ref):
    _group_norm_store(q_ref[...], gq_ref[...], EPS, [qo_ref])
    _group_norm_store(k_ref[...], gk_ref[...], EPS, [ko_ref])
    sa_ref[...] = _sigmoid(ga_ref[...])
    sb_ref[...] = _sigmoid(gb_ref[...])


def _ep_gdn_conv(accs, e, o, carry, first, last, *, width):
    raw = accs[0]
    taps = [e[j][...] for j in range(width)]
    nflag = e[width][...]
    gain = e[width + 1][...]
    val = _silu(_conv_taps(raw, carry[0], taps, first))
    for g in range(val.shape[1] // LANES):
        sl = slice(g * LANES, (g + 1) * LANES)
        blk = val[:, sl]
        ss = jnp.sum(blk * blk, axis=-1, keepdims=True)
        scale = jnp.where(nflag[:, sl] > 0.5, lax.rsqrt(ss + EPS), 1.0) * gain[:, sl]
        o[0][:, sl] = (blk * scale).astype(o[0].dtype)
    if last:
        o[1][...] = carry[0][0:SUBLANES, :]


def _ep_branch(accs, e, o, carry, first, last):
    o[0][...] = (e[0][...].astype(F32) * accs[0] + e[1][...].astype(F32) * accs[1]).astype(o[0].dtype)


def _ep_residual(accs, e, o, carry, first, last):
    o[0][...] = e[0][...] + e[1][...] * accs[0]


def _ep_ffn_prompt(accs, e, o, carry, first, last, *, width):
    outs = []
    for d in range(2):
        taps = [e[d * (width + 1) + j][...] for j in range(width)]
        bias = e[d * (width + 1) + width][...]
        outs.append(_conv_taps(accs[d], carry[d], taps, first) + bias)
        if last:
            o[1 + d][...] = carry[d][0:SUBLANES, :]
    o[0][...] = (_silu(outs[0]) * outs[1]).astype(o[0].dtype)


def _sample_mix_kernel(ra_ref, rb_ref, ga_ref, gb_ref, o_ref):
    o_ref[...] = (ga_ref[...] * ra_ref[...] + gb_ref[...] * rb_ref[...]).astype(o_ref.dtype)


def _resid_normmod_kernel(x_ref, r_ref, gate_ref, g_ref, sc_ref, sh_ref, x2_ref, h_ref):
    x2 = x_ref[...] + gate_ref[...] * r_ref[...]
    x2_ref[...] = x2
    ms = jnp.mean(x2 * x2, axis=-1, keepdims=True)
    h = x2 * lax.rsqrt(ms + EPS) * g_ref[...]
    h_ref[...] = (h * (1.0 + sc_ref[...]) + sh_ref[...]).astype(h_ref.dtype)


def _ffn_sample_kernel(*refs, width):
    ns = width - 1
    ups = refs[0:2]
    states = (refs[2:2 + ns], refs[2 + ns:2 + 2 * ns])
    taps = refs[2 + 2 * ns:4 + 2 * ns]
    bias = refs[4 + 2 * ns:6 + 2 * ns]
    o_ref = refs[6 + 2 * ns]
    vals = []
    for d in range(2):
        y = ups[d][...] * taps[d][width - 1:width, :]
        for j in range(ns):
            y = y + states[d][j][...] * taps[d][j:j + 1, :]
        vals.append(y + bias[d][...])
    o_ref[...] = (_silu(vals[0]) * vals[1]).astype(o_ref.dtype)


def kernel(x_prompt, x_sample, c_prompt, c_sample, cache_k, cache_v, state_gdn, state_gdn_conv, state_ffn_conv, page_table, w_ada, b_ada, norm1_g, norm2_g, w_in, gdn_conv_w, gdn_a_log, gdn_dt_bias, gdn_norm_g, diff_q_norm_g, diff_k_norm_g, diff_lambda, diff_subln_g, w_branch_a, w_branch_b, w_o, w_up, ffn_conv_w, ffn_conv_b, w_down):
    depth = w_in.shape[0]
    assert depth == 1 and x_prompt.shape[0] == 1 and x_sample.shape[1] == 1
    lyr = 0
    lam_init = 0.8 - 0.6 * math.exp(-0.3 * lyr)
    d = x_prompt.shape[-1]
    l = x_prompt.shape[1]
    nb = x_sample.shape[0]
    h_a, dk, dva = state_gdn.shape[2:]
    h_b = cache_k.shape[3]
    dq = cache_k.shape[4] // 2
    dvb = cache_v.shape[4]
    assert dk == LANES and dva == LANES and dq == LANES and dvb == 2 * dq
    d_ff = w_down.shape[1]
    gw = gdn_conv_w.shape[1]
    fw = ffn_conv_w.shape[1]
    hd_a = h_a * dk
    hd_b = h_b * dvb

    w_t = jnp.swapaxes(w_in[lyr], 0, 1)
    c_za = 3 * hd_a
    c_ba = c_za + hd_a
    n_head = c_ba + LANES
    c_qb = c_ba + 2 * h_a
    c_kb, c_vb, c_ga, c_gb = c_qb + hd_b, c_qb + 2 * hd_b, c_qb + 3 * hd_b, c_qb + 3 * hd_b + d
    n_tail = 3 * hd_b + 2 * d
    assert c_qb + n_tail == w_t.shape[0] and 2 * h_a <= LANES
    t_qb, t_kb, t_vb, t_ga, t_gb = 0, hd_b, 2 * hd_b, 3 * hd_b, 3 * hd_b + d
    w_up0 = w_up[lyr]
    w_dn = w_down[lyr].astype(BF16)
    w_o_b = w_o[lyr]
    w_ba_b = w_branch_a[lyr]
    w_bb_b = w_branch_b[lyr]

    row = lambda v: v.reshape(1, -1).astype(F32)
    tile_rows = lambda v, reps: jnp.tile(row(v), (1, reps))

    c_all = jnp.concatenate([c_prompt, c_sample], axis=0).astype(F32)
    mp = -(-c_all.shape[0] // 16) * 16
    c_all = jnp.pad(c_all, ((0, mp - c_all.shape[0]), (0, 0)))
    (mod,) = _matmul("adaln", [c_all], [(w_ada[lyr], 0)], [(0, 0)], 6 * d, _ep_bias, [(F32, "tile")],
                     tm=mp, tn=_tile(6 * d, 512, LANES), extras=[(row(b_ada[lyr]), "row", 0)],
                     a_fn=_silu)
    mod_p = [mod[0:1, j * d:(j + 1) * d] for j in range(6)]
    mod_s = [mod[1:1 + nb, j * d:(j + 1) * d] for j in range(6)]

    gq = tile_rows(diff_q_norm_g[lyr], 2 * h_b) * (dq ** -0.5)
    gk = tile_rows(diff_k_norm_g[lyr], 2 * h_b)
    gq2 = gq * math.log2(math.e)
    score_bound = dq * jnp.max(jnp.abs(gq2)) * jnp.max(jnp.abs(gk))
    cw = gdn_conv_w[lyr].astype(F32)
    conv_nflag = jnp.concatenate([jnp.ones((1, 2 * hd_a), F32), jnp.zeros((1, hd_a), F32)], axis=1)
    conv_gain = jnp.concatenate([jnp.full((1, hd_a), dk ** -0.5, F32), jnp.ones((1, 2 * hd_a), F32)], axis=1)
    gnorm = row(gdn_norm_g[lyr])
    subln = row(diff_subln_g[lyr])
    lamp = diff_lambda[lyr].astype(F32)
    fcw = ffn_conv_w[lyr].astype(F32)
    fcb = row(ffn_conv_b[lyr])
    a_log = gdn_a_log[lyr].astype(F32)
    dt_bias = gdn_dt_bias[lyr].astype(F32)

    xp = x_prompt[0].astype(F32)
    tm = _tile(l, TILES["mm_m"], 16)
    tn = TILES["mm_n"]
    split = 4 if tm % 64 == 0 else 1
    split_ffn = split
    h1 = _normmod(xp, row(norm1_g[lyr]), mod_p[1], mod_p[0])
    xs = x_sample[:, 0, :].astype(F32)
    h1s = _normmod(xs, row(norm1_g[lyr]), mod_s[1], mod_s[0])

    stationary = dict(tm=tm, order="nm", cast_once=True, row_split=split, wt=True, side=h1s)
    qkv_c, qkv_tail, zq_s = _matmul(
        "p_qkv_conv", [h1], [(w_t, 0)], [(0, 0)], 3 * hd_a,
        functools.partial(_ep_gdn_conv, width=gw), [(BF16, "tile"), (F32, "tail")],
        tn=_tile(hd_a, tn, LANES), carry_rows=1, **stationary,
        extras=[(cw[j:j + 1], "row", 0) for j in range(gw)] + [(conv_nflag, "row", 0), (conv_gain, "row", 0)])
    za_p, za_s = _matmul("p_za", [h1], [(w_t, c_za)], [(0, 0)], hd_a, _ep_silu, [(BF16, "tile")],
                         tn=_tile(hd_a, tn, LANES), **stationary)
    tn_b = _tile(hd_b, tn, LANES)
    qb_p, qb_s = _matmul("p_qb", [h1], [(w_t, c_qb)], [(0, 0)], hd_b, _ep_qknorm, [(BF16, "tile")],
                         tn=tn_b, extras=[(gq2, "row", 0)], **stationary)
    kb_p, kb_p16, kb_s = _matmul("p_kb", [h1], [(w_t, c_kb)], [(0, 0)], hd_b, _ep_qknorm,
                                 [(F32, "tile"), (BF16, "tile")], tn=tn_b, extras=[(gk, "row", 0)],
                                 **stationary)
    vb_p, vb_p16, vb_s = _matmul("p_vb", [h1], [(w_t, c_vb)], [(0, 0)], hd_b, _ep_copy2,
                                 [(F32, "tile"), (BF16, "tile")], tn=tn_b, **stationary)
    tn_d = _tile(d, tn, LANES)
    gates_p, gates_s = _matmul("p_gates", [h1], [(w_t, c_ga)], [(0, 0)], 2 * d, _ep_sigmoid,
                               [(BF16, "tile")], tn=tn_d, **stationary)

    oa_p, s_p, ba_s = _gdn_prompt(qkv_c, h1, h1s, w_t, c_ba, za_p, a_log, dt_bias, gnorm, h_a, dk)
    ob_p = _attn_prompt(qb_p, kb_p16, vb_p16, score_bound, lamp, subln, h_b, dq, lam_init)

    tns = TILES["small_n"]
    oa_s, s_s, gbuf_s = _gdn_sample(zq_s, za_s, ba_s, state_gdn_conv[lyr].astype(F32),
                                    state_gdn[lyr].astype(F32), cw, a_log, dt_bias, gnorm, h_a, dk)
    qn_s, kn_s, sga_s, sgb_s = pl.pallas_call(
        _sample_post_kernel,
        out_shape=[jax.ShapeDtypeStruct((nb, hd_b), F32), jax.ShapeDtypeStruct((nb, hd_b), F32),
                   jax.ShapeDtypeStruct((nb, d), F32), jax.ShapeDtypeStruct((nb, d), F32)],
        name="sample_post",
    )(qb_s, kb_s, gates_s[:, :d], gates_s[:, d:], gq, gk)
    ob_s = _attn_decode(qn_s, kn_s, vb_s, cache_k, cache_v, page_table.astype(jnp.int32), lamp, subln,
                        h_b, dq, lam_init, lyr)
    tn_ds = _tile(d, tns, LANES)

    mixed_p, ra_s, rb_s = _matmul(
        "p_branch", [oa_p, ob_p], [(w_ba_b, 0), (w_bb_b, 0)], [(0, 0), (1, 1)], d,
        _ep_branch, [(BF16, "tile")], tm=tm, tn=tn_d, row_split=split, order="nm", cast_once=True,
        extras=[(gates_p, "tile", 0), (gates_p, "tile", d)], side=[oa_s[:, 0, :], ob_s[:, 0, :]])
    mixed_s = pl.pallas_call(
        _sample_mix_kernel, out_shape=jax.ShapeDtypeStruct((nb, d), BF16), name="s_mix",
    )(ra_s, rb_s, sga_s, sgb_s)
    x2_p, ro_s = _matmul(
        "p_wo", [mixed_p], [(w_o_b, 0)], [(0, 0)], d, _ep_residual, [(F32, "tile")],
        tm=tm, tn=tn_d, row_split=split, order="nm", cast_once=True,
        extras=[(xp, "tile", 0), (mod_p[2], "row", 0)], side=mixed_s)
    x2_s, h2s = pl.pallas_call(
        _resid_normmod_kernel,
        out_shape=[jax.ShapeDtypeStruct((nb, d), F32), jax.ShapeDtypeStruct((nb, d), BF16)],
        name="s_resid_norm",
    )(xs, ro_s, mod_s[2], row(norm2_g[lyr]), mod_s[4], mod_s[3])
    h2 = _normmod(x2_p, row(norm2_g[lyr]), mod_p[4], mod_p[3])
    tn_u = _tile(d_ff, TILES["up_n"], LANES)
    assert d_ff % tn_u == 0
    ffn_rows = ([(fcw[j:j + 1], "row", 0) for j in range(fw)] + [(fcb, "row", 0)]
                + [(fcw[j:j + 1], "row", d_ff) for j in range(fw)] + [(fcb, "row", d_ff)])
    act_f, tail_g, tail_v, up_g, up_v = _matmul(
        "p_ffn_up", [h2], [(w_up0, 0), (w_up0, d_ff)], [(0, 0), (0, 1)], d_ff,
        functools.partial(_ep_ffn_prompt, width=fw),
        [(BF16, "tile"), (F32, "tail"), (F32, "tail")],
        tm=tm, tn=tn_u, order="nm", carry_rows=2, cast_once=True, row_split=split_ffn, extras=ffn_rows,
        side=h2s)
    tm_dn = _tile(l, TILES["down_m"], 16)
    (y_p,) = _matmul("p_down", [act_f], [(w_dn, 0)], [(0, 0)], d, _ep_residual, [(F32, "tile")],
                     tm=tm_dn, tn=tn_d, row_split=2 if tm_dn % 32 == 0 else 1,
                     extras=[(x2_p, "tile", 0), (mod_p[5], "row", 0)])

    fbuf = state_ffn_conv[lyr].astype(F32)
    tn_f = _tile(d_ff, 8192, LANES)
    nfb = d_ff // tn_f
    half = lambda rows, v: pl.BlockSpec((rows, tn_f), lambda j: (0, j + v * nfb))
    act_s = pl.pallas_call(
        functools.partial(_ffn_sample_kernel, width=fw),
        grid=(nfb,),
        in_specs=([half(nb, 0), half(nb, 0)]
                  + [half(nb, 0) for _ in range(fw - 1)] + [half(nb, 1) for _ in range(fw - 1)]
                  + [half(fw, 0), half(fw, 1), half(1, 0), half(1, 1)]),
        out_specs=half(nb, 0),
        out_shape=jax.ShapeDtypeStruct((nb, d_ff), BF16),
        compiler_params=_params("parallel"),
        name="s_ffn_conv",
    )(up_g, up_v, *[fbuf[:, j, :] for j in range(fw - 1)], *[fbuf[:, j, :] for j in range(fw - 1)],
      fcw, fcw, fcb, fcb)
    (y_s,) = _matmul("s_down", [act_s], [(w_dn, 0)], [(0, 0)], d, _ep_residual, [(F32, "tile")],
                     tm=nb, tn=tn_ds, tk=_tile(d_ff, TILES["down_k"], LANES),
                     extras=[(x2_s, "tile", 0), (mod_s[5], "tile", 0)])

    y_prompt = y_p[None].astype(x_prompt.dtype)
    y_sample = y_s[:, None, :].astype(x_sample.dtype)
    k_prompt = kb_p.reshape(1, 1, l, h_b, 2 * dq)
    v_prompt = vb_p.reshape(1, 1, l, h_b, dvb)
    gdn_state_prompt = s_p[None, None]
    gdn_conv_prompt = qkv_tail[SUBLANES - (gw - 1):][None, None]
    ffn_conv_prompt = jnp.concatenate([tail_g[SUBLANES - (fw - 1):], tail_v[SUBLANES - (fw - 1):]],
                                      axis=1)[None, None]
    k_sample = kn_s.reshape(1, nb, 1, h_b, 2 * dq)
    v_sample = vb_s.reshape(1, nb, 1, h_b, dvb)
    gdn_state_sample = s_s[None]
    gdn_conv_sample = gbuf_s[None]
    up_new = jnp.concatenate([up_g, up_v], axis=1)
    ffn_conv_sample = jnp.concatenate([fbuf[:, 1:, :], up_new[:, None, :]], axis=1)[None]
    return (y_prompt, y_sample, k_prompt, v_prompt, gdn_state_prompt, gdn_conv_prompt,
            ffn_conv_prompt, k_sample, v_sample, gdn_state_sample, gdn_conv_sample, ffn_conv_sample)
```
